```python
import math
import jax, jax.numpy as jnp
from jax import lax
import numpy as np

D_MODEL = 1024
BATCH = 16
SEQ = 2048
DEPTH = 1

GRID_W = 64
A_HEADS = 8
A_HEAD_DIM = 64
D_A = A_HEADS * A_HEAD_DIM
W_LORA = 64
A_LORA = 64
G_LORA = 128
DECAY_SCALE = math.exp(-0.5)
GN_EPS = 64e-5
B_HEADS = 8
B_HEAD_DIM = 64
D_B = B_HEADS * B_HEAD_DIM
NA_WR_MAX = 8
NA_WC = 16
NA_QB = 16
NA_KB = NA_QB + NA_WC
N_EXPERTS = 32
TOP_K = 4
D_EXPERT = D_MODEL
SWIGLU_LIMIT = 7.0
SWIGLU_ALPHA = 1.702
MOE_BLOCK = 512
RMS_EPS = 1e-5

A_COLS = 3 * D_A + 2 * W_LORA + 2 * A_LORA + G_LORA
B_COLS = 3 * D_B
IN_COLS = A_COLS + B_COLS + 2 * D_MODEL

kernel_name = "hybrid_rwkv7_natten_moe_block"


def rmsnorm(x, g):
    xf = x.astype(jnp.float32)
    y = xf * lax.rsqrt(jnp.mean(xf * xf, axis=-1, keepdims=True) + RMS_EPS) * g.astype(jnp.float32)
    return y.astype(x.dtype)


def head_rmsnorm(t, g):
    tf = t.astype(jnp.float32)
    return tf * lax.rsqrt(jnp.mean(tf * tf, axis=-1, keepdims=True) + RMS_EPS) * g.astype(jnp.float32)


def centred_token_shift(p, mu_prev, mu_next):
    prev = jnp.pad(p[:, :-1], ((0, 0), (1, 0), (0, 0)))
    nxt = jnp.pad(p[:, 1:], ((0, 0), (0, 1), (0, 0)))
    return p + mu_prev * (prev - p) + mu_next * (nxt - p)


def rwkv7_scan(r, w, k, v, kk, a, reverse):
    B, T, H, N = r.shape
    xs = tuple(t.transpose(1, 0, 2, 3) for t in (r, w, k, v, kk, a))

    def step(S, inp):
        r_t, w_t, k_t, v_t, kk_t, a_t = inp
        sa = jnp.einsum('bhvk,bhk->bhv', S, -kk_t)
        S = (S * w_t[:, :, None, :]
             + sa[..., None] * (kk_t * a_t)[:, :, None, :]
             + v_t[..., None] * k_t[:, :, None, :])
        return S, jnp.einsum('bhvk,bhk->bhv', S, r_t)

    S0 = jnp.zeros((B, H, N, N), jnp.float32)
    _, o = lax.scan(step, S0, xs, reverse=reverse)
    return o.transpose(1, 0, 2, 3)


def rwkv7_branch(pa, mu_prev, mu_next, w0_f, w2_f, w0_b, w2_b, a0_f, a2_f, a0_b, a2_b,
                 g2, k_k, k_a, r_k, lnx_g, lnx_b):
    B, T, _ = pa.shape
    xa = centred_token_shift(pa, mu_prev, mu_next)
    split_at = np.cumsum([D_A, D_A, D_A, W_LORA, W_LORA, A_LORA, A_LORA]).tolist()
    r, k, v, lw_f, lw_b, la_f, la_b, lg = jnp.split(xa, split_at, axis=-1)

    def heads(t):
        return t.reshape(B, T, A_HEADS, A_HEAD_DIM).astype(jnp.float32)

    g = jax.nn.sigmoid(lg) @ g2
    kk = heads(k * k_k)
    kk = kk / jnp.maximum(jnp.sqrt(jnp.sum(kk * kk, axis=-1, keepdims=True)), 1e-12)
    r_h, v_h = heads(r), heads(v)

    outs, bonus = [], []
    for w0, w2, a0, a2, lw, la, rev in ((w0_f, w2_f, a0_f, a2_f, lw_f, la_f, False),
                                        (w0_b, w2_b, a0_b, a2_b, lw_b, la_b, True)):
        decay = jnp.exp(-DECAY_SCALE * jax.nn.sigmoid((w0 + jnp.tanh(lw) @ w2).astype(jnp.float32)))
        a = jax.nn.sigmoid(a0 + la @ a2)
        k_dir = heads(k * (1.0 + (a - 1.0) * k_a))
        a_h = heads(a)
        outs.append(rwkv7_scan(r_h, decay.reshape(B, T, A_HEADS, A_HEAD_DIM), k_dir, v_h, kk, a_h, rev))
        bonus.append(jnp.sum(r_h * k_dir * r_k.astype(jnp.float32), axis=-1, keepdims=True) * v_h)

    o = outs[0] + outs[1]
    mu = jnp.mean(o, axis=-1, keepdims=True)
    var = jnp.mean(jnp.square(o - mu), axis=-1, keepdims=True)
    y = ((o - mu) * lax.rsqrt(var + GN_EPS)).reshape(B, T, D_A) * lnx_g + lnx_b
    y = y + (bonus[0] + bonus[1]).reshape(B, T, D_A)
    return (y * g).astype(pa.dtype)


def na_branch(pb, q_gain, k_gain, rpb):
    B, T, _ = pb.shape
    rows = T // GRID_W
    wr = min(NA_WR_MAX, rows)
    n_cb = GRID_W // NA_QB
    q, k, v = jnp.split(pb, 3, axis=-1)

    def to_grid(t):
        return t.reshape(B, rows, GRID_W, B_HEADS, B_HEAD_DIM).transpose(0, 3, 1, 2, 4)

    qg = to_grid(head_rmsnorm(q.reshape(B, T, B_HEADS, B_HEAD_DIM), q_gain))
    kg = to_grid(head_rmsnorm(k.reshape(B, T, B_HEADS, B_HEAD_DIM), k_gain))
    vg = to_grid(v.reshape(B, T, B_HEADS, B_HEAD_DIM))

    qcols = np.arange(GRID_W).reshape(n_cb, NA_QB)
    cs = np.clip(qcols - NA_WC // 2, 0, GRID_W - NA_WC)
    kc0 = np.minimum(cs[:, 0], GRID_W - NA_KB)
    kcols = kc0[:, None] + np.arange(NA_KB)
    col_mask = ((kcols[:, None, :] >= cs[..., None]) &
                (kcols[:, None, :] < cs[..., None] + NA_WC))
    dc_idx = np.clip(kcols[:, None, :] - qcols[..., None] + NA_WC - 1, 0, 2 * NA_WC - 2)
    bias_c = rpb.astype(jnp.float32)[:, :, dc_idx]
    mask = jnp.asarray(col_mask)[:, :, None, :]
    scale = B_HEAD_DIM ** -0.5

    def row_step(r):
        rs = jnp.clip(r - wr // 2, 0, rows - wr)
        k_blk = lax.dynamic_slice_in_dim(kg, rs, wr, axis=2)[:, :, :, kcols]
        v_blk = lax.dynamic_slice_in_dim(vg, rs, wr, axis=2)[:, :, :, kcols]
        q_row = lax.dynamic_index_in_dim(qg, r, axis=2, keepdims=False).reshape(
            B, B_HEADS, n_cb, NA_QB, B_HEAD_DIM)
        s = jnp.einsum('bhcqd,bhrckd->bhcqrk', q_row, k_blk) * scale
        dr_idx = rs + jnp.arange(wr) - r + NA_WR_MAX - 1
        s = s + jnp.take(bias_c, dr_idx, axis=1).transpose(0, 2, 3, 1, 4)
        s = jnp.where(mask, s, -jnp.inf)
        p = jax.nn.softmax(s.reshape(B, B_HEADS, n_cb, NA_QB, wr * NA_KB), axis=-1).reshape(s.shape)
        return jnp.einsum('bhcqrk,bhrckd->bhcqd', p.astype(v_blk.dtype), v_blk)

    o = lax.map(row_step, jnp.arange(rows))
    return o.transpose(1, 0, 3, 4, 2, 5).reshape(B, T, D_B).astype(pb.dtype)


def moe_ffn(h, w_router, b_router, w1, b1, w2, b2):
    B, T, D = h.shape
    M = B * T
    hf = h.reshape(M, D)
    logits = (hf @ w_router).astype(jnp.float32) + b_router.astype(jnp.float32)
    top_vals, top_idx = lax.top_k(logits, TOP_K)
    gate_w = jax.nn.softmax(top_vals, axis=-1)

    A = M * TOP_K
    flat_e = top_idx.reshape(A)
    flat_tok = jnp.arange(A, dtype=jnp.int32) // TOP_K
    flat_w = gate_w.reshape(A)
    order = jnp.argsort(flat_e)
    sorted_e = flat_e[order]
    counts = jnp.bincount(flat_e, length=N_EXPERTS)
    padded = ((counts + MOE_BLOCK - 1) // MOE_BLOCK) * MOE_BLOCK
    start = jnp.cumsum(counts) - counts
    pend = jnp.cumsum(padded)
    pstart = pend - padded
    dest = pstart[sorted_e] + (jnp.arange(A) - start[sorted_e])
    n_blocks = -(-A // MOE_BLOCK) + N_EXPERTS
    n_pad = n_blocks * MOE_BLOCK
    tok_buf = jnp.zeros((n_pad,), jnp.int32).at[dest].set(flat_tok[order])
    w_buf = jnp.zeros((n_pad,), jnp.float32).at[dest].set(flat_w[order])
    block_e = jnp.minimum(jnp.searchsorted(pend, jnp.arange(n_blocks) * MOE_BLOCK, side='right'),
                          N_EXPERTS - 1)

    def expert_block(args):
        e, toks = args
        u = hf[toks] @ w1[e] + b1[e]
        glu = jnp.minimum(u[:, :D_EXPERT], SWIGLU_LIMIT)
        lin = jnp.clip(u[:, D_EXPERT:], -SWIGLU_LIMIT, SWIGLU_LIMIT)
        act = glu * jax.nn.sigmoid(SWIGLU_ALPHA * glu) * (lin + 1.0)
        return act @ w2[e] + b2[e]

    y = lax.map(expert_block, (block_e, tok_buf.reshape(n_blocks, MOE_BLOCK)))
    y = y.reshape(n_pad, D).astype(jnp.float32) * w_buf[:, None]
    out = jnp.zeros((M, D), jnp.float32).at[tok_buf].add(y)
    return out.astype(h.dtype).reshape(B, T, D)


def setup_inputs(seed: int = 0) -> dict:
    key = jax.random.key(seed)
    ks = iter(jax.random.split(key, 40))
    f32 = jnp.float32
    L = DEPTH

    def nrm(shape, scale):
        return jax.random.normal(next(ks), shape, f32) * scale

    def uni(shape, lo, hi):
        return jax.random.uniform(next(ks), shape, f32, lo, hi)

    return {
        "x": nrm((BATCH, SEQ, D_MODEL), 1.0),
        "g_mix": 1.0 + nrm((L, D_MODEL), 0.02),
        "w_in": nrm((L, D_MODEL, IN_COLS), D_MODEL ** -0.5),
        "mu_prev": uni((L, A_COLS), 0.0, 0.5),
        "mu_next": uni((L, A_COLS), 0.0, 0.5),
        "w0_f": nrm((L, D_A), 1.0),
        "w2_f": nrm((L, W_LORA, D_A), 0.1),
        "w0_b": nrm((L, D_A), 1.0),
        "w2_b": nrm((L, W_LORA, D_A), 0.1),
        "a0_f": nrm((L, D_A), 0.5),
        "a2_f": nrm((L, A_LORA, D_A), 0.1),
        "a0_b": nrm((L, D_A), 0.5),
        "a2_b": nrm((L, A_LORA, D_A), 0.1),
        "g2": nrm((L, G_LORA, D_A), G_LORA ** -0.5),
        "k_k": 0.85 + nrm((L, D_A), 0.02),
        "k_a": 1.0 + nrm((L, D_A), 0.02),
        "r_k": nrm((L, A_HEADS, A_HEAD_DIM), 0.1),
        "lnx_g": 1.0 + nrm((L, D_A), 0.02),
        "lnx_b": nrm((L, D_A), 0.01),
        "q_norm_g": 1.0 + nrm((L, B_HEAD_DIM), 0.02),
        "k_norm_g": 1.0 + nrm((L, B_HEAD_DIM), 0.02),
        "rpb": nrm((L, B_HEADS, 2 * NA_WR_MAX - 1, 2 * NA_WC - 1), 0.1),
        "w_a": nrm((L, D_A, D_MODEL), D_A ** -0.5),
        "w_b": nrm((L, D_B, D_MODEL), D_B ** -0.5),
        "w_o": nrm((L, D_MODEL, D_MODEL), D_MODEL ** -0.5),
        "g_ffn": 1.0 + nrm((L, D_MODEL), 0.02),
        "w_router": nrm((L, D_MODEL, N_EXPERTS), D_MODEL ** -0.5),
        "b_router": nrm((L, N_EXPERTS), 0.01),
        "w1": nrm((L, N_EXPERTS, D_MODEL, 2 * D_EXPERT), D_MODEL ** -0.5),
        "b1": nrm((L, N_EXPERTS, 2 * D_EXPERT), 0.01),
        "w2": nrm((L, N_EXPERTS, D_EXPERT, D_MODEL), D_EXPERT ** -0.5),
        "b2": nrm((L, N_EXPERTS, D_MODEL), 0.01),
    }


def reference(x, g_mix, w_in, mu_prev, mu_next, w0_f, w2_f, w0_b, w2_b, a0_f, a2_f, a0_b, a2_b,
              g2, k_k, k_a, r_k, lnx_g, lnx_b, q_norm_g, k_norm_g, rpb, w_a, w_b, w_o,
              g_ffn, w_router, b_router, w1, b1, w2, b2):
    for l in range(DEPTH):
        h = rmsnorm(x, g_mix[l])
        p = h @ w_in[l]
        pa, pb, gates = jnp.split(p, [A_COLS, A_COLS + B_COLS], axis=-1)
        ya = rwkv7_branch(pa, mu_prev[l], mu_next[l], w0_f[l], w2_f[l], w0_b[l], w2_b[l],
                          a0_f[l], a2_f[l], a0_b[l], a2_b[l], g2[l], k_k[l], k_a[l], r_k[l],
                          lnx_g[l], lnx_b[l])
        yb = na_branch(pb, q_norm_g[l], k_norm_g[l], rpb[l])
        gate_a, gate_b = jnp.split(gates, 2, axis=-1)
        merged = jax.nn.sigmoid(gate_a) * (ya @ w_a[l]) + jax.nn.sigmoid(gate_b) * (yb @ w_b[l])
        x = x + merged @ w_o[l]
        x = x + moe_ffn(rmsnorm(x, g_ffn[l]), w_router[l], b_router[l], w1[l], b1[l], w2[l], b2[l])
    return x
```

```python
import functools
import math

import numpy as np
import jax
import jax.numpy as jnp
from jax import lax
from jax.experimental import pallas as pl
from jax.experimental.pallas import tpu as pltpu

F32 = jnp.float32
BF16 = jnp.bfloat16

LANES = 128
HEAD_DIM = 64
PAIR = 2 * HEAD_DIM
GRID_W = 64
NA_WR = 8
NA_WC = 16
W_LORA = 64
A_LORA = 64
G_LORA = 128
DECAY_SCALE = math.exp(-0.5)
GN_EPS = 64e-5
RMS_EPS = 1e-5
N_EXPERTS = 32
TOP_K = 4
MOE_BLOCK = 512
SWIGLU_LIMIT = 7.0
SWIGLU_ALPHA = 1.702
NEG_BIG = -1e30
CHUNK = 128
VMEM_LIMIT = 56 * 1024 * 1024


def _dot(a, b):
    return jnp.dot(a, b, preferred_element_type=F32)


def _dot_nt(a, b):
    return lax.dot_general(a, b, (((1,), (1,)), ((), ())), preferred_element_type=F32)


def _split(a):
    hi = a.astype(BF16)
    lo = (a - hi.astype(F32)).astype(BF16)
    return hi, lo


def _mm_exact_rhs(a, b_bf16):
    hi, lo = _split(a)
    return _dot(hi, b_bf16) + _dot(lo, b_bf16)


def _mm3(a, b):
    ah, al = _split(a)
    bh, bl = _split(b)
    return _dot(ah, bh) + _dot(al, bh) + _dot(ah, bl)


def _sigmoid(x):
    return 1.0 / (1.0 + jnp.exp(-x))


def _params(sem):
    return pltpu.CompilerParams(dimension_semantics=sem, vmem_limit_bytes=VMEM_LIMIT)


def _block_ones(n, blk):
    i = np.arange(n) // blk
    return jnp.asarray(i[:, None] == i[None, :], BF16)


def _inproj_kernel(x_ref, g_ref, wa_ref, wb_ref, wg_ref, pa_ref, pb_ref, pg_ref):
    x = x_ref[...]
    ms = jnp.mean(x * x, axis=-1, keepdims=True)
    h = (x * lax.rsqrt(ms + RMS_EPS) * g_ref[...]).astype(BF16)
    pa_ref[...] = _dot(h, wa_ref[...])
    pb_ref[...] = _dot(h, wb_ref[...])
    pg_ref[...] = _dot(h, wg_ref[...])


def _inproj(x2, g_mix, w_a, w_b, w_g, tm=256):
    m, d = x2.shape
    na, nb, ng = w_a.shape[1], w_b.shape[1], w_g.shape[1]
    full = lambda i: (0, 0)
    return pl.pallas_call(
        _inproj_kernel,
        grid=(m // tm,),
        in_specs=[
            pl.BlockSpec((tm, d), lambda i: (i, 0)),
            pl.BlockSpec((1, d), full),
            pl.BlockSpec((d, na), full),
            pl.BlockSpec((d, nb), full),
            pl.BlockSpec((d, ng), full),
        ],
        out_specs=[
            pl.BlockSpec((tm, na), lambda i: (i, 0)),
            pl.BlockSpec((tm, nb), lambda i: (i, 0)),
            pl.BlockSpec((tm, ng), lambda i: (i, 0)),
        ],
        out_shape=[
            jax.ShapeDtypeStruct((m, na), F32),
            jax.ShapeDtypeStruct((m, nb), F32),
            jax.ShapeDtypeStruct((m, ng), F32),
        ],
        compiler_params=_params(("arbitrary",)),
        name="inproj",
    )(x2, g_mix, w_a, w_b, w_g)


def _prep_kernel(p_ref, prev_ref, next_ref, mup_ref, mun_ref, w0_ref, w2_ref, a0_ref, a2_ref,
                 g2_ref, kk_ref, ka_ref, rk_ref, ones_ref,
                 r_o, v_o, kk_o, lwf_o, lwb_o, kf_o, kb_o, bf_o, bb_o, bonus_o, g_o, *, d_a):
    i = pl.program_id(1)
    n_t = pl.num_programs(1)
    p = p_ref[...]
    tt = p.shape[0]
    prow = jnp.where(i > 0, prev_ref[7:8, :], 0.0)
    nrow = jnp.where(i < n_t - 1, next_ref[0:1, :], 0.0)
    rid = lax.broadcasted_iota(jnp.int32, (tt, 1), 0)
    prev = jnp.where(rid == 0, prow, pltpu.roll(p, 1, axis=0))
    nxt = jnp.where(rid == tt - 1, nrow, pltpu.roll(p, tt - 1, axis=0))
    xa = p + mup_ref[...] * (prev - p) + mun_ref[...] * (nxt - p)

    r = xa[:, 0:d_a]
    k = xa[:, d_a:2 * d_a]
    v = xa[:, 2 * d_a:3 * d_a]
    o = 3 * d_a
    lw = xa[:, o:o + 2 * W_LORA]
    la = xa[:, o + 2 * W_LORA:o + 2 * W_LORA + 2 * A_LORA]
    lg = xa[:, o + 2 * W_LORA + 2 * A_LORA:]

    dpre = w0_ref[...] + _mm3(jnp.tanh(lw), w2_ref[...])
    apre = a0_ref[...] + _mm3(la, a2_ref[...])
    g = _mm3(_sigmoid(lg), g2_ref[...])
    logw = -DECAY_SCALE * _sigmoid(dpre)
    a = _sigmoid(apre)

    ones = ones_ref[...]
    kkr = k * kk_ref[...]
    ss = _mm_exact_rhs(kkr * kkr, ones)
    kk = kkr / jnp.maximum(jnp.sqrt(ss), 1e-12)

    ka = ka_ref[...]
    k_f = k * (1.0 + (a[:, :d_a] - 1.0) * ka)
    k_b = k * (1.0 + (a[:, d_a:] - 1.0) * ka)
    b_f = kk * a[:, :d_a]
    b_b = kk * a[:, d_a:]
    rk = rk_ref[...]
    bon = _mm_exact_rhs(r * (k_f + k_b) * rk, ones) * v

    bonus_o[...] = bon
    g_o[...] = g
    for pi in range(d_a // PAIR):
        sl = slice(pi * PAIR, (pi + 1) * PAIR)
        r_o[pi] = r[:, sl]
        v_o[pi] = v[:, sl]
        kk_o[pi] = kk[:, sl]
        lwf_o[pi] = logw[:, sl]
        lwb_o[pi] = logw[:, d_a + pi * PAIR:d_a + (pi + 1) * PAIR]
        kf_o[pi] = k_f[:, sl]
        kb_o[pi] = k_b[:, sl]
        bf_o[pi] = b_f[:, sl]
        bb_o[pi] = b_b[:, sl]


def _prep(pa3, mu_prev, mu_next, w0c, w2blk, a0c, a2blk, g2, k_k, k_a, r_k, d_a, tt=256):
    b, t, ac = pa3.shape
    n_t = t // tt
    n_p = d_a // PAIR
    ones = _block_ones(d_a, HEAD_DIM)
    c2 = lambda bi, i: (0, 0)
    pair_spec = pl.BlockSpec((None, n_p, tt, PAIR), lambda bi, i: (bi, 0, i, 0))
    pair_shape = jax.ShapeDtypeStruct((b, n_p, t, PAIR), F32)
    flat_spec = pl.BlockSpec((None, tt, d_a), lambda bi, i: (bi, i, 0))
    flat_shape = jax.ShapeDtypeStruct((b, t, d_a), F32)
    r8 = tt // 8
    return pl.pallas_call(
        functools.partial(_prep_kernel, d_a=d_a),
        grid=(b, n_t),
        in_specs=[
            pl.BlockSpec((None, tt, ac), lambda bi, i: (bi, i, 0)),
            pl.BlockSpec((None, 8, ac), lambda bi, i: (bi, jnp.maximum(i * r8 - 1, 0), 0)),
            pl.BlockSpec((None, 8, ac), lambda bi, i: (bi, jnp.minimum((i + 1) * r8, t // 8 - 1), 0)),
            pl.BlockSpec((1, ac), c2),
            pl.BlockSpec((1, ac), c2),
            pl.BlockSpec((1, 2 * d_a), c2),
            pl.BlockSpec((2 * W_LORA, 2 * d_a), c2),
            pl.BlockSpec((1, 2 * d_a), c2),
            pl.BlockSpec((2 * A_LORA, 2 * d_a), c2),
            pl.BlockSpec((G_LORA, d_a), c2),
            pl.BlockSpec((1, d_a), c2),
            pl.BlockSpec((1, d_a), c2),
            pl.BlockSpec((1, d_a), c2),
            pl.BlockSpec((d_a, d_a), c2),
        ],
        out_specs=[pair_spec] * 9 + [flat_spec] * 2,
        out_shape=[pair_shape] * 9 + [flat_shape] * 2,
        compiler_params=_params(("arbitrary", "arbitrary")),
        name="rwkv_prep",
    )(pa3, pa3, pa3, mu_prev, mu_next, w0c, w2blk, a0c, a2blk, g2, k_k, k_a, r_k, ones)


def _scan_chunk(r, kk, v, lw, k, b, s_ref, reverse):
    c_len = r.shape[0]
    ri = lax.broadcasted_iota(jnp.int32, (c_len, c_len), 0)
    ci = lax.broadcasted_iota(jnp.int32, (c_len, c_len), 1)
    if reverse:
        incl = ci >= ri
        strict = ci > ri
        end = 0
    else:
        incl = ci <= ri
        strict = ci < ri
        end = c_len - 1
    lane = lax.broadcasted_iota(jnp.int32, (1, PAIR), 1)
    m0 = (lane < HEAD_DIM).astype(F32)
    m1 = 1.0 - m0

    cum = _mm_exact_rhs_left(incl.astype(BF16), lw)
    cmid = cum[c_len // 2:c_len // 2 + 1, :]
    cend = cum[end:end + 1, :]
    e_abs = jnp.exp(cum)
    r_abs = r * e_abs
    a_abs = -kk * jnp.exp(cum - lw)
    to_mid = jnp.exp(-cmid)
    r_rel = r_abs * to_mid
    a_rel = a_abs * to_mid
    from_mid = jnp.exp(cmid - cum)
    k_rel = k * from_mid
    b_rel = b * from_mid
    to_end = jnp.exp(cend - cum)
    k_end = k * to_end
    b_end = b * to_end
    d_tot = jnp.exp(cend)

    lhs = jnp.concatenate([r_rel * m0, r_rel * m1, a_rel * m0, a_rel * m1], axis=0).astype(BF16)
    rhs = jnp.concatenate([k_rel, b_rel], axis=0).astype(BF16)
    gram = _dot_nt(lhs, rhs)
    c = c_len
    zero = jnp.zeros((), F32)
    a_rk = [jnp.where(incl, gram[e * c:(e + 1) * c, 0:c], zero) for e in range(2)]
    a_rb = [jnp.where(incl, gram[e * c:(e + 1) * c, c:2 * c], zero) for e in range(2)]
    a_ak = [jnp.where(strict, gram[(2 + e) * c:(3 + e) * c, 0:c], zero) for e in range(2)]
    a_ab = [jnp.where(strict, gram[(2 + e) * c:(3 + e) * c, c:2 * c], zero) for e in range(2)]

    eye = (ri == ci).astype(F32)
    t_inv = []
    for e in range(2):
        pw = a_ab[e]
        t = eye + pw
        n_sq = int(round(math.log2(c_len))) - 1
        for _ in range(n_sq):
            pwb = pw.astype(BF16)
            pw = _dot(pwb, pwb)
            t = t + _dot(t.astype(BF16), pw.astype(BF16))
        t_inv.append(t)

    v_blk = jnp.concatenate([v * m0, v * m1], axis=0).astype(BF16)
    ak_cat = jnp.concatenate(a_ak, axis=1).astype(BF16)
    akv = _dot(ak_cat, v_blk)
    rk_cat = jnp.concatenate(a_rk, axis=1).astype(BF16)
    o_rk = _dot(rk_cat, v_blk)
    t_cat = jnp.concatenate(t_inv, axis=1).astype(BF16)
    y_blk = jnp.concatenate([
        jnp.concatenate([a_abs * m0, akv * m0], axis=1),
        jnp.concatenate([a_abs * m1, akv * m1], axis=1)], axis=0).astype(BF16)
    x = _dot(t_cat, y_blk)
    w_til = x[:, 0:PAIR]
    u_til = x[:, PAIR:2 * PAIR]

    s0 = s_ref[...]
    s0b = s0.astype(BF16)
    u = _dot(w_til.astype(BF16), s0b) + u_til
    u_blk = jnp.concatenate([u * m0, u * m1], axis=0).astype(BF16)
    rb_cat = jnp.concatenate(a_rb, axis=1).astype(BF16)
    o = _dot(r_abs.astype(BF16), s0b) + _dot(rb_cat, u_blk) + o_rk

    kb_t = jnp.concatenate([b_end.T, k_end.T], axis=1).astype(BF16)
    uv = jnp.concatenate([u, v], axis=0).astype(BF16)
    d_col = jnp.broadcast_to(d_tot, (PAIR, PAIR)).T
    hi = lax.broadcasted_iota(jnp.int32, (PAIR, PAIR), 0) // HEAD_DIM
    hj = lax.broadcasted_iota(jnp.int32, (PAIR, PAIR), 1) // HEAD_DIM
    s_new = jnp.where(hi == hj, d_col * s0 + _dot(kb_t, uv), zero)
    s_ref[...] = s_new
    return o


def _mm_exact_rhs_left(tri_bf16, x):
    hi, lo = _split(x)
    return _dot(tri_bf16, hi) + _dot(tri_bf16, lo)


def _scan_kernel(rf, vf, kkf, lwf, kf, bf, rb, vb, kkb, lwb, kb, bb, of_ref, ob_ref, sf_ref, sb_ref):
    ci = pl.program_id(2)

    @pl.when(ci == 0)
    def _():
        sf_ref[...] = jnp.zeros_like(sf_ref)
        sb_ref[...] = jnp.zeros_like(sb_ref)

    of_ref[...] = _scan_chunk(rf[...], kkf[...], vf[...], lwf[...], kf[...], bf[...], sf_ref, False)
    ob_ref[...] = _scan_chunk(rb[...], kkb[...], vb[...], lwb[...], kb[...], bb[...], sb_ref, True)


def _scan(r, v, kk, lw_f, lw_b, k_f, k_b, b_f, b_b):
    bsz, n_p, t, _ = r.shape
    nc = t // CHUNK
    fwd = pl.BlockSpec((None, None, CHUNK, PAIR), lambda bi, p, c: (bi, p, c, 0))
    bwd = pl.BlockSpec((None, None, CHUNK, PAIR), lambda bi, p, c: (bi, p, nc - 1 - c, 0))
    shape = jax.ShapeDtypeStruct((bsz, n_p, t, PAIR), F32)
    return pl.pallas_call(
        _scan_kernel,
        grid=(bsz, n_p, nc),
        in_specs=[fwd] * 6 + [bwd] * 6,
        out_specs=[fwd, bwd],
        out_shape=[shape, shape],
        scratch_shapes=[pltpu.VMEM((PAIR, PAIR), F32), pltpu.VMEM((PAIR, PAIR), F32)],
        compiler_params=_params(("arbitrary", "arbitrary", "arbitrary")),
        name="rwkv_scan",
    )(r, v, kk, lw_f, k_f, b_f, r, v, kk, lw_b, k_b, b_b)


def _na_kernel(q_ref, k_ref, v_ref, gq_ref, gk_ref, tab_ref, ones_ref, o_ref, qn_s, kn_s, vb_s, *, rows):
    ones = ones_ref[...]
    scale = HEAD_DIM ** -0.5
    q = q_ref[...]
    k = k_ref[...]
    inv_d = 1.0 / HEAD_DIM
    qn = q * lax.rsqrt(_mm_exact_rhs(q * q, ones) * inv_d + RMS_EPS) * (gq_ref[...] * scale)
    kn = k * lax.rsqrt(_mm_exact_rhs(k * k, ones) * inv_d + RMS_EPS) * gk_ref[...]
    qn_s[...] = qn.astype(BF16)
    kn_s[...] = kn.astype(BF16)
    vb_s[...] = v_ref[...].astype(BF16)
    lane = lax.broadcasted_iota(jnp.int32, (1, PAIR), 1)
    head0 = lane < HEAD_DIM
    win = NA_WR * GRID_W

    def row(r, carry):
        rs = jnp.clip(r - NA_WR // 2, 0, rows - NA_WR)
        d0 = rs - r + NA_WR - 1
        q_row = qn_s[pl.ds(pl.multiple_of(r * GRID_W, GRID_W), GRID_W), :]
        k_win = kn_s[pl.ds(pl.multiple_of(rs * GRID_W, GRID_W), win), :]
        v_win = vb_s[pl.ds(pl.multiple_of(rs * GRID_W, GRID_W), win), :]
        outs = []
        for e in range(2):
            mask = head0 if e == 0 else jnp.logical_not(head0)
            qm = jnp.where(mask, q_row, jnp.zeros_like(q_row))
            s = _dot_nt(qm, k_win)
            bias = jnp.concatenate(
                [tab_ref[e, pl.ds(d0 + 2 * m, 1)][0] for m in range(NA_WR // 2)], axis=1)
            s = s + bias
            mx = jnp.max(s, axis=-1, keepdims=True)
            p = jnp.exp(s - mx)
            l = jnp.sum(p, axis=-1, keepdims=True)
            outs.append(_dot(p.astype(BF16), v_win) / l)
        o_ref[pl.ds(pl.multiple_of(r * GRID_W, GRID_W), GRID_W), :] = jnp.where(head0, outs[0], outs[1])
        return carry

    lax.fori_loop(0, rows, row, 0)


def _na_bias_table(rpb):
    qc = np.arange(GRID_W)
    kc = np.arange(GRID_W)
    cs = np.clip(qc - NA_WC // 2, 0, GRID_W - NA_WC)
    valid = (kc[None, :] >= cs[:, None]) & (kc[None, :] < cs[:, None] + NA_WC)
    dc = np.clip(kc[None, :] - qc[:, None] + NA_WC - 1, 0, 2 * NA_WC - 2)
    b = rpb.astype(F32)[:, :, dc]
    b = jnp.where(jnp.asarray(valid)[None, None], b, NEG_BIG)
    return jnp.concatenate([b[:, :-1], b[:, 1:]], axis=-1)


def _na(pb3, q_gain, k_gain, table, d_b):
    bsz, t, _ = pb3.shape
    rows = t // GRID_W
    n_p = d_b // PAIR
    ones = _block_ones(PAIR, HEAD_DIM)
    gq = jnp.tile(q_gain.reshape(1, HEAD_DIM), (1, 2))
    gk = jnp.tile(k_gain.reshape(1, HEAD_DIM), (1, 2))
    n_d = table.shape[1]
    c2 = lambda bi, p: (0, 0)
    return pl.pallas_call(
        functools.partial(_na_kernel, rows=rows),
        grid=(bsz, n_p),
        in_specs=[
            pl.BlockSpec((None, t, PAIR), lambda bi, p: (bi, 0, p)),
            pl.BlockSpec((None, t, PAIR), lambda bi, p: (bi, 0, n_p + p)),
            pl.BlockSpec((None, t, PAIR), lambda bi, p: (bi, 0, 2 * n_p + p)),
            pl.BlockSpec((1, PAIR), c2),
            pl.BlockSpec((1, PAIR), c2),
            pl.BlockSpec((2, n_d, GRID_W, PAIR), lambda bi, p: (p, 0, 0, 0)),
            pl.BlockSpec((PAIR, PAIR), c2),
        ],
        out_specs=pl.BlockSpec((None, t, PAIR), lambda bi, p: (bi, 0, p)),
        out_shape=jax.ShapeDtypeStruct((bsz, t, d_b), F32),
        scratch_shapes=[pltpu.VMEM((t, PAIR), BF16)] * 3,
        compiler_params=_params(("arbitrary", "arbitrary")),
        name="natten",
    )(pb3, pb3, pb3, gq, gk, table, ones)


def _merge_kernel(of_ref, ob_ref, bonus_ref, g_ref, yb_ref, gates_ref, x_ref,
                  lng_ref, lnb_ref, wa_ref, wb_ref, wo_ref, gffn_ref, wr_ref, br_ref,
                  ones_ref, tri_ref,
                  x1_ref, h2_ref, idx_ref, rank_ref, gw_ref, cnt_ref, carry_ref, *, d_model):
    first = jnp.logical_and(pl.program_id(0) == 0, pl.program_id(1) == 0)

    @pl.when(first)
    def _():
        carry_ref[...] = jnp.zeros_like(carry_ref)

    n_p = of_ref.shape[0]
    o = jnp.concatenate([of_ref[p] + ob_ref[p] for p in range(n_p)], axis=1)
    ones = ones_ref[...]
    inv_d = 1.0 / HEAD_DIM
    mu = _mm_exact_rhs(o, ones) * inv_d
    dv = o - mu
    var = _mm_exact_rhs(dv * dv, ones) * inv_d
    y = dv * lax.rsqrt(var + GN_EPS) * lng_ref[...] + lnb_ref[...] + bonus_ref[...]
    ya = y * g_ref[...]

    gates = gates_ref[...]
    pa = _dot(ya.astype(BF16), wa_ref[...])
    pb = _dot(yb_ref[...].astype(BF16), wb_ref[...])
    merged = _sigmoid(gates[:, :d_model]) * pa + _sigmoid(gates[:, d_model:]) * pb
    x1 = x_ref[...] + _dot(merged.astype(BF16), wo_ref[...])
    x1_ref[...] = x1
    ms = jnp.mean(x1 * x1, axis=-1, keepdims=True)
    h2 = x1 * lax.rsqrt(ms + RMS_EPS) * gffn_ref[...]
    h2_ref[...] = h2

    logits = _mm3(h2, wr_ref[...]) + br_ref[...]
    tm = logits.shape[0]
    lane = lax.broadcasted_iota(jnp.int32, (tm, LANES), 1)
    work = logits
    vals, idxs = [], []
    for _ in range(TOP_K):
        m = jnp.max(work, axis=-1, keepdims=True)
        ix = jnp.min(jnp.where(work == m, lane, LANES), axis=-1, keepdims=True)
        vals.append(m)
        idxs.append(ix)
        work = jnp.where(lane == ix, -jnp.inf, work)
    es = [jnp.exp(vk - vals[0]) for vk in vals]
    den = es[0] + es[1] + es[2] + es[3]
    member = jnp.zeros((tm, LANES), F32)
    for ix in idxs:
        member = member + (lane == ix).astype(F32)
    before = _dot(tri_ref[...], member.astype(BF16)) + carry_ref[...]
    idx_out = jnp.zeros((tm, LANES), jnp.int32)
    rank_out = jnp.zeros((tm, LANES), jnp.int32)
    gw_out = jnp.zeros((tm, LANES), F32)
    for kq in range(TOP_K):
        rk = jnp.sum(jnp.where(lane == idxs[kq], before, 0.0), axis=-1, keepdims=True)
        sel = lane == kq
        idx_out = jnp.where(sel, idxs[kq], idx_out)
        rank_out = jnp.where(sel, rk.astype(jnp.int32), rank_out)
        gw_out = jnp.where(sel, es[kq] / den, gw_out)
    idx_ref[...] = idx_out
    rank_ref[...] = rank_out
    gw_ref[...] = gw_out
    carry_ref[...] = carry_ref[...] + jnp.sum(member, axis=0, keepdims=True)
    cnt_ref[...] = carry_ref[...]


def _merge(o_f, o_b, bonus, g, yb, gates3, x3, lnx_g, lnx_b, w_a, w_b, w_o, g_ffn, wr_pad, br_pad, tm=256):
    bsz, n_p, t, _ = o_f.shape
    d_a = n_p * PAIR
    d_b = yb.shape[-1]
    d_model = x3.shape[-1]
    n_t = t // tm
    m = bsz * t
    ones = _block_ones(d_a, HEAD_DIM)
    tri = jnp.asarray(np.tril(np.ones((tm, tm)), -1), BF16)
    c2 = lambda bi, i: (0, 0)
    tok = lambda w: pl.BlockSpec((None, tm, w), lambda bi, i: (bi, i, 0))
    flat = lambda w: pl.BlockSpec((tm, w), lambda bi, i: (bi * n_t + i, 0))
    pair = pl.BlockSpec((None, n_p, tm, PAIR), lambda bi, i: (bi, 0, i, 0))
    return pl.pallas_call(
        functools.partial(_merge_kernel, d_model=d_model),
        grid=(bsz, n_t),
        in_specs=[
            pair, pair, tok(d_a), tok(d_a), tok(d_b), tok(2 * d_model), tok(d_model),
            pl.BlockSpec((1, d_a), c2), pl.BlockSpec((1, d_a), c2),
            pl.BlockSpec((d_a, d_model), c2), pl.BlockSpec((d_b, d_model), c2),
            pl.BlockSpec((d_model, d_model), c2), pl.BlockSpec((1, d_model), c2),
            pl.BlockSpec((d_model, LANES), c2), pl.BlockSpec((1, LANES), c2),
            pl.BlockSpec((d_a, d_a), c2), pl.BlockSpec((tm, tm), c2),
        ],
        out_specs=[flat(d_model), flat(d_model), flat(LANES), flat(LANES), flat(LANES),
                   pl.BlockSpec((1, LANES), c2)],
        out_shape=[
            jax.ShapeDtypeStruct((m, d_model), F32),
            jax.ShapeDtypeStruct((m, d_model), F32),
            jax.ShapeDtypeStruct((m, LANES), jnp.int32),
            jax.ShapeDtypeStruct((m, LANES), jnp.int32),
            jax.ShapeDtypeStruct((m, LANES), F32),
            jax.ShapeDtypeStruct((1, LANES), F32),
        ],
        scratch_shapes=[pltpu.VMEM((1, LANES), F32)],
        compiler_params=_params(("arbitrary", "arbitrary")),
        name="merge_router",
    )(o_f, o_b, bonus, g, yb, gates3, x3, lnx_g, lnx_b, w_a, w_b, w_o, g_ffn, wr_pad, br_pad, ones, tri)


def _dispatch_kernel(dest_ref, h_ref, xs_in_ref, xs_ref, sem):
    del xs_in_ref
    n = dest_ref.shape[1]

    def row_copy(j):
        return pltpu.make_async_copy(
            h_ref.at[pl.ds(j // TOP_K, 1), :], xs_ref.at[pl.ds(dest_ref[0, j], 1), :], sem)

    def issue(j, c):
        row_copy(j).start()
        return c

    def drain(j, c):
        row_copy(j).wait()
        return c

    lax.fori_loop(0, n, issue, 0)
    lax.fori_loop(0, n, drain, 0)


def _dispatch(dest, h2, n_pad, tm=256):
    m, d = h2.shape
    nt = m // tm
    dest3 = dest.reshape(nt, 1, tm * TOP_K)
    xs0 = jnp.zeros((n_pad, d), F32)
    return pl.pallas_call(
        _dispatch_kernel,
        grid=(nt,),
        in_specs=[
            pl.BlockSpec((None, 1, tm * TOP_K), lambda i: (i, 0, 0), memory_space=pltpu.SMEM),
            pl.BlockSpec((tm, d), lambda i: (i, 0)),
            pl.BlockSpec(memory_space=pl.ANY),
        ],
        out_specs=pl.BlockSpec(memory_space=pl.ANY),
        out_shape=jax.ShapeDtypeStruct((n_pad, d), F32),
        scratch_shapes=[pltpu.SemaphoreType.DMA(())],
        input_output_aliases={2: 0},
        compiler_params=_params(("arbitrary",)),
        name="moe_dispatch",
    )(dest3, h2, xs0)


def _expert_kernel(blk_ref, be_ref, nu_ref, xs_ref, w1_ref, b1_ref, w2_ref, b2_ref, ys_ref, *, d_e):
    del blk_ref, be_ref

    @pl.when(pl.program_id(0) < nu_ref[0])
    def _():
        x = xs_ref[...].astype(BF16)
        u = _dot(x, w1_ref[...]) + b1_ref[...]
        glu = jnp.minimum(u[:, :d_e], SWIGLU_LIMIT)
        lin = jnp.clip(u[:, d_e:], -SWIGLU_LIMIT, SWIGLU_LIMIT)
        act = glu * _sigmoid(SWIGLU_ALPHA * glu) * (lin + 1.0)
        ys_ref[...] = _dot(act.astype(BF16), w2_ref[...]) + b2_ref[...]

    @pl.when(pl.program_id(0) >= nu_ref[0])
    def _():
        ys_ref[...] = jnp.zeros_like(ys_ref)


def _experts(blk_idx, blk_e, n_used, xs, w1, b1, w2, b2):
    n_pad, d = xs.shape
    nb = n_pad // MOE_BLOCK
    n_e, _, d2 = w1.shape
    d_e = d2 // 2
    grid_spec = pltpu.PrefetchScalarGridSpec(
        num_scalar_prefetch=3,
        grid=(nb,),
        in_specs=[
            pl.BlockSpec((MOE_BLOCK, d), lambda i, bi, be, nu: (bi[i], 0)),
            pl.BlockSpec((None, d, d2), lambda i, bi, be, nu: (be[i], 0, 0)),
            pl.BlockSpec((None, 1, d2), lambda i, bi, be, nu: (be[i], 0, 0)),
            pl.BlockSpec((None, d_e, d), lambda i, bi, be, nu: (be[i], 0, 0)),
            pl.BlockSpec((None, 1, d), lambda i, bi, be, nu: (be[i], 0, 0)),
        ],
        out_specs=pl.BlockSpec((MOE_BLOCK, d), lambda i, bi, be, nu: (i, 0)),
    )
    return pl.pallas_call(
        functools.partial(_expert_kernel, d_e=d_e),
        grid_spec=grid_spec,
        out_shape=jax.ShapeDtypeStruct((n_pad, d), F32),
        compiler_params=_params(("arbitrary",)),
        name="moe_experts",
    )(blk_idx, blk_e, n_used, xs, w1, b1.reshape(n_e, 1, d2), w2, b2.reshape(n_e, 1, d))


def _combine_kernel(dest_ref, ys_ref, x1_ref, gw_ref, o_ref, buf, sem):
    n = dest_ref.shape[1]
    tm = x1_ref.shape[0]

    def row_copy(j):
        kq = j % TOP_K
        return pltpu.make_async_copy(
            ys_ref.at[pl.ds(dest_ref[0, j], 1), :], buf.at[kq, pl.ds(j // TOP_K, 1), :], sem)

    def issue(j, c):
        row_copy(j).start()
        return c

    def drain(j, c):
        row_copy(j).wait()
        return c

    lax.fori_loop(0, n, issue, 0)
    lax.fori_loop(0, n, drain, 0)
    gw = gw_ref[...]
    acc = x1_ref[...]
    for kq in range(TOP_K):
        acc = acc + gw[:, kq:kq + 1] * buf[kq]
    o_ref[...] = acc


def _combine(dest, ys, x1, gw, tm=128):
    m, d = x1.shape
    nt = m // tm
    dest3 = dest.reshape(nt, 1, tm * TOP_K)
    return pl.pallas_call(
        _combine_kernel,
        grid=(nt,),
        in_specs=[
            pl.BlockSpec((None, 1, tm * TOP_K), lambda i: (i, 0, 0), memory_space=pltpu.SMEM),
            pl.BlockSpec(memory_space=pl.ANY),
            pl.BlockSpec((tm, d), lambda i: (i, 0)),
            pl.BlockSpec((tm, LANES), lambda i: (i, 0)),
        ],
        out_specs=pl.BlockSpec((tm, d), lambda i: (i, 0)),
        out_shape=jax.ShapeDtypeStruct((m, d), F32),
        scratch_shapes=[pltpu.VMEM((TOP_K, tm, d), F32), pltpu.SemaphoreType.DMA(())],
        compiler_params=_params(("arbitrary",)),
        name="moe_combine",
    )(dest3, ys, x1, gw)


def _blockdiag2(a, b):
    za = jnp.zeros((a.shape[0], b.shape[1]), a.dtype)
    zb = jnp.zeros((b.shape[0], a.shape[1]), a.dtype)
    return jnp.concatenate([jnp.concatenate([a, za], axis=1), jnp.concatenate([zb, b], axis=1)], axis=0)


def _layer(x, g_mix, w_in, mu_prev, mu_next, w0_f, w2_f, w0_b, w2_b, a0_f, a2_f, a0_b, a2_b,
           g2, k_k, k_a, r_k, lnx_g, lnx_b, q_norm_g, k_norm_g, rpb, w_a, w_b, w_o,
           g_ffn, w_router, b_router, w1, b1, w2, b2):
    bsz, t, d_model = x.shape
    m = bsz * t
    d_a = w_a.shape[0]
    d_b = w_b.shape[0]
    a_cols = mu_prev.shape[0]
    b_cols = 3 * d_b
    row = lambda a: a.reshape(1, -1).astype(F32)

    w_in_b = w_in.astype(BF16)
    pa, pb, gates = _inproj(x.reshape(m, d_model), row(g_mix), w_in_b[:, :a_cols],
                            w_in_b[:, a_cols:a_cols + b_cols], w_in_b[:, a_cols + b_cols:])

    prep = _prep(pa.reshape(bsz, t, a_cols), row(mu_prev), row(mu_next),
                 jnp.concatenate([row(w0_f), row(w0_b)], axis=1), _blockdiag2(w2_f, w2_b),
                 jnp.concatenate([row(a0_f), row(a0_b)], axis=1), _blockdiag2(a2_f, a2_b),
                 g2, row(k_k), row(k_a), row(r_k), d_a)
    r, v, kk, lw_f, lw_b, k_f, k_b, b_f, b_b, bonus, g = prep
    o_f, o_b = _scan(r, v, kk, lw_f, lw_b, k_f, k_b, b_f, b_b)

    yb = _na(pb.reshape(bsz, t, b_cols), q_norm_g, k_norm_g, _na_bias_table(rpb), d_b)

    n_e = w_router.shape[1]
    wr_pad = jnp.zeros((d_model, LANES), F32).at[:, :n_e].set(w_router)
    br_pad = jnp.full((1, LANES), NEG_BIG, F32).at[0, :n_e].set(b_router)
    x1, h2, idx, rank, gw, cnt = _merge(
        o_f, o_b, bonus, g, yb, gates.reshape(bsz, t, 2 * d_model), x,
        row(lnx_g), row(lnx_b), w_a.astype(BF16), w_b.astype(BF16), w_o.astype(BF16),
        row(g_ffn), wr_pad, br_pad)

    counts = cnt[0, :n_e].astype(jnp.int32)
    padded = ((counts + MOE_BLOCK - 1) // MOE_BLOCK) * MOE_BLOCK
    pend = jnp.cumsum(padded)
    pstart = pend - padded
    n_assign = m * TOP_K
    n_blocks = -(-n_assign // MOE_BLOCK) + n_e
    n_pad = n_blocks * MOE_BLOCK
    top_idx = idx[:, :TOP_K]
    dest = (pstart[top_idx] + rank[:, :TOP_K]).reshape(-1).astype(jnp.int32)
    n_used = (pend[-1] // MOE_BLOCK).astype(jnp.int32)
    blk_idx = jnp.minimum(jnp.arange(n_blocks, dtype=jnp.int32), n_used - 1)
    blk_e = jnp.minimum(jnp.searchsorted(pend, blk_idx * MOE_BLOCK, side='right'), n_e - 1).astype(jnp.int32)

    xs = _dispatch(dest, h2, n_pad)
    ys = _experts(blk_idx, blk_e, n_used.reshape(1), xs, w1.astype(BF16), b1, w2.astype(BF16), b2)
    out = _combine(dest, ys, x1, gw)
    return out.reshape(bsz, t, d_model)


def kernel(x, g_mix, w_in, mu_prev, mu_next, w0_f, w2_f, w0_b, w2_b, a0_f, a2_f, a0_b, a2_b, g2, k_k, k_a, r_k, lnx_g, lnx_b, q_norm_g, k_norm_g, rpb, w_a, w_b, w_o, g_ffn, w_router, b_router, w1, b1, w2, b2):
    for l in range(g_mix.shape[0]):
        x = _layer(x, g_mix[l], w_in[l], mu_prev[l], mu_next[l], w0_f[l], w2_f[l], w0_b[l], w2_b[l],
                   a0_f[l], a2_f[l], a0_b[l], a2_b[l], g2[l], k_k[l], k_a[l], r_k[l], lnx_g[l], lnx_b[l],
                   q_norm_g[l], k_norm_g[l], rpb[l], w_a[l], w_b[l], w_o[l], g_ffn[l], w_router[l],
                   b_router[l], w1[l], b1[l], w2[l], b2[l])
    return x
```

```python
import functools
import math

import numpy as np
import jax
import jax.numpy as jnp
from jax import lax
from jax.experimental import pallas as pl
from jax.experimental.pallas import tpu as pltpu

F32 = jnp.float32
BF16 = jnp.bfloat16

LANES = 128
HEAD_DIM = 64
PAIR = 2 * HEAD_DIM
GRID_W = 64
NA_WR = 8
NA_WC = 16
W_LORA = 64
A_LORA = 64
G_LORA = 128
DECAY_SCALE = math.exp(-0.5)
GN_EPS = 64e-5
RMS_EPS = 1e-5
N_EXPERTS = 32
TOP_K = 4
MOE_BLOCK = 512
SWIGLU_LIMIT = 7.0
SWIGLU_ALPHA = 1.702
NEG_BIG = -1e30
CHUNK = 128
VMEM_LIMIT = 56 * 1024 * 1024


def _dot(a, b):
    return jnp.dot(a, b, preferred_element_type=F32)


def _dot_nt(a, b):
    return lax.dot_general(a, b, (((1,), (1,)), ((), ())), preferred_element_type=F32)


def _split(a):
    hi = a.astype(BF16)
    lo = (a - hi.astype(F32)).astype(BF16)
    return hi, lo


def _mm_exact_rhs(a, b_bf16):
    hi, lo = _split(a)
    return _dot(hi, b_bf16) + _dot(lo, b_bf16)


def _mm3(a, b):
    ah, al = _split(a)
    bh, bl = _split(b)
    return _dot(ah, bh) + _dot(al, bh) + _dot(ah, bl)


def _sigmoid(x):
    return 1.0 / (1.0 + jnp.exp(-x))


def _params(sem):
    return pltpu.CompilerParams(dimension_semantics=sem, vmem_limit_bytes=VMEM_LIMIT)


def _block_ones(n, blk):
    i = np.arange(n) // blk
    return jnp.asarray(i[:, None] == i[None, :], BF16)


def _inproj_kernel(x_ref, g_ref, wa_ref, wb_ref, wg_ref, pa_ref, pb_ref, pg_ref):
    x = x_ref[...]
    ms = jnp.mean(x * x, axis=-1, keepdims=True)
    h = (x * lax.rsqrt(ms + RMS_EPS) * g_ref[...]).astype(BF16)
    pa_ref[...] = _dot(h, wa_ref[...])
    pb_ref[...] = _dot(h, wb_ref[...])
    pg_ref[...] = _dot(h, wg_ref[...])


def _inproj(x2, g_mix, w_a, w_b, w_g, tm=256):
    m, d = x2.shape
    na, nb, ng = w_a.shape[1], w_b.shape[1], w_g.shape[1]
    full = lambda i: (0, 0)
    return pl.pallas_call(
        _inproj_kernel,
        grid=(m // tm,),
        in_specs=[
            pl.BlockSpec((tm, d), lambda i: (i, 0)),
            pl.BlockSpec((1, d), full),
            pl.BlockSpec((d, na), full),
            pl.BlockSpec((d, nb), full),
            pl.BlockSpec((d, ng), full),
        ],
        out_specs=[
            pl.BlockSpec((tm, na), lambda i: (i, 0)),
            pl.BlockSpec((tm, nb), lambda i: (i, 0)),
            pl.BlockSpec((tm, ng), lambda i: (i, 0)),
        ],
        out_shape=[
            jax.ShapeDtypeStruct((m, na), F32),
            jax.ShapeDtypeStruct((m, nb), F32),
            jax.ShapeDtypeStruct((m, ng), F32),
        ],
        compiler_params=_params(("arbitrary",)),
        name="inproj",
    )(x2, g_mix, w_a, w_b, w_g)


def _prep_kernel(p_ref, prev_ref, next_ref, mup_ref, mun_ref, w0_ref, w2_ref, a0_ref, a2_ref,
                 g2_ref, kk_ref, ka_ref, rk_ref, ones_ref,
                 r_o, v_o, kk_o, lwf_o, lwb_o, kf_o, kb_o, bf_o, bb_o, bonus_o, g_o, *, d_a):
    i = pl.program_id(1)
    n_t = pl.num_programs(1)
    p = p_ref[...]
    tt = p.shape[0]
    prow = jnp.where(i > 0, prev_ref[7:8, :], 0.0)
    nrow = jnp.where(i < n_t - 1, next_ref[0:1, :], 0.0)
    rid = lax.broadcasted_iota(jnp.int32, (tt, 1), 0)
    prev = jnp.where(rid == 0, prow, pltpu.roll(p, 1, axis=0))
    nxt = jnp.where(rid == tt - 1, nrow, pltpu.roll(p, tt - 1, axis=0))
    xa = p + mup_ref[...] * (prev - p) + mun_ref[...] * (nxt - p)

    r = xa[:, 0:d_a]
    k = xa[:, d_a:2 * d_a]
    v = xa[:, 2 * d_a:3 * d_a]
    o = 3 * d_a
    lw = xa[:, o:o + 2 * W_LORA]
    la = xa[:, o + 2 * W_LORA:o + 2 * W_LORA + 2 * A_LORA]
    lg = xa[:, o + 2 * W_LORA + 2 * A_LORA:]

    dpre = w0_ref[...] + _mm3(jnp.tanh(lw), w2_ref[...])
    apre = a0_ref[...] + _mm3(la, a2_ref[...])
    g = _mm3(_sigmoid(lg), g2_ref[...])
    logw = -DECAY_SCALE * _sigmoid(dpre)
    a = _sigmoid(apre)

    ones = ones_ref[...]
    kkr = k * kk_ref[...]
    ss = _mm_exact_rhs(kkr * kkr, ones)
    kk = kkr / jnp.maximum(jnp.sqrt(ss), 1e-12)

    ka = ka_ref[...]
    k_f = k * (1.0 + (a[:, :d_a] - 1.0) * ka)
    k_b = k * (1.0 + (a[:, d_a:] - 1.0) * ka)
    b_f = kk * a[:, :d_a]
    b_b = kk * a[:, d_a:]
    rk = rk_ref[...]
    bon = _mm_exact_rhs(r * (k_f + k_b) * rk, ones) * v

    bonus_o[...] = bon
    g_o[...] = g
    for pi in range(d_a // PAIR):
        sl = slice(pi * PAIR, (pi + 1) * PAIR)
        r_o[pi] = r[:, sl]
        v_o[pi] = v[:, sl]
        kk_o[pi] = kk[:, sl]
        lwf_o[pi] = logw[:, sl]
        lwb_o[pi] = logw[:, d_a + pi * PAIR:d_a + (pi + 1) * PAIR]
        kf_o[pi] = k_f[:, sl]
        kb_o[pi] = k_b[:, sl]
        bf_o[pi] = b_f[:, sl]
        bb_o[pi] = b_b[:, sl]


def _prep(pa3, mu_prev, mu_next, w0c, w2blk, a0c, a2blk, g2, k_k, k_a, r_k, d_a, tt=256):
    b, t, ac = pa3.shape
    n_t = t // tt
    n_p = d_a // PAIR
    ones = _block_ones(d_a, HEAD_DIM)
    c2 = lambda bi, i: (0, 0)
    pair_spec = pl.BlockSpec((None, n_p, tt, PAIR), lambda bi, i: (bi, 0, i, 0))
    pair_shape = jax.ShapeDtypeStruct((b, n_p, t, PAIR), F32)
    flat_spec = pl.BlockSpec((None, tt, d_a), lambda bi, i: (bi, i, 0))
    flat_shape = jax.ShapeDtypeStruct((b, t, d_a), F32)
    r8 = tt // 8
    return pl.pallas_call(
        functools.partial(_prep_kernel, d_a=d_a),
        grid=(b, n_t),
        in_specs=[
            pl.BlockSpec((None, tt, ac), lambda bi, i: (bi, i, 0)),
            pl.BlockSpec((None, 8, ac), lambda bi, i: (bi, jnp.maximum(i * r8 - 1, 0), 0)),
            pl.BlockSpec((None, 8, ac), lambda bi, i: (bi, jnp.minimum((i + 1) * r8, t // 8 - 1), 0)),
            pl.BlockSpec((1, ac), c2),
            pl.BlockSpec((1, ac), c2),
            pl.BlockSpec((1, 2 * d_a), c2),
            pl.BlockSpec((2 * W_LORA, 2 * d_a), c2),
            pl.BlockSpec((1, 2 * d_a), c2),
            pl.BlockSpec((2 * A_LORA, 2 * d_a), c2),
            pl.BlockSpec((G_LORA, d_a), c2),
            pl.BlockSpec((1, d_a), c2),
            pl.BlockSpec((1, d_a), c2),
            pl.BlockSpec((1, d_a), c2),
            pl.BlockSpec((d_a, d_a), c2),
        ],
        out_specs=[pair_spec] * 9 + [flat_spec] * 2,
        out_shape=[pair_shape] * 9 + [flat_shape] * 2,
        compiler_params=_params(("arbitrary", "arbitrary")),
        name="rwkv_prep",
    )(pa3, pa3, pa3, mu_prev, mu_next, w0c, w2blk, a0c, a2blk, g2, k_k, k_a, r_k, ones)


def _mm_exact_rhs_left(tri_bf16, x):
    hi, lo = _split(x)
    return _dot(tri_bf16, hi) + _dot(tri_bf16, lo)


def _scan_stage(items, s_refs):
    c = items[0][0].shape[0]
    ri = lax.broadcasted_iota(jnp.int32, (c, c), 0)
    ci = lax.broadcasted_iota(jnp.int32, (c, c), 1)
    lane = lax.broadcasted_iota(jnp.int32, (1, PAIR), 1)
    m0 = (lane < HEAD_DIM).astype(F32)
    m1 = 1.0 - m0
    eye = (ri == ci).astype(F32)
    hi_ = lax.broadcasted_iota(jnp.int32, (PAIR, PAIR), 0) // HEAD_DIM
    hj_ = lax.broadcasted_iota(jnp.int32, (PAIR, PAIR), 1) // HEAD_DIM
    same_head = hi_ == hj_
    zero = jnp.zeros((), F32)
    n = len(items)

    def masks(reverse):
        if reverse:
            return ci >= ri, ci > ri, 0
        return ci <= ri, ci < ri, c - 1

    cums = [_mm_exact_rhs_left(masks(it[6])[0].astype(BF16), it[3]) for it in items]

    pre = []
    for (r, kk, v, lw, k, b, reverse), cum in zip(items, cums):
        end = masks(reverse)[2]
        cmid = cum[c // 2:c // 2 + 1, :]
        cend = cum[end:end + 1, :]
        r_abs = r * jnp.exp(cum)
        a_abs = -kk * jnp.exp(cum - lw)
        to_mid = jnp.exp(-cmid)
        from_mid = jnp.exp(cmid - cum)
        to_end = jnp.exp(cend - cum)
        pre.append(dict(
            r_abs=r_abs, a_abs=a_abs, r_rel=r_abs * to_mid, a_rel=a_abs * to_mid,
            k_rel=k * from_mid, b_rel=b * from_mid, k_end=k * to_end, b_end=b * to_end,
            d_tot=jnp.exp(cend), v=v))

    grams = []
    for p in pre:
        lhs = jnp.concatenate([p["r_rel"] * m0, p["r_rel"] * m1, p["a_rel"] * m0, p["a_rel"] * m1],
                              axis=0).astype(BF16)
        rhs = jnp.concatenate([p["k_rel"], p["b_rel"]], axis=0).astype(BF16)
        grams.append(_dot_nt(lhs, rhs))

    pws, ts = [], []
    for it, gram in zip(items, grams):
        strict = masks(it[6])[1]
        for e in range(2):
            a_ab = jnp.where(strict, gram[(2 + e) * c:(3 + e) * c, c:2 * c], zero)
            pws.append(a_ab)
            ts.append(eye + a_ab)
    for _ in range(int(round(math.log2(c))) - 1):
        pws = [_dot(pw.astype(BF16), pw.astype(BF16)) for pw in pws]
        ts = [t + _dot(t.astype(BF16), pw.astype(BF16)) for t, pw in zip(ts, pws)]

    v_blks = [jnp.concatenate([p["v"] * m0, p["v"] * m1], axis=0).astype(BF16) for p in pre]
    akvs, o_rks, rb_cats = [], [], []
    for it, gram, v_blk in zip(items, grams, v_blks):
        incl, strict, _ = masks(it[6])
        ak_cat = jnp.concatenate(
            [jnp.where(strict, gram[(2 + e) * c:(3 + e) * c, 0:c], zero) for e in range(2)], axis=1)
        rk_cat = jnp.concatenate(
            [jnp.where(incl, gram[e * c:(e + 1) * c, 0:c], zero) for e in range(2)], axis=1)
        rb_cats.append(jnp.concatenate(
            [jnp.where(incl, gram[e * c:(e + 1) * c, c:2 * c], zero) for e in range(2)], axis=1).astype(BF16))
        akvs.append(_dot(ak_cat.astype(BF16), v_blk))
        o_rks.append(_dot(rk_cat.astype(BF16), v_blk))

    xs = []
    for i, (p, akv) in enumerate(zip(pre, akvs)):
        t_cat = jnp.concatenate([ts[2 * i], ts[2 * i + 1]], axis=1).astype(BF16)
        y_blk = jnp.concatenate([
            jnp.concatenate([p["a_abs"] * m0, akv * m0], axis=1),
            jnp.concatenate([p["a_abs"] * m1, akv * m1], axis=1)], axis=0).astype(BF16)
        xs.append(_dot(t_cat, y_blk))

    s0s = [s_ref[...] for s_ref in s_refs]
    s0bs = [s0.astype(BF16) for s0 in s0s]
    us = [_dot(x[:, 0:PAIR].astype(BF16), s0b) + x[:, PAIR:2 * PAIR] for x, s0b in zip(xs, s0bs)]
    outs = []
    for i in range(n):
        p, u = pre[i], us[i]
        u_blk = jnp.concatenate([u * m0, u * m1], axis=0).astype(BF16)
        outs.append(_dot(p["r_abs"].astype(BF16), s0bs[i]) + _dot(rb_cats[i], u_blk) + o_rks[i])
        kb_t = jnp.concatenate([p["b_end"].T, p["k_end"].T], axis=1).astype(BF16)
        uv = jnp.concatenate([u, p["v"]], axis=0).astype(BF16)
        d_col = jnp.broadcast_to(p["d_tot"], (PAIR, PAIR)).T
        s_refs[i][...] = jnp.where(same_head, d_col * s0s[i] + _dot(kb_t, uv), zero)
    return outs


def _scan_kernel(rf, vf, kkf, lwf, kf, bf, rb, vb, kkb, lwb, kb, bb, of_ref, ob_ref, s_ref):
    @pl.when(pl.program_id(1) == 0)
    def _():
        s_ref[...] = jnp.zeros_like(s_ref)

    n_p = rf.shape[0]
    items, s_refs = [], []
    for p in range(n_p):
        items.append((rf[p], kkf[p], vf[p], lwf[p], kf[p], bf[p], False))
        s_refs.append(s_ref.at[2 * p])
        items.append((rb[p], kkb[p], vb[p], lwb[p], kb[p], bb[p], True))
        s_refs.append(s_ref.at[2 * p + 1])
    outs = _scan_stage(items, s_refs)
    for p in range(n_p):
        of_ref[p] = outs[2 * p]
        ob_ref[p] = outs[2 * p + 1]


def _scan(r, v, kk, lw_f, lw_b, k_f, k_b, b_f, b_b):
    bsz, n_p, t, _ = r.shape
    nc = t // CHUNK
    fwd = pl.BlockSpec((None, n_p, CHUNK, PAIR), lambda bi, c: (bi, 0, c, 0))
    bwd = pl.BlockSpec((None, n_p, CHUNK, PAIR), lambda bi, c: (bi, 0, nc - 1 - c, 0))
    shape = jax.ShapeDtypeStruct((bsz, n_p, t, PAIR), F32)
    return pl.pallas_call(
        _scan_kernel,
        grid=(bsz, nc),
        in_specs=[fwd] * 6 + [bwd] * 6,
        out_specs=[fwd, bwd],
        out_shape=[shape, shape],
        scratch_shapes=[pltpu.VMEM((2 * n_p, PAIR, PAIR), F32)],
        compiler_params=_params(("arbitrary", "arbitrary")),
        name="rwkv_scan",
    )(r, v, kk, lw_f, k_f, b_f, r, v, kk, lw_b, k_b, b_b)


def _na_kernel(q_ref, k_ref, v_ref, gq_ref, gk_ref, tab_ref, ones_ref, o_ref, qn_s, kn_s, vb_s, *, rows):
    ones = ones_ref[...]
    scale = HEAD_DIM ** -0.5
    q = q_ref[...]
    k = k_ref[...]
    inv_d = 1.0 / HEAD_DIM
    qn = q * lax.rsqrt(_mm_exact_rhs(q * q, ones) * inv_d + RMS_EPS) * (gq_ref[...] * scale)
    kn = k * lax.rsqrt(_mm_exact_rhs(k * k, ones) * inv_d + RMS_EPS) * gk_ref[...]
    qn_s[...] = qn.astype(BF16)
    kn_s[...] = kn.astype(BF16)
    vb_s[...] = v_ref[...].astype(BF16)
    lane = lax.broadcasted_iota(jnp.int32, (1, PAIR), 1)
    head0 = lane < HEAD_DIM
    win = NA_WR * GRID_W

    def row(r, carry):
        rs = jnp.clip(r - NA_WR // 2, 0, rows - NA_WR)
        d0 = rs - r + NA_WR - 1
        q_row = qn_s[pl.ds(pl.multiple_of(r * GRID_W, GRID_W), GRID_W), :]
        k_win = kn_s[pl.ds(pl.multiple_of(rs * GRID_W, GRID_W), win), :]
        v_win = vb_s[pl.ds(pl.multiple_of(rs * GRID_W, GRID_W), win), :]
        outs = []
        for e in range(2):
            mask = head0 if e == 0 else jnp.logical_not(head0)
            qm = jnp.where(mask, q_row, jnp.zeros_like(q_row))
            s = _dot_nt(qm, k_win)
            bias = jnp.concatenate(
                [tab_ref[e, pl.ds(d0 + 2 * m, 1)][0] for m in range(NA_WR // 2)], axis=1)
            s = s + bias
            mx = jnp.max(s, axis=-1, keepdims=True)
            p = jnp.exp(s - mx)
            l = jnp.sum(p, axis=-1, keepdims=True)
            outs.append(_dot(p.astype(BF16), v_win) / l)
        o_ref[pl.ds(pl.multiple_of(r * GRID_W, GRID_W), GRID_W), :] = jnp.where(head0, outs[0], outs[1])
        return carry

    lax.fori_loop(0, rows, row, 0)


def _na_bias_table(rpb):
    qc = np.arange(GRID_W)
    kc = np.arange(GRID_W)
    cs = np.clip(qc - NA_WC // 2, 0, GRID_W - NA_WC)
    valid = (kc[None, :] >= cs[:, None]) & (kc[None, :] < cs[:, None] + NA_WC)
    dc = np.clip(kc[None, :] - qc[:, None] + NA_WC - 1, 0, 2 * NA_WC - 2)
    b = rpb.astype(F32)[:, :, dc]
    b = jnp.where(jnp.asarray(valid)[None, None], b, NEG_BIG)
    return jnp.concatenate([b[:, :-1], b[:, 1:]], axis=-1)


def _na(pb3, q_gain, k_gain, table, d_b):
    bsz, t, _ = pb3.shape
    rows = t // GRID_W
    n_p = d_b // PAIR
    ones = _block_ones(PAIR, HEAD_DIM)
    gq = jnp.tile(q_gain.reshape(1, HEAD_DIM), (1, 2))
    gk = jnp.tile(k_gain.reshape(1, HEAD_DIM), (1, 2))
    n_d = table.shape[1]
    c2 = lambda bi, p: (0, 0)
    return pl.pallas_call(
        functools.partial(_na_kernel, rows=rows),
        grid=(bsz, n_p),
        in_specs=[
            pl.BlockSpec((None, t, PAIR), lambda bi, p: (bi, 0, p)),
            pl.BlockSpec((None, t, PAIR), lambda bi, p: (bi, 0, n_p + p)),
            pl.BlockSpec((None, t, PAIR), lambda bi, p: (bi, 0, 2 * n_p + p)),
            pl.BlockSpec((1, PAIR), c2),
            pl.BlockSpec((1, PAIR), c2),
            pl.BlockSpec((2, n_d, GRID_W, PAIR), lambda bi, p: (p, 0, 0, 0)),
            pl.BlockSpec((PAIR, PAIR), c2),
        ],
        out_specs=pl.BlockSpec((None, t, PAIR), lambda bi, p: (bi, 0, p)),
        out_shape=jax.ShapeDtypeStruct((bsz, t, d_b), F32),
        scratch_shapes=[pltpu.VMEM((t, PAIR), BF16)] * 3,
        compiler_params=_params(("arbitrary", "arbitrary")),
        name="natten",
    )(pb3, pb3, pb3, gq, gk, table, ones)


def _merge_kernel(of_ref, ob_ref, bonus_ref, g_ref, yb_ref, gates_ref, x_ref,
                  lng_ref, lnb_ref, wa_ref, wb_ref, wo_ref, gffn_ref, wr_ref, br_ref,
                  ones_ref, tri_ref,
                  x1_ref, h2_ref, idx_ref, rank_ref, gw_ref, cnt_ref, carry_ref, *, d_model):
    first = jnp.logical_and(pl.program_id(0) == 0, pl.program_id(1) == 0)

    @pl.when(first)
    def _():
        carry_ref[...] = jnp.zeros_like(carry_ref)

    n_p = of_ref.shape[0]
    o = jnp.concatenate([of_ref[p] + ob_ref[p] for p in range(n_p)], axis=1)
    ones = ones_ref[...]
    inv_d = 1.0 / HEAD_DIM
    mu = _mm_exact_rhs(o, ones) * inv_d
    dv = o - mu
    var = _mm_exact_rhs(dv * dv, ones) * inv_d
    y = dv * lax.rsqrt(var + GN_EPS) * lng_ref[...] + lnb_ref[...] + bonus_ref[...]
    ya = y * g_ref[...]

    gates = gates_ref[...]
    pa = _dot(ya.astype(BF16), wa_ref[...])
    pb = _dot(yb_ref[...].astype(BF16), wb_ref[...])
    merged = _sigmoid(gates[:, :d_model]) * pa + _sigmoid(gates[:, d_model:]) * pb
    x1 = x_ref[...] + _dot(merged.astype(BF16), wo_ref[...])
    x1_ref[...] = x1
    ms = jnp.mean(x1 * x1, axis=-1, keepdims=True)
    h2 = x1 * lax.rsqrt(ms + RMS_EPS) * gffn_ref[...]
    h2_ref[...] = h2

    logits = _mm3(h2, wr_ref[...]) + br_ref[...]
    tm = logits.shape[0]
    lane = lax.broadcasted_iota(jnp.int32, (tm, LANES), 1)
    work = logits
    vals, idxs = [], []
    for _ in range(TOP_K):
        m = jnp.max(work, axis=-1, keepdims=True)
        ix = jnp.min(jnp.where(work == m, lane, LANES), axis=-1, keepdims=True)
        vals.append(m)
        idxs.append(ix)
        work = jnp.where(lane == ix, -jnp.inf, work)
    es = [jnp.exp(vk - vals[0]) for vk in vals]
    den = es[0] + es[1] + es[2] + es[3]
    member = jnp.zeros((tm, LANES), F32)
    for ix in idxs:
        member = member + (lane == ix).astype(F32)
    before = _dot(tri_ref[...], member.astype(BF16)) + carry_ref[...]
    idx_out = jnp.zeros((tm, LANES), jnp.int32)
    rank_out = jnp.zeros((tm, LANES), jnp.int32)
    gw_out = jnp.zeros((tm, LANES), F32)
    for kq in range(TOP_K):
        rk = jnp.sum(jnp.where(lane == idxs[kq], before, 0.0), axis=-1, keepdims=True)
        sel = lane == kq
        idx_out = jnp.where(sel, idxs[kq], idx_out)
        rank_out = jnp.where(sel, rk.astype(jnp.int32), rank_out)
        gw_out = jnp.where(sel, es[kq] / den, gw_out)
    idx_ref[...] = idx_out
    rank_ref[...] = rank_out
    gw_ref[...] = gw_out
    carry_ref[...] = carry_ref[...] + jnp.sum(member, axis=0, keepdims=True)
    cnt_ref[...] = carry_ref[...]


def _merge(o_f, o_b, bonus, g, yb, gates3, x3, lnx_g, lnx_b, w_a, w_b, w_o, g_ffn, wr_pad, br_pad, tm=256):
    bsz, n_p, t, _ = o_f.shape
    d_a = n_p * PAIR
    d_b = yb.shape[-1]
    d_model = x3.shape[-1]
    n_t = t // tm
    m = bsz * t
    ones = _block_ones(d_a, HEAD_DIM)
    tri = jnp.asarray(np.tril(np.ones((tm, tm)), -1), BF16)
    c2 = lambda bi, i: (0, 0)
    tok = lambda w: pl.BlockSpec((None, tm, w), lambda bi, i: (bi, i, 0))
    flat = lambda w: pl.BlockSpec((tm, w), lambda bi, i: (bi * n_t + i, 0))
    pair = pl.BlockSpec((None, n_p, tm, PAIR), lambda bi, i: (bi, 0, i, 0))
    return pl.pallas_call(
        functools.partial(_merge_kernel, d_model=d_model),
        grid=(bsz, n_t),
        in_specs=[
            pair, pair, tok(d_a), tok(d_a), tok(d_b), tok(2 * d_model), tok(d_model),
            pl.BlockSpec((1, d_a), c2), pl.BlockSpec((1, d_a), c2),
            pl.BlockSpec((d_a, d_model), c2), pl.BlockSpec((d_b, d_model), c2),
            pl.BlockSpec((d_model, d_model), c2), pl.BlockSpec((1, d_model), c2),
            pl.BlockSpec((d_model, LANES), c2), pl.BlockSpec((1, LANES), c2),
            pl.BlockSpec((d_a, d_a), c2), pl.BlockSpec((tm, tm), c2),
        ],
        out_specs=[flat(d_model), flat(d_model), flat(LANES), flat(LANES), flat(LANES),
                   pl.BlockSpec((1, LANES), c2)],
        out_shape=[
            jax.ShapeDtypeStruct((m, d_model), F32),
            jax.ShapeDtypeStruct((m, d_model), F32),
            jax.ShapeDtypeStruct((m, LANES), jnp.int32),
            jax.ShapeDtypeStruct((m, LANES), jnp.int32),
            jax.ShapeDtypeStruct((m, LANES), F32),
            jax.ShapeDtypeStruct((1, LANES), F32),
        ],
        scratch_shapes=[pltpu.VMEM((1, LANES), F32)],
        compiler_params=_params(("arbitrary", "arbitrary")),
        name="merge_router",
    )(o_f, o_b, bonus, g, yb, gates3, x3, lnx_g, lnx_b, w_a, w_b, w_o, g_ffn, wr_pad, br_pad, ones, tri)


def _dispatch_kernel(dest_ref, h_ref, xs_in_ref, xs_ref, sem):
    del xs_in_ref
    tm = h_ref.shape[0]

    def row_copy(t, kq):
        return pltpu.make_async_copy(
            h_ref.at[pl.ds(t, 1), :], xs_ref.at[pl.ds(dest_ref[0, t * TOP_K + kq], 1), :], sem)

    def issue(t, c):
        for kq in range(TOP_K):
            row_copy(t, kq).start()
        return c

    def drain(t, c):
        for kq in range(TOP_K):
            row_copy(t, kq).wait()
        return c

    lax.fori_loop(0, tm, issue, 0, unroll=4)
    lax.fori_loop(0, tm, drain, 0, unroll=8)


def _dispatch(dest, h2, n_pad, tm=512):
    m, d = h2.shape
    nt = m // tm
    dest3 = dest.reshape(nt, 1, tm * TOP_K)
    xs0 = jnp.zeros((n_pad, d), F32)
    return pl.pallas_call(
        _dispatch_kernel,
        grid=(nt,),
        in_specs=[
            pl.BlockSpec((None, 1, tm * TOP_K), lambda i: (i, 0, 0), memory_space=pltpu.SMEM),
            pl.BlockSpec((tm, d), lambda i: (i, 0)),
            pl.BlockSpec(memory_space=pl.ANY),
        ],
        out_specs=pl.BlockSpec(memory_space=pl.ANY),
        out_shape=jax.ShapeDtypeStruct((n_pad, d), F32),
        scratch_shapes=[pltpu.SemaphoreType.DMA(())],
        input_output_aliases={2: 0},
        compiler_params=_params(("arbitrary",)),
        name="moe_dispatch",
    )(dest3, h2, xs0)


def _expert_kernel(blk_ref, be_ref, nu_ref, xs_ref, w1_ref, b1_ref, w2_ref, b2_ref, ys_ref, *, d_e):
    del blk_ref, be_ref

    @pl.when(pl.program_id(0) < nu_ref[0])
    def _():
        x = xs_ref[...].astype(BF16)
        u = _dot(x, w1_ref[...]) + b1_ref[...]
        glu = jnp.minimum(u[:, :d_e], SWIGLU_LIMIT)
        lin = jnp.clip(u[:, d_e:], -SWIGLU_LIMIT, SWIGLU_LIMIT)
        act = glu * _sigmoid(SWIGLU_ALPHA * glu) * (lin + 1.0)
        ys_ref[...] = _dot(act.astype(BF16), w2_ref[...]) + b2_ref[...]

    @pl.when(pl.program_id(0) >= nu_ref[0])
    def _():
        ys_ref[...] = jnp.zeros_like(ys_ref)


def _experts(blk_idx, blk_e, n_used, xs, w1, b1, w2, b2):
    n_pad, d = xs.shape
    nb = n_pad // MOE_BLOCK
    n_e, _, d2 = w1.shape
    d_e = d2 // 2
    grid_spec = pltpu.PrefetchScalarGridSpec(
        num_scalar_prefetch=3,
        grid=(nb,),
        in_specs=[
            pl.BlockSpec((MOE_BLOCK, d), lambda i, bi, be, nu: (bi[i], 0)),
            pl.BlockSpec((None, d, d2), lambda i, bi, be, nu: (be[i], 0, 0)),
            pl.BlockSpec((None, 1, d2), lambda i, bi, be, nu: (be[i], 0, 0)),
            pl.BlockSpec((None, d_e, d), lambda i, bi, be, nu: (be[i], 0, 0)),
            pl.BlockSpec((None, 1, d), lambda i, bi, be, nu: (be[i], 0, 0)),
        ],
        out_specs=pl.BlockSpec((MOE_BLOCK, d), lambda i, bi, be, nu: (i, 0)),
    )
    return pl.pallas_call(
        functools.partial(_expert_kernel, d_e=d_e),
        grid_spec=grid_spec,
        out_shape=jax.ShapeDtypeStruct((n_pad, d), F32),
        compiler_params=_params(("arbitrary",)),
        name="moe_experts",
    )(blk_idx, blk_e, n_used, xs, w1, b1.reshape(n_e, 1, d2), w2, b2.reshape(n_e, 1, d))


def _combine_kernel(dest_ref, dnext_ref, ys_ref, x1_ref, gw_ref, o_ref, buf, sems):
    i = pl.program_id(0)
    nt = pl.num_programs(0)
    tm = x1_ref.shape[0]
    slot = i % 2

    def row_copy(d_ref, sl, t, kq):
        return pltpu.make_async_copy(
            ys_ref.at[pl.ds(d_ref[0, t * TOP_K + kq], 1), :], buf.at[sl, kq, pl.ds(t, 1), :], sems.at[sl])

    def issue_all(d_ref, sl):
        def body(t, c):
            for kq in range(TOP_K):
                row_copy(d_ref, sl, t, kq).start()
            return c
        lax.fori_loop(0, tm, body, 0, unroll=4)

    @pl.when(i == 0)
    def _():
        issue_all(dest_ref, 0)

    @pl.when(i + 1 < nt)
    def _():
        issue_all(dnext_ref, 1 - slot)

    def drain(t, c):
        for kq in range(TOP_K):
            row_copy(dest_ref, slot, t, kq).wait()
        return c

    lax.fori_loop(0, tm, drain, 0, unroll=8)
    gw = gw_ref[...]
    acc = x1_ref[...]
    for kq in range(TOP_K):
        acc = acc + gw[:, kq:kq + 1] * buf[slot, kq]
    o_ref[...] = acc


def _combine(dest, ys, x1, gw, tm=128):
    m, d = x1.shape
    nt = m // tm
    dest3 = dest.reshape(nt, 1, tm * TOP_K)
    return pl.pallas_call(
        _combine_kernel,
        grid=(nt,),
        in_specs=[
            pl.BlockSpec((None, 1, tm * TOP_K), lambda i: (i, 0, 0), memory_space=pltpu.SMEM),
            pl.BlockSpec((None, 1, tm * TOP_K), lambda i: (jnp.minimum(i + 1, nt - 1), 0, 0),
                         memory_space=pltpu.SMEM),
            pl.BlockSpec(memory_space=pl.ANY),
            pl.BlockSpec((tm, d), lambda i: (i, 0)),
            pl.BlockSpec((tm, LANES), lambda i: (i, 0)),
        ],
        out_specs=pl.BlockSpec((tm, d), lambda i: (i, 0)),
        out_shape=jax.ShapeDtypeStruct((m, d), F32),
        scratch_shapes=[pltpu.VMEM((2, TOP_K, tm, d), F32), pltpu.SemaphoreType.DMA((2,))],
        compiler_params=_params(("arbitrary",)),
        name="moe_combine",
    )(dest3, dest3, ys, x1, gw)


def _blockdiag2(a, b):
    za = jnp.zeros((a.shape[0], b.shape[1]), a.dtype)
    zb = jnp.zeros((b.shape[0], a.shape[1]), a.dtype)
    return jnp.concatenate([jnp.concatenate([a, za], axis=1), jnp.concatenate([zb, b], axis=1)], axis=0)


def _layer(x, g_mix, w_in, mu_prev, mu_next, w0_f, w2_f, w0_b, w2_b, a0_f, a2_f, a0_b, a2_b,
           g2, k_k, k_a, r_k, lnx_g, lnx_b, q_norm_g, k_norm_g, rpb, w_a, w_b, w_o,
           g_ffn, w_router, b_router, w1, b1, w2, b2):
    bsz, t, d_model = x.shape
    m = bsz * t
    d_a = w_a.shape[0]
    d_b = w_b.shape[0]
    a_cols = mu_prev.shape[0]
    b_cols = 3 * d_b
    row = lambda a: a.reshape(1, -1).astype(F32)

    w_in_b = w_in.astype(BF16)
    pa, pb, gates = _inproj(x.reshape(m, d_model), row(g_mix), w_in_b[:, :a_cols],
                            w_in_b[:, a_cols:a_cols + b_cols], w_in_b[:, a_cols + b_cols:])

    prep = _prep(pa.reshape(bsz, t, a_cols), row(mu_prev), row(mu_next),
                 jnp.concatenate([row(w0_f), row(w0_b)], axis=1), _blockdiag2(w2_f, w2_b),
                 jnp.concatenate([row(a0_f), row(a0_b)], axis=1), _blockdiag2(a2_f, a2_b),
                 g2, row(k_k), row(k_a), row(r_k), d_a)
    r, v, kk, lw_f, lw_b, k_f, k_b, b_f, b_b, bonus, g = prep
    o_f, o_b = _scan(r, v, kk, lw_f, lw_b, k_f, k_b, b_f, b_b)

    yb = _na(pb.reshape(bsz, t, b_cols), q_norm_g, k_norm_g, _na_bias_table(rpb), d_b)

    n_e = w_router.shape[1]
    wr_pad = jnp.zeros((d_model, LANES), F32).at[:, :n_e].set(w_router)
    br_pad = jnp.full((1, LANES), NEG_BIG, F32).at[0, :n_e].set(b_router)
    x1, h2, idx, rank, gw, cnt = _merge(
        o_f, o_b, bonus, g, yb, gates.reshape(bsz, t, 2 * d_model), x,
        row(lnx_g), row(lnx_b), w_a.astype(BF16), w_b.astype(BF16), w_o.astype(BF16),
        row(g_ffn), wr_pad, br_pad)

    counts = cnt[0, :n_e].astype(jnp.int32)
    padded = ((counts + MOE_BLOCK - 1) // MOE_BLOCK) * MOE_BLOCK
    pend = jnp.cumsum(padded)
    pstart = pend - padded
    n_assign = m * TOP_K
    n_blocks = -(-n_assign // MOE_BLOCK) + n_e
    n_pad = n_blocks * MOE_BLOCK
    top_idx = idx[:, :TOP_K]
    dest = (pstart[top_idx] + rank[:, :TOP_K]).reshape(-1).astype(jnp.int32)
    n_used = (pend[-1] // MOE_BLOCK).astype(jnp.int32)
    blk_idx = jnp.minimum(jnp.arange(n_blocks, dtype=jnp.int32), n_used - 1)
    blk_e = jnp.minimum(jnp.searchsorted(pend, blk_idx * MOE_BLOCK, side='right'), n_e - 1).astype(jnp.int32)

    xs = _dispatch(dest, h2, n_pad)
    ys = _experts(blk_idx, blk_e, n_used.reshape(1), xs, w1.astype(BF16), b1, w2.astype(BF16), b2)
    out = _combine(dest, ys, x1, gw)
    return out.reshape(bsz, t, d_model)


def kernel(x, g_mix, w_in, mu_prev, mu_next, w0_f, w2_f, w0_b, w2_b, a0_f, a2_f, a0_b, a2_b, g2, k_k, k_a, r_k, lnx_g, lnx_b, q_norm_g, k_norm_g, rpb, w_a, w_b, w_o, g_ffn, w_router, b_router, w1, b1, w2, b2):
    for l in range(g_mix.shape[0]):
        x = _layer(x, g_mix[l], w_in[l], mu_prev[l], mu_next[l], w0_f[l], w2_f[l], w0_b[l], w2_b[l],
                   a0_f[l], a2_f[l], a0_b[l], a2_b[l], g2[l], k_k[l], k_a[l], r_k[l], lnx_g[l], lnx_b[l],
                   q_norm_g[l], k_norm_g[l], rpb[l], w_a[l], w_b[l], w_o[l], g_ffn[l], w_router[l],
                   b_router[l], w1[l], b1[l], w2[l], b2[l])
    return x
```

```python
import functools
import math

import numpy as np
import jax
import jax.numpy as jnp
from jax import lax
from jax.experimental import pallas as pl
from jax.experimental.pallas import tpu as pltpu

F32 = jnp.float32
BF16 = jnp.bfloat16

LANES = 128
HEAD_DIM = 64
PAIR = 2 * HEAD_DIM
GRID_W = 64
NA_WR = 8
NA_WC = 16
W_LORA = 64
A_LORA = 64
G_LORA = 128
DECAY_SCALE = math.exp(-0.5)
GN_EPS = 64e-5
RMS_EPS = 1e-5
N_EXPERTS = 32
TOP_K = 4
MOE_BLOCK = 512
SWIGLU_LIMIT = 7.0
SWIGLU_ALPHA = 1.702
NEG_BIG = -1e30
CHUNK = 128
NA_ROWS_PER_STEP = 4
VMEM_LIMIT = 56 * 1024 * 1024


def _dot(a, b):
    return jnp.dot(a, b, preferred_element_type=F32)


def _dot_nt(a, b):
    return lax.dot_general(a, b, (((1,), (1,)), ((), ())), preferred_element_type=F32)


def _split(a):
    hi = a.astype(BF16)
    lo = (a - hi.astype(F32)).astype(BF16)
    return hi, lo


def _mm_exact_rhs(a, b_bf16):
    hi, lo = _split(a)
    return _dot(hi, b_bf16) + _dot(lo, b_bf16)


def _mm3(a, b):
    ah, al = _split(a)
    bh, bl = _split(b)
    return _dot(ah, bh) + _dot(al, bh) + _dot(ah, bl)


def _sigmoid(x):
    return 1.0 / (1.0 + jnp.exp(-x))


def _params(sem):
    return pltpu.CompilerParams(dimension_semantics=sem, vmem_limit_bytes=VMEM_LIMIT)


def _block_ones(n, blk):
    i = np.arange(n) // blk
    return jnp.asarray(i[:, None] == i[None, :], BF16)


def _inproj_kernel(x_ref, g_ref, wa_ref, wb_ref, wg_ref, pa_ref, pb_ref, pg_ref):
    x = x_ref[...]
    ms = jnp.mean(x * x, axis=-1, keepdims=True)
    h = (x * lax.rsqrt(ms + RMS_EPS) * g_ref[...]).astype(BF16)
    pa_ref[...] = _dot(h, wa_ref[...])
    pb_ref[...] = _dot(h, wb_ref[...])
    pg_ref[...] = _dot(h, wg_ref[...])


def _inproj(x2, g_mix, w_a, w_b, w_g, tm=256):
    m, d = x2.shape
    na, nb, ng = w_a.shape[1], w_b.shape[1], w_g.shape[1]
    full = lambda i: (0, 0)
    return pl.pallas_call(
        _inproj_kernel,
        grid=(m // tm,),
        in_specs=[
            pl.BlockSpec((tm, d), lambda i: (i, 0)),
            pl.BlockSpec((1, d), full),
            pl.BlockSpec((d, na), full),
            pl.BlockSpec((d, nb), full),
            pl.BlockSpec((d, ng), full),
        ],
        out_specs=[
            pl.BlockSpec((tm, na), lambda i: (i, 0)),
            pl.BlockSpec((tm, nb), lambda i: (i, 0)),
            pl.BlockSpec((tm, ng), lambda i: (i, 0)),
        ],
        out_shape=[
            jax.ShapeDtypeStruct((m, na), F32),
            jax.ShapeDtypeStruct((m, nb), F32),
            jax.ShapeDtypeStruct((m, ng), F32),
        ],
        compiler_params=_params(("arbitrary",)),
        name="inproj",
    )(x2, g_mix, w_a, w_b, w_g)


def _prep_kernel(p_ref, prev_ref, next_ref, mup_ref, mun_ref, w0_ref, w2_ref, a0_ref, a2_ref,
                 g2_ref, kk_ref, ka_ref, rk_ref, ones_ref,
                 r_o, v_o, kk_o, lwf_o, lwb_o, kf_o, kb_o, bf_o, bb_o, bonus_o, g_o, *, d_a):
    i = pl.program_id(1)
    n_t = pl.num_programs(1)
    p = p_ref[...]
    tt = p.shape[0]
    prow = jnp.where(i > 0, prev_ref[7:8, :], 0.0)
    nrow = jnp.where(i < n_t - 1, next_ref[0:1, :], 0.0)
    rid = lax.broadcasted_iota(jnp.int32, (tt, 1), 0)
    prev = jnp.where(rid == 0, prow, pltpu.roll(p, 1, axis=0))
    nxt = jnp.where(rid == tt - 1, nrow, pltpu.roll(p, tt - 1, axis=0))
    xa = p + mup_ref[...] * (prev - p) + mun_ref[...] * (nxt - p)

    r = xa[:, 0:d_a]
    k = xa[:, d_a:2 * d_a]
    v = xa[:, 2 * d_a:3 * d_a]
    o = 3 * d_a
    lw = xa[:, o:o + 2 * W_LORA]
    la = xa[:, o + 2 * W_LORA:o + 2 * W_LORA + 2 * A_LORA]
    lg = xa[:, o + 2 * W_LORA + 2 * A_LORA:]

    dpre = w0_ref[...] + _mm3(jnp.tanh(lw), w2_ref[...])
    apre = a0_ref[...] + _mm3(la, a2_ref[...])
    g = _mm3(_sigmoid(lg), g2_ref[...])
    logw = -DECAY_SCALE * _sigmoid(dpre)
    a = _sigmoid(apre)

    ones = ones_ref[...]
    kkr = k * kk_ref[...]
    ss = _mm_exact_rhs(kkr * kkr, ones)
    kk = kkr / jnp.maximum(jnp.sqrt(ss), 1e-12)

    ka = ka_ref[...]
    k_f = k * (1.0 + (a[:, :d_a] - 1.0) * ka)
    k_b = k * (1.0 + (a[:, d_a:] - 1.0) * ka)
    b_f = kk * a[:, :d_a]
    b_b = kk * a[:, d_a:]
    rk = rk_ref[...]
    bon = _mm_exact_rhs(r * (k_f + k_b) * rk, ones) * v

    bonus_o[...] = bon
    g_o[...] = g
    for pi in range(d_a // PAIR):
        sl = slice(pi * PAIR, (pi + 1) * PAIR)
        r_o[pi] = r[:, sl]
        v_o[pi] = v[:, sl]
        kk_o[pi] = kk[:, sl]
        lwf_o[pi] = logw[:, sl]
        lwb_o[pi] = logw[:, d_a + pi * PAIR:d_a + (pi + 1) * PAIR]
        kf_o[pi] = k_f[:, sl]
        kb_o[pi] = k_b[:, sl]
        bf_o[pi] = b_f[:, sl]
        bb_o[pi] = b_b[:, sl]


def _prep(pa3, mu_prev, mu_next, w0c, w2blk, a0c, a2blk, g2, k_k, k_a, r_k, d_a, tt=256):
    b, t, ac = pa3.shape
    n_t = t // tt
    n_p = d_a // PAIR
    ones = _block_ones(d_a, HEAD_DIM)
    c2 = lambda bi, i: (0, 0)
    pair_spec = pl.BlockSpec((None, n_p, tt, PAIR), lambda bi, i: (bi, 0, i, 0))
    pair_shape = jax.ShapeDtypeStruct((b, n_p, t, PAIR), F32)
    flat_spec = pl.BlockSpec((None, tt, d_a), lambda bi, i: (bi, i, 0))
    flat_shape = jax.ShapeDtypeStruct((b, t, d_a), F32)
    r8 = tt // 8
    return pl.pallas_call(
        functools.partial(_prep_kernel, d_a=d_a),
        grid=(b, n_t),
        in_specs=[
            pl.BlockSpec((None, tt, ac), lambda bi, i: (bi, i, 0)),
            pl.BlockSpec((None, 8, ac), lambda bi, i: (bi, jnp.maximum(i * r8 - 1, 0), 0)),
            pl.BlockSpec((None, 8, ac), lambda bi, i: (bi, jnp.minimum((i + 1) * r8, t // 8 - 1), 0)),
            pl.BlockSpec((1, ac), c2),
            pl.BlockSpec((1, ac), c2),
            pl.BlockSpec((1, 2 * d_a), c2),
            pl.BlockSpec((2 * W_LORA, 2 * d_a), c2),
            pl.BlockSpec((1, 2 * d_a), c2),
            pl.BlockSpec((2 * A_LORA, 2 * d_a), c2),
            pl.BlockSpec((G_LORA, d_a), c2),
            pl.BlockSpec((1, d_a), c2),
            pl.BlockSpec((1, d_a), c2),
            pl.BlockSpec((1, d_a), c2),
            pl.BlockSpec((d_a, d_a), c2),
        ],
        out_specs=[pair_spec] * 9 + [flat_spec] * 2,
        out_shape=[pair_shape] * 9 + [flat_shape] * 2,
        compiler_params=_params(("arbitrary", "arbitrary")),
        name="rwkv_prep",
    )(pa3, pa3, pa3, mu_prev, mu_next, w0c, w2blk, a0c, a2blk, g2, k_k, k_a, r_k, ones)


def _mm_exact_rhs_left(tri_bf16, x):
    hi, lo = _split(x)
    return _dot(tri_bf16, hi) + _dot(tri_bf16, lo)


def _scan_stage(items, s_refs):
    c = items[0][0].shape[0]
    ri = lax.broadcasted_iota(jnp.int32, (c, c), 0)
    ci = lax.broadcasted_iota(jnp.int32, (c, c), 1)
    lane = lax.broadcasted_iota(jnp.int32, (1, PAIR), 1)
    m0 = (lane < HEAD_DIM).astype(F32)
    m1 = 1.0 - m0
    eye = (ri == ci).astype(F32)
    hi_ = lax.broadcasted_iota(jnp.int32, (PAIR, PAIR), 0) // HEAD_DIM
    hj_ = lax.broadcasted_iota(jnp.int32, (PAIR, PAIR), 1) // HEAD_DIM
    same_head = hi_ == hj_
    zero = jnp.zeros((), F32)
    n = len(items)

    def masks(reverse):
        if reverse:
            return ci >= ri, ci > ri, 0
        return ci <= ri, ci < ri, c - 1

    cums = [_mm_exact_rhs_left(masks(it[6])[0].astype(BF16), it[3]) for it in items]

    pre = []
    for (r, kk, v, lw, k, b, reverse), cum in zip(items, cums):
        end = masks(reverse)[2]
        cmid = cum[c // 2:c // 2 + 1, :]
        cend = cum[end:end + 1, :]
        r_abs = r * jnp.exp(cum)
        a_abs = -kk * jnp.exp(cum - lw)
        to_mid = jnp.exp(-cmid)
        from_mid = jnp.exp(cmid - cum)
        to_end = jnp.exp(cend - cum)
        pre.append(dict(
            r_abs=r_abs, a_abs=a_abs, r_rel=r_abs * to_mid, a_rel=a_abs * to_mid,
            k_rel=k * from_mid, b_rel=b * from_mid, k_end=k * to_end, b_end=b * to_end,
            d_tot=jnp.exp(cend), v=v))

    grams = []
    for p in pre:
        lhs = jnp.concatenate([p["r_rel"] * m0, p["r_rel"] * m1, p["a_rel"] * m0, p["a_rel"] * m1],
                              axis=0).astype(BF16)
        rhs = jnp.concatenate([p["k_rel"], p["b_rel"]], axis=0).astype(BF16)
        grams.append(_dot_nt(lhs, rhs))

    pws, ts = [], []
    for it, gram in zip(items, grams):
        strict = masks(it[6])[1]
        for e in range(2):
            a_ab = jnp.where(strict, gram[(2 + e) * c:(3 + e) * c, c:2 * c], zero)
            pws.append(a_ab)
            ts.append(eye + a_ab)
    for _ in range(int(round(math.log2(c))) - 1):
        pws = [_dot(pw.astype(BF16), pw.astype(BF16)) for pw in pws]
        ts = [t + _dot(t.astype(BF16), pw.astype(BF16)) for t, pw in zip(ts, pws)]

    v_blks = [jnp.concatenate([p["v"] * m0, p["v"] * m1], axis=0).astype(BF16) for p in pre]
    akvs, o_rks, rb_cats = [], [], []
    for it, gram, v_blk in zip(items, grams, v_blks):
        incl, strict, _ = masks(it[6])
        ak_cat = jnp.concatenate(
            [jnp.where(strict, gram[(2 + e) * c:(3 + e) * c, 0:c], zero) for e in range(2)], axis=1)
        rk_cat = jnp.concatenate(
            [jnp.where(incl, gram[e * c:(e + 1) * c, 0:c], zero) for e in range(2)], axis=1)
        rb_cats.append(jnp.concatenate(
            [jnp.where(incl, gram[e * c:(e + 1) * c, c:2 * c], zero) for e in range(2)], axis=1).astype(BF16))
        akvs.append(_dot(ak_cat.astype(BF16), v_blk))
        o_rks.append(_dot(rk_cat.astype(BF16), v_blk))

    xs = []
    for i, (p, akv) in enumerate(zip(pre, akvs)):
        t_cat = jnp.concatenate([ts[2 * i], ts[2 * i + 1]], axis=1).astype(BF16)
        y_blk = jnp.concatenate([
            jnp.concatenate([p["a_abs"] * m0, akv * m0], axis=1),
            jnp.concatenate([p["a_abs"] * m1, akv * m1], axis=1)], axis=0).astype(BF16)
        xs.append(_dot(t_cat, y_blk))

    s0s = [s_ref[...] for s_ref in s_refs]
    s0bs = [s0.astype(BF16) for s0 in s0s]
    us = [_dot(x[:, 0:PAIR].astype(BF16), s0b) + x[:, PAIR:2 * PAIR] for x, s0b in zip(xs, s0bs)]
    outs = []
    for i in range(n):
        p, u = pre[i], us[i]
        u_blk = jnp.concatenate([u * m0, u * m1], axis=0).astype(BF16)
        outs.append(_dot(p["r_abs"].astype(BF16), s0bs[i]) + _dot(rb_cats[i], u_blk) + o_rks[i])
        kb_t = jnp.concatenate([p["b_end"].T, p["k_end"].T], axis=1).astype(BF16)
        uv = jnp.concatenate([u, p["v"]], axis=0).astype(BF16)
        d_col = jnp.broadcast_to(p["d_tot"], (PAIR, PAIR)).T
        s_refs[i][...] = jnp.where(same_head, d_col * s0s[i] + _dot(kb_t, uv), zero)
    return outs


def _scan_kernel(rf, vf, kkf, lwf, kf, bf, rb, vb, kkb, lwb, kb, bb, of_ref, ob_ref, s_ref):
    @pl.when(pl.program_id(1) == 0)
    def _():
        s_ref[...] = jnp.zeros_like(s_ref)

    n_p = rf.shape[0]
    items, s_refs = [], []
    for p in range(n_p):
        items.append((rf[p], kkf[p], vf[p], lwf[p], kf[p], bf[p], False))
        s_refs.append(s_ref.at[2 * p])
        items.append((rb[p], kkb[p], vb[p], lwb[p], kb[p], bb[p], True))
        s_refs.append(s_ref.at[2 * p + 1])
    outs = _scan_stage(items, s_refs)
    for p in range(n_p):
        of_ref[p] = outs[2 * p]
        ob_ref[p] = outs[2 * p + 1]


def _scan(r, v, kk, lw_f, lw_b, k_f, k_b, b_f, b_b):
    bsz, n_p, t, _ = r.shape
    nc = t // CHUNK
    fwd = pl.BlockSpec((None, n_p, CHUNK, PAIR), lambda bi, c: (bi, 0, c, 0))
    bwd = pl.BlockSpec((None, n_p, CHUNK, PAIR), lambda bi, c: (bi, 0, nc - 1 - c, 0))
    shape = jax.ShapeDtypeStruct((bsz, n_p, t, PAIR), F32)
    return pl.pallas_call(
        _scan_kernel,
        grid=(bsz, nc),
        in_specs=[fwd] * 6 + [bwd] * 6,
        out_specs=[fwd, bwd],
        out_shape=[shape, shape],
        scratch_shapes=[pltpu.VMEM((2 * n_p, PAIR, PAIR), F32)],
        compiler_params=_params(("arbitrary", "arbitrary")),
        name="rwkv_scan",
    )(r, v, kk, lw_f, k_f, b_f, r, v, kk, lw_b, k_b, b_b)


def _na_kernel(q_ref, k_ref, v_ref, gq_ref, gk_ref, tab_ref, ones_ref, o_ref, qn_s, kn_s, vb_s, *, rows):
    ones = ones_ref[...]
    scale = HEAD_DIM ** -0.5
    q = q_ref[...]
    k = k_ref[...]
    inv_d = 1.0 / HEAD_DIM
    qn = q * lax.rsqrt(_mm_exact_rhs(q * q, ones) * inv_d + RMS_EPS) * (gq_ref[...] * scale)
    kn = k * lax.rsqrt(_mm_exact_rhs(k * k, ones) * inv_d + RMS_EPS) * gk_ref[...]
    qn_s[...] = qn.astype(BF16)
    kn_s[...] = kn.astype(BF16)
    vb_s[...] = v_ref[...].astype(BF16)
    lane = lax.broadcasted_iota(jnp.int32, (1, PAIR), 1)
    head0 = lane < HEAD_DIM
    win = NA_WR * GRID_W

    def row_group(gi, carry):
        rws = [gi * NA_ROWS_PER_STEP + j for j in range(NA_ROWS_PER_STEP)]
        rss = [jnp.clip(r - NA_WR // 2, 0, rows - NA_WR) for r in rws]
        q_rows = [qn_s[pl.ds(pl.multiple_of(r * GRID_W, GRID_W), GRID_W), :] for r in rws]
        k_wins = [kn_s[pl.ds(pl.multiple_of(rs * GRID_W, GRID_W), win), :] for rs in rss]
        v_wins = [vb_s[pl.ds(pl.multiple_of(rs * GRID_W, GRID_W), win), :] for rs in rss]
        ss = []
        for j in range(NA_ROWS_PER_STEP):
            d0 = rss[j] - rws[j] + NA_WR - 1
            for e in range(2):
                mask = head0 if e == 0 else jnp.logical_not(head0)
                qm = jnp.where(mask, q_rows[j], jnp.zeros_like(q_rows[j]))
                bias = jnp.concatenate(
                    [tab_ref[e, pl.ds(d0 + 2 * m, 1)][0] for m in range(NA_WR // 2)], axis=1)
                ss.append(_dot_nt(qm, k_wins[j]) + bias)
        mxs = [jnp.max(s, axis=-1, keepdims=True) for s in ss]
        ps = [jnp.exp(s - mx) for s, mx in zip(ss, mxs)]
        ls = [jnp.sum(p, axis=-1, keepdims=True) for p in ps]
        pvs = [_dot(p.astype(BF16), v_wins[i // 2]) for i, p in enumerate(ps)]
        for j in range(NA_ROWS_PER_STEP):
            o0 = pvs[2 * j] / ls[2 * j]
            o1 = pvs[2 * j + 1] / ls[2 * j + 1]
            o_ref[pl.ds(pl.multiple_of(rws[j] * GRID_W, GRID_W), GRID_W), :] = jnp.where(head0, o0, o1)
        return carry

    lax.fori_loop(0, rows // NA_ROWS_PER_STEP, row_group, 0)


def _na_bias_table(rpb):
    qc = np.arange(GRID_W)
    kc = np.arange(GRID_W)
    cs = np.clip(qc - NA_WC // 2, 0, GRID_W - NA_WC)
    valid = (kc[None, :] >= cs[:, None]) & (kc[None, :] < cs[:, None] + NA_WC)
    dc = np.clip(kc[None, :] - qc[:, None] + NA_WC - 1, 0, 2 * NA_WC - 2)
    b = rpb.astype(F32)[:, :, dc]
    b = jnp.where(jnp.asarray(valid)[None, None], b, NEG_BIG)
    return jnp.concatenate([b[:, :-1], b[:, 1:]], axis=-1)


def _na(pb3, q_gain, k_gain, table, d_b):
    bsz, t, _ = pb3.shape
    rows = t // GRID_W
    n_p = d_b // PAIR
    ones = _block_ones(PAIR, HEAD_DIM)
    gq = jnp.tile(q_gain.reshape(1, HEAD_DIM), (1, 2))
    gk = jnp.tile(k_gain.reshape(1, HEAD_DIM), (1, 2))
    n_d = table.shape[1]
    c2 = lambda bi, p: (0, 0)
    return pl.pallas_call(
        functools.partial(_na_kernel, rows=rows),
        grid=(bsz, n_p),
        in_specs=[
            pl.BlockSpec((None, t, PAIR), lambda bi, p: (bi, 0, p)),
            pl.BlockSpec((None, t, PAIR), lambda bi, p: (bi, 0, n_p + p)),
            pl.BlockSpec((None, t, PAIR), lambda bi, p: (bi, 0, 2 * n_p + p)),
            pl.BlockSpec((1, PAIR), c2),
            pl.BlockSpec((1, PAIR), c2),
            pl.BlockSpec((2, n_d, GRID_W, PAIR), lambda bi, p: (p, 0, 0, 0)),
            pl.BlockSpec((PAIR, PAIR), c2),
        ],
        out_specs=pl.BlockSpec((None, t, PAIR), lambda bi, p: (bi, 0, p)),
        out_shape=jax.ShapeDtypeStruct((bsz, t, d_b), F32),
        scratch_shapes=[pltpu.VMEM((t, PAIR), BF16)] * 3,
        compiler_params=_params(("arbitrary", "arbitrary")),
        name="natten",
    )(pb3, pb3, pb3, gq, gk, table, ones)


def _merge_kernel(of_ref, ob_ref, bonus_ref, g_ref, yb_ref, gates_ref, x_ref,
                  lng_ref, lnb_ref, wa_ref, wb_ref, wo_ref, gffn_ref, wr_ref, br_ref,
                  ones_ref, tri_ref,
                  x1_ref, h2_ref, idx_ref, rank_ref, gw_ref, cnt_ref, carry_ref, *, d_model):
    first = jnp.logical_and(pl.program_id(0) == 0, pl.program_id(1) == 0)

    @pl.when(first)
    def _():
        carry_ref[...] = jnp.zeros_like(carry_ref)

    n_p = of_ref.shape[0]
    o = jnp.concatenate([of_ref[p] + ob_ref[p] for p in range(n_p)], axis=1)
    ones = ones_ref[...]
    inv_d = 1.0 / HEAD_DIM
    mu = _mm_exact_rhs(o, ones) * inv_d
    dv = o - mu
    var = _mm_exact_rhs(dv * dv, ones) * inv_d
    y = dv * lax.rsqrt(var + GN_EPS) * lng_ref[...] + lnb_ref[...] + bonus_ref[...]
    ya = y * g_ref[...]

    gates = gates_ref[...]
    pa = _dot(ya.astype(BF16), wa_ref[...])
    pb = _dot(yb_ref[...].astype(BF16), wb_ref[...])
    merged = _sigmoid(gates[:, :d_model]) * pa + _sigmoid(gates[:, d_model:]) * pb
    x1 = x_ref[...] + _dot(merged.astype(BF16), wo_ref[...])
    x1_ref[...] = x1
    ms = jnp.mean(x1 * x1, axis=-1, keepdims=True)
    h2 = x1 * lax.rsqrt(ms + RMS_EPS) * gffn_ref[...]
    h2_ref[...] = h2

    logits = _mm3(h2, wr_ref[...]) + br_ref[...]
    tm = logits.shape[0]
    lane = lax.broadcasted_iota(jnp.int32, (tm, LANES), 1)
    work = logits
    vals, idxs = [], []
    for _ in range(TOP_K):
        m = jnp.max(work, axis=-1, keepdims=True)
        ix = jnp.min(jnp.where(work == m, lane, LANES), axis=-1, keepdims=True)
        vals.append(m)
        idxs.append(ix)
        work = jnp.where(lane == ix, -jnp.inf, work)
    es = [jnp.exp(vk - vals[0]) for vk in vals]
    den = es[0] + es[1] + es[2] + es[3]
    member = jnp.zeros((tm, LANES), F32)
    for ix in idxs:
        member = member + (lane == ix).astype(F32)
    before = _dot(tri_ref[...], member.astype(BF16)) + carry_ref[...]
    idx_out = jnp.zeros((tm, LANES), jnp.int32)
    rank_out = jnp.zeros((tm, LANES), jnp.int32)
    gw_out = jnp.zeros((tm, LANES), F32)
    for kq in range(TOP_K):
        rk = jnp.sum(jnp.where(lane == idxs[kq], before, 0.0), axis=-1, keepdims=True)
        sel = lane == kq
        idx_out = jnp.where(sel, idxs[kq], idx_out)
        rank_out = jnp.where(sel, rk.astype(jnp.int32), rank_out)
        gw_out = jnp.where(sel, es[kq] / den, gw_out)
    idx_ref[...] = idx_out
    rank_ref[...] = rank_out
    gw_ref[...] = gw_out
    carry_ref[...] = carry_ref[...] + jnp.sum(member, axis=0, keepdims=True)
    cnt_ref[...] = carry_ref[...]


def _merge(o_f, o_b, bonus, g, yb, gates3, x3, lnx_g, lnx_b, w_a, w_b, w_o, g_ffn, wr_pad, br_pad, tm=256):
    bsz, n_p, t, _ = o_f.shape
    d_a = n_p * PAIR
    d_b = yb.shape[-1]
    d_model = x3.shape[-1]
    n_t = t // tm
    m = bsz * t
    ones = _block_ones(d_a, HEAD_DIM)
    tri = jnp.asarray(np.tril(np.ones((tm, tm)), -1), BF16)
    c2 = lambda bi, i: (0, 0)
    tok = lambda w: pl.BlockSpec((None, tm, w), lambda bi, i: (bi, i, 0))
    flat = lambda w: pl.BlockSpec((tm, w), lambda bi, i: (bi * n_t + i, 0))
    pair = pl.BlockSpec((None, n_p, tm, PAIR), lambda bi, i: (bi, 0, i, 0))
    return pl.pallas_call(
        functools.partial(_merge_kernel, d_model=d_model),
        grid=(bsz, n_t),
        in_specs=[
            pair, pair, tok(d_a), tok(d_a), tok(d_b), tok(2 * d_model), tok(d_model),
            pl.BlockSpec((1, d_a), c2), pl.BlockSpec((1, d_a), c2),
            pl.BlockSpec((d_a, d_model), c2), pl.BlockSpec((d_b, d_model), c2),
            pl.BlockSpec((d_model, d_model), c2), pl.BlockSpec((1, d_model), c2),
            pl.BlockSpec((d_model, LANES), c2), pl.BlockSpec((1, LANES), c2),
            pl.BlockSpec((d_a, d_a), c2), pl.BlockSpec((tm, tm), c2),
        ],
        out_specs=[flat(d_model), flat(d_model), flat(LANES), flat(LANES), flat(LANES),
                   pl.BlockSpec((1, LANES), c2)],
        out_shape=[
            jax.ShapeDtypeStruct((m, d_model), F32),
            jax.ShapeDtypeStruct((m, d_model), F32),
            jax.ShapeDtypeStruct((m, LANES), jnp.int32),
            jax.ShapeDtypeStruct((m, LANES), jnp.int32),
            jax.ShapeDtypeStruct((m, LANES), F32),
            jax.ShapeDtypeStruct((1, LANES), F32),
        ],
        scratch_shapes=[pltpu.VMEM((1, LANES), F32)],
        compiler_params=_params(("arbitrary", "arbitrary")),
        name="merge_router",
    )(o_f, o_b, bonus, g, yb, gates3, x3, lnx_g, lnx_b, w_a, w_b, w_o, g_ffn, wr_pad, br_pad, ones, tri)


def _dispatch_kernel(dest_ref, h_ref, xs_in_ref, xs_ref, sem):
    del xs_in_ref
    tm = h_ref.shape[0]

    def row_copy(t, kq):
        return pltpu.make_async_copy(
            h_ref.at[pl.ds(t, 1), :], xs_ref.at[pl.ds(dest_ref[0, t * TOP_K + kq], 1), :], sem)

    def issue(t, c):
        for kq in range(TOP_K):
            row_copy(t, kq).start()
        return c

    def drain(t, c):
        for kq in range(TOP_K):
            row_copy(t, kq).wait()
        return c

    lax.fori_loop(0, tm, issue, 0, unroll=4)
    lax.fori_loop(0, tm, drain, 0, unroll=8)


def _dispatch(dest, h2, n_pad, tm=256):
    m, d = h2.shape
    nt = m // tm
    dest3 = dest.reshape(nt, 1, tm * TOP_K)
    xs0 = jnp.zeros((n_pad, d), F32)
    return pl.pallas_call(
        _dispatch_kernel,
        grid=(nt,),
        in_specs=[
            pl.BlockSpec((None, 1, tm * TOP_K), lambda i: (i, 0, 0), memory_space=pltpu.SMEM),
            pl.BlockSpec((tm, d), lambda i: (i, 0)),
            pl.BlockSpec(memory_space=pl.ANY),
        ],
        out_specs=pl.BlockSpec(memory_space=pl.ANY),
        out_shape=jax.ShapeDtypeStruct((n_pad, d), F32),
        scratch_shapes=[pltpu.SemaphoreType.DMA(())],
        input_output_aliases={2: 0},
        compiler_params=_params(("arbitrary",)),
        name="moe_dispatch",
    )(dest3, h2, xs0)


def _expert_kernel(blk_ref, be_ref, nu_ref, xs_ref, w1_ref, b1_ref, w2_ref, b2_ref, ys_ref, *, d_e):
    del blk_ref, be_ref

    @pl.when(pl.program_id(0) < nu_ref[0])
    def _():
        x = xs_ref[...].astype(BF16)
        u = _dot(x, w1_ref[...]) + b1_ref[...]
        glu = jnp.minimum(u[:, :d_e], SWIGLU_LIMIT)
        lin = jnp.clip(u[:, d_e:], -SWIGLU_LIMIT, SWIGLU_LIMIT)
        act = glu * _sigmoid(SWIGLU_ALPHA * glu) * (lin + 1.0)
        ys_ref[...] = _dot(act.astype(BF16), w2_ref[...]) + b2_ref[...]

    @pl.when(pl.program_id(0) >= nu_ref[0])
    def _():
        ys_ref[...] = jnp.zeros_like(ys_ref)


def _experts(blk_idx, blk_e, n_used, xs, w1, b1, w2, b2):
    n_pad, d = xs.shape
    nb = n_pad // MOE_BLOCK
    n_e, _, d2 = w1.shape
    d_e = d2 // 2
    grid_spec = pltpu.PrefetchScalarGridSpec(
        num_scalar_prefetch=3,
        grid=(nb,),
        in_specs=[
            pl.BlockSpec((MOE_BLOCK, d), lambda i, bi, be, nu: (bi[i], 0)),
            pl.BlockSpec((None, d, d2), lambda i, bi, be, nu: (be[i], 0, 0)),
            pl.BlockSpec((None, 1, d2), lambda i, bi, be, nu: (be[i], 0, 0)),
            pl.BlockSpec((None, d_e, d), lambda i, bi, be, nu: (be[i], 0, 0)),
            pl.BlockSpec((None, 1, d), lambda i, bi, be, nu: (be[i], 0, 0)),
        ],
        out_specs=pl.BlockSpec((MOE_BLOCK, d), lambda i, bi, be, nu: (i, 0)),
    )
    return pl.pallas_call(
        functools.partial(_expert_kernel, d_e=d_e),
        grid_spec=grid_spec,
        out_shape=jax.ShapeDtypeStruct((n_pad, d), F32),
        compiler_params=_params(("arbitrary",)),
        name="moe_experts",
    )(blk_idx, blk_e, n_used, xs, w1, b1.reshape(n_e, 1, d2), w2, b2.reshape(n_e, 1, d))


def _combine_kernel(dest_ref, dnext_ref, ys_ref, x1_ref, gw_ref, o_ref, buf, sems):
    i = pl.program_id(0)
    nt = pl.num_programs(0)
    tm = x1_ref.shape[0]
    slot = i % 2

    def row_copy(d_ref, sl, t, kq):
        return pltpu.make_async_copy(
            ys_ref.at[pl.ds(d_ref[0, t * TOP_K + kq], 1), :], buf.at[sl, kq, pl.ds(t, 1), :], sems.at[sl])

    def issue_all(d_ref, sl):
        def body(t, c):
            for kq in range(TOP_K):
                row_copy(d_ref, sl, t, kq).start()
            return c
        lax.fori_loop(0, tm, body, 0, unroll=4)

    @pl.when(i == 0)
    def _():
        issue_all(dest_ref, 0)

    @pl.when(i + 1 < nt)
    def _():
        issue_all(dnext_ref, 1 - slot)

    def drain(t, c):
        for kq in range(TOP_K):
            row_copy(dest_ref, slot, t, kq).wait()
        return c

    lax.fori_loop(0, tm, drain, 0, unroll=8)
    gw = gw_ref[...]
    acc = x1_ref[...]
    for kq in range(TOP_K):
        acc = acc + gw[:, kq:kq + 1] * buf[slot, kq]
    o_ref[...] = acc


def _combine(dest, ys, x1, gw, tm=128):
    m, d = x1.shape
    nt = m // tm
    dest3 = dest.reshape(nt, 1, tm * TOP_K)
    return pl.pallas_call(
        _combine_kernel,
        grid=(nt,),
        in_specs=[
            pl.BlockSpec((None, 1, tm * TOP_K), lambda i: (i, 0, 0), memory_space=pltpu.SMEM),
            pl.BlockSpec((None, 1, tm * TOP_K), lambda i: (jnp.minimum(i + 1, nt - 1), 0, 0),
                         memory_space=pltpu.SMEM),
            pl.BlockSpec(memory_space=pl.ANY),
            pl.BlockSpec((tm, d), lambda i: (i, 0)),
            pl.BlockSpec((tm, LANES), lambda i: (i, 0)),
        ],
        out_specs=pl.BlockSpec((tm, d), lambda i: (i, 0)),
        out_shape=jax.ShapeDtypeStruct((m, d), F32),
        scratch_shapes=[pltpu.VMEM((2, TOP_K, tm, d), F32), pltpu.SemaphoreType.DMA((2,))],
        compiler_params=_params(("arbitrary",)),
        name="moe_combine",
    )(dest3, dest3, ys, x1, gw)


def _blockdiag2(a, b):
    za = jnp.zeros((a.shape[0], b.shape[1]), a.dtype)
    zb = jnp.zeros((b.shape[0], a.shape[1]), a.dtype)
    return jnp.concatenate([jnp.concatenate([a, za], axis=1), jnp.concatenate([zb, b], axis=1)], axis=0)


def _layer(x, g_mix, w_in, mu_prev, mu_next, w0_f, w2_f, w0_b, w2_b, a0_f, a2_f, a0_b, a2_b,
           g2, k_k, k_a, r_k, lnx_g, lnx_b, q_norm_g, k_norm_g, rpb, w_a, w_b, w_o,
           g_ffn, w_router, b_router, w1, b1, w2, b2):
    bsz, t, d_model = x.shape
    m = bsz * t
    d_a = w_a.shape[0]
    d_b = w_b.shape[0]
    a_cols = mu_prev.shape[0]
    b_cols = 3 * d_b
    row = lambda a: a.reshape(1, -1).astype(F32)

    w_in_b = w_in.astype(BF16)
    pa, pb, gates = _inproj(x.reshape(m, d_model), row(g_mix), w_in_b[:, :a_cols],
                            w_in_b[:, a_cols:a_cols + b_cols], w_in_b[:, a_cols + b_cols:])

    prep = _prep(pa.reshape(bsz, t, a_cols), row(mu_prev), row(mu_next),
                 jnp.concatenate([row(w0_f), row(w0_b)], axis=1), _blockdiag2(w2_f, w2_b),
                 jnp.concatenate([row(a0_f), row(a0_b)], axis=1), _blockdiag2(a2_f, a2_b),
                 g2, row(k_k), row(k_a), row(r_k), d_a)
    r, v, kk, lw_f, lw_b, k_f, k_b, b_f, b_b, bonus, g = prep
    o_f, o_b = _scan(r, v, kk, lw_f, lw_b, k_f, k_b, b_f, b_b)

    yb = _na(pb.reshape(bsz, t, b_cols), q_norm_g, k_norm_g, _na_bias_table(rpb), d_b)

    n_e = w_router.shape[1]
    wr_pad = jnp.zeros((d_model, LANES), F32).at[:, :n_e].set(w_router)
    br_pad = jnp.full((1, LANES), NEG_BIG, F32).at[0, :n_e].set(b_router)
    x1, h2, idx, rank, gw, cnt = _merge(
        o_f, o_b, bonus, g, yb, gates.reshape(bsz, t, 2 * d_model), x,
        row(lnx_g), row(lnx_b), w_a.astype(BF16), w_b.astype(BF16), w_o.astype(BF16),
        row(g_ffn), wr_pad, br_pad)

    counts = cnt[0, :n_e].astype(jnp.int32)
    padded = ((counts + MOE_BLOCK - 1) // MOE_BLOCK) * MOE_BLOCK
    pend = jnp.cumsum(padded)
    pstart = pend - padded
    n_assign = m * TOP_K
    n_blocks = -(-n_assign // MOE_BLOCK) + n_e
    n_pad = n_blocks * MOE_BLOCK
    top_idx = idx[:, :TOP_K]
    dest = (pstart[top_idx] + rank[:, :TOP_K]).reshape(-1).astype(jnp.int32)
    n_used = (pend[-1] // MOE_BLOCK).astype(jnp.int32)
    blk_idx = jnp.minimum(jnp.arange(n_blocks, dtype=jnp.int32), n_used - 1)
    blk_e = jnp.sum((blk_idx[:, None] * MOE_BLOCK >= pend[None, :]).astype(jnp.int32), axis=1)
    blk_e = jnp.minimum(blk_e, n_e - 1)

    xs = _dispatch(dest, h2, n_pad)
    ys = _experts(blk_idx, blk_e, n_used.reshape(1), xs, w1.astype(BF16), b1, w2.astype(BF16), b2)
    out = _combine(dest, ys, x1, gw)
    return out.reshape(bsz, t, d_model)


def kernel(x, g_mix, w_in, mu_prev, mu_next, w0_f, w2_f, w0_b, w2_b, a0_f, a2_f, a0_b, a2_b, g2, k_k, k_a, r_k, lnx_g, lnx_b, q_norm_g, k_norm_g, rpb, w_a, w_b, w_o, g_ffn, w_router, b_router, w1, b1, w2, b2):
    for l in range(g_mix.shape[0]):
        x = _layer(x, g_mix[l], w_in[l], mu_prev[l], mu_next[l], w0_f[l], w2_f[l], w0_b[l], w2_b[l],
                   a0_f[l], a2_f[l], a0_b[l], a2_b[l], g2[l], k_k[l], k_a[l], r_k[l], lnx_g[l], lnx_b[l],
                   q_norm_g[l], k_norm_g[l], rpb[l], w_a[l], w_b[l], w_o[l], g_ffn[l], w_router[l],
                   b_router[l], w1[l], b1[l], w2[l], b2[l])
    return x
```

```python
import functools
import math

import numpy as np
import jax
import jax.numpy as jnp
from jax import lax
from jax.experimental import pallas as pl
from jax.experimental.pallas import tpu as pltpu

F32 = jnp.float32
BF16 = jnp.bfloat16

LANES = 128
HEAD_DIM = 64
PAIR = 2 * HEAD_DIM
GRID_W = 64
NA_WR = 8
NA_WC = 16
W_LORA = 64
A_LORA = 64
G_LORA = 128
DECAY_SCALE = math.exp(-0.5)
GN_EPS = 64e-5
RMS_EPS = 1e-5
N_EXPERTS = 32
TOP_K = 4
MOE_BLOCK = 512
SWIGLU_LIMIT = 7.0
SWIGLU_ALPHA = 1.702
NEG_BIG = -1e30
CHUNK = 128
NA_ROWS_PER_STEP = 8
VMEM_LIMIT = 56 * 1024 * 1024


def _dot(a, b):
    return jnp.dot(a, b, preferred_element_type=F32)


def _dot_nt(a, b):
    return lax.dot_general(a, b, (((1,), (1,)), ((), ())), preferred_element_type=F32)


def _split(a):
    hi = a.astype(BF16)
    lo = (a - hi.astype(F32)).astype(BF16)
    return hi, lo


def _mm_exact_rhs(a, b_bf16):
    hi, lo = _split(a)
    return _dot(hi, b_bf16) + _dot(lo, b_bf16)


def _hilo(b):
    return jnp.stack(_split(b))


def _mm3(a, b_hilo):
    ah, al = _split(a)
    bh = b_hilo[0]
    return _dot(ah, bh) + _dot(al, bh) + _dot(ah, b_hilo[1])


def _sigmoid(x):
    return 1.0 / (1.0 + jnp.exp(-x))


def _params(sem):
    return pltpu.CompilerParams(dimension_semantics=sem, vmem_limit_bytes=VMEM_LIMIT)


def _block_ones(n, blk):
    i = np.arange(n) // blk
    return jnp.asarray(i[:, None] == i[None, :], BF16)


def _inproj_kernel(x_ref, g_ref, wa_ref, wb_ref, wg_ref, pa_ref, pb_ref, pg_ref):
    x = x_ref[...]
    ms = jnp.mean(x * x, axis=-1, keepdims=True)
    h = (x * lax.rsqrt(ms + RMS_EPS) * g_ref[...]).astype(BF16)
    pa_ref[...] = _dot(h, wa_ref[...])
    pb_ref[...] = _dot(h, wb_ref[...])
    pg_ref[...] = _dot(h, wg_ref[...])


def _inproj(x2, g_mix, w_a, w_b, w_g, tm=256):
    m, d = x2.shape
    na, nb, ng = w_a.shape[1], w_b.shape[1], w_g.shape[1]
    full = lambda i: (0, 0)
    return pl.pallas_call(
        _inproj_kernel,
        grid=(m // tm,),
        in_specs=[
            pl.BlockSpec((tm, d), lambda i: (i, 0)),
            pl.BlockSpec((1, d), full),
            pl.BlockSpec((d, na), full),
            pl.BlockSpec((d, nb), full),
            pl.BlockSpec((d, ng), full),
        ],
        out_specs=[
            pl.BlockSpec((tm, na), lambda i: (i, 0)),
            pl.BlockSpec((tm, nb), lambda i: (i, 0)),
            pl.BlockSpec((tm, ng), lambda i: (i, 0)),
        ],
        out_shape=[
            jax.ShapeDtypeStruct((m, na), F32),
            jax.ShapeDtypeStruct((m, nb), F32),
            jax.ShapeDtypeStruct((m, ng), F32),
        ],
        compiler_params=_params(("arbitrary",)),
        name="inproj",
    )(x2, g_mix, w_a, w_b, w_g)


def _prep_kernel(p_ref, prev_ref, next_ref, mup_ref, mun_ref, w0_ref, w2_ref, a0_ref, a2_ref,
                 g2_ref, kk_ref, ka_ref, rk_ref, ones_ref,
                 r_o, v_o, kk_o, lwf_o, lwb_o, kf_o, kb_o, bf_o, bb_o, bonus_o, g_o, *, d_a):
    i = pl.program_id(1)
    n_t = pl.num_programs(1)
    p = p_ref[...]
    tt = p.shape[0]
    prow = jnp.where(i > 0, prev_ref[7:8, :], 0.0)
    nrow = jnp.where(i < n_t - 1, next_ref[0:1, :], 0.0)
    rid = lax.broadcasted_iota(jnp.int32, (tt, 1), 0)
    prev = jnp.where(rid == 0, prow, pltpu.roll(p, 1, axis=0))
    nxt = jnp.where(rid == tt - 1, nrow, pltpu.roll(p, tt - 1, axis=0))
    xa = p + mup_ref[...] * (prev - p) + mun_ref[...] * (nxt - p)

    r = xa[:, 0:d_a]
    k = xa[:, d_a:2 * d_a]
    v = xa[:, 2 * d_a:3 * d_a]
    o = 3 * d_a
    lw = xa[:, o:o + 2 * W_LORA]
    la = xa[:, o + 2 * W_LORA:o + 2 * W_LORA + 2 * A_LORA]
    lg = xa[:, o + 2 * W_LORA + 2 * A_LORA:]

    dpre = w0_ref[...] + _mm3(jnp.tanh(lw), w2_ref)
    apre = a0_ref[...] + _mm3(la, a2_ref)
    g = _mm3(_sigmoid(lg), g2_ref)
    logw = -DECAY_SCALE * _sigmoid(dpre)
    a = _sigmoid(apre)

    ones = ones_ref[...]
    kkr = k * kk_ref[...]
    ss = _mm_exact_rhs(kkr * kkr, ones)
    kk = kkr / jnp.maximum(jnp.sqrt(ss), 1e-12)

    ka = ka_ref[...]
    k_f = k * (1.0 + (a[:, :d_a] - 1.0) * ka)
    k_b = k * (1.0 + (a[:, d_a:] - 1.0) * ka)
    b_f = kk * a[:, :d_a]
    b_b = kk * a[:, d_a:]
    rk = rk_ref[...]
    bon = _mm_exact_rhs(r * (k_f + k_b) * rk, ones) * v

    bonus_o[...] = bon
    g_o[...] = g
    for pi in range(d_a // PAIR):
        sl = slice(pi * PAIR, (pi + 1) * PAIR)
        r_o[pi] = r[:, sl]
        v_o[pi] = v[:, sl]
        kk_o[pi] = kk[:, sl]
        lwf_o[pi] = logw[:, sl]
        lwb_o[pi] = logw[:, d_a + pi * PAIR:d_a + (pi + 1) * PAIR]
        kf_o[pi] = k_f[:, sl]
        kb_o[pi] = k_b[:, sl]
        bf_o[pi] = b_f[:, sl]
        bb_o[pi] = b_b[:, sl]


def _prep(pa3, mu_prev, mu_next, w0c, w2blk, a0c, a2blk, g2, k_k, k_a, r_k, d_a, tt=256):
    b, t, ac = pa3.shape
    n_t = t // tt
    n_p = d_a // PAIR
    ones = _block_ones(d_a, HEAD_DIM)
    c2 = lambda bi, i: (0, 0)
    c3 = lambda bi, i: (0, 0, 0)
    pair_spec = pl.BlockSpec((None, n_p, tt, PAIR), lambda bi, i: (bi, 0, i, 0))
    pair_shape = jax.ShapeDtypeStruct((b, n_p, t, PAIR), F32)
    flat_spec = pl.BlockSpec((None, tt, d_a), lambda bi, i: (bi, i, 0))
    flat_shape = jax.ShapeDtypeStruct((b, t, d_a), F32)
    r8 = tt // 8
    return pl.pallas_call(
        functools.partial(_prep_kernel, d_a=d_a),
        grid=(b, n_t),
        in_specs=[
            pl.BlockSpec((None, tt, ac), lambda bi, i: (bi, i, 0)),
            pl.BlockSpec((None, 8, ac), lambda bi, i: (bi, jnp.maximum(i * r8 - 1, 0), 0)),
            pl.BlockSpec((None, 8, ac), lambda bi, i: (bi, jnp.minimum((i + 1) * r8, t // 8 - 1), 0)),
            pl.BlockSpec((1, ac), c2),
            pl.BlockSpec((1, ac), c2),
            pl.BlockSpec((1, 2 * d_a), c2),
            pl.BlockSpec((2, 2 * W_LORA, 2 * d_a), c3),
            pl.BlockSpec((1, 2 * d_a), c2),
            pl.BlockSpec((2, 2 * A_LORA, 2 * d_a), c3),
            pl.BlockSpec((2, G_LORA, d_a), c3),
            pl.BlockSpec((1, d_a), c2),
            pl.BlockSpec((1, d_a), c2),
            pl.BlockSpec((1, d_a), c2),
            pl.BlockSpec((d_a, d_a), c2),
        ],
        out_specs=[pair_spec] * 9 + [flat_spec] * 2,
        out_shape=[pair_shape] * 9 + [flat_shape] * 2,
        compiler_params=_params(("arbitrary", "arbitrary")),
        name="rwkv_prep",
    )(pa3, pa3, pa3, mu_prev, mu_next, w0c, _hilo(w2blk), a0c, _hilo(a2blk), _hilo(g2), k_k, k_a, r_k, ones)


def _mm_exact_rhs_left(tri_bf16, x):
    hi, lo = _split(x)
    return _dot(tri_bf16, hi) + _dot(tri_bf16, lo)


def _scan_stage(items, s_refs):
    c = items[0][0].shape[0]
    ri = lax.broadcasted_iota(jnp.int32, (c, c), 0)
    ci = lax.broadcasted_iota(jnp.int32, (c, c), 1)
    lane = lax.broadcasted_iota(jnp.int32, (1, PAIR), 1)
    m0 = (lane < HEAD_DIM).astype(F32)
    m1 = 1.0 - m0
    eye = (ri == ci).astype(F32)
    hi_ = lax.broadcasted_iota(jnp.int32, (PAIR, PAIR), 0) // HEAD_DIM
    hj_ = lax.broadcasted_iota(jnp.int32, (PAIR, PAIR), 1) // HEAD_DIM
    same_head = hi_ == hj_
    zero = jnp.zeros((), F32)
    n = len(items)

    def masks(reverse):
        if reverse:
            return ci >= ri, ci > ri, 0
        return ci <= ri, ci < ri, c - 1

    cums = [_mm_exact_rhs_left(masks(it[6])[0].astype(BF16), it[3]) for it in items]

    pre = []
    for (r, kk, v, lw, k, b, reverse), cum in zip(items, cums):
        end = masks(reverse)[2]
        cmid = cum[c // 2:c // 2 + 1, :]
        cend = cum[end:end + 1, :]
        r_abs = r * jnp.exp(cum)
        a_abs = -kk * jnp.exp(cum - lw)
        to_mid = jnp.exp(-cmid)
        from_mid = jnp.exp(cmid - cum)
        to_end = jnp.exp(cend - cum)
        pre.append(dict(
            r_abs=r_abs, a_abs=a_abs, r_rel=r_abs * to_mid, a_rel=a_abs * to_mid,
            k_rel=k * from_mid, b_rel=b * from_mid, k_end=k * to_end, b_end=b * to_end,
            d_tot=jnp.exp(cend), v=v))

    grams = []
    for p in pre:
        lhs = jnp.concatenate([p["r_rel"] * m0, p["r_rel"] * m1, p["a_rel"] * m0, p["a_rel"] * m1],
                              axis=0).astype(BF16)
        rhs = jnp.concatenate([p["k_rel"], p["b_rel"]], axis=0).astype(BF16)
        grams.append(_dot_nt(lhs, rhs))

    pws, ts = [], []
    for it, gram in zip(items, grams):
        strict = masks(it[6])[1]
        for e in range(2):
            a_ab = jnp.where(strict, gram[(2 + e) * c:(3 + e) * c, c:2 * c], zero)
            pws.append(a_ab)
            ts.append(eye + a_ab)
    for _ in range(int(round(math.log2(c))) - 1):
        pws = [_dot(pw.astype(BF16), pw.astype(BF16)) for pw in pws]
        ts = [t + _dot(t.astype(BF16), pw.astype(BF16)) for t, pw in zip(ts, pws)]

    v_blks = [jnp.concatenate([p["v"] * m0, p["v"] * m1], axis=0).astype(BF16) for p in pre]
    akvs, o_rks, rb_cats = [], [], []
    for it, gram, v_blk in zip(items, grams, v_blks):
        incl, strict, _ = masks(it[6])
        ak_cat = jnp.concatenate(
            [jnp.where(strict, gram[(2 + e) * c:(3 + e) * c, 0:c], zero) for e in range(2)], axis=1)
        rk_cat = jnp.concatenate(
            [jnp.where(incl, gram[e * c:(e + 1) * c, 0:c], zero) for e in range(2)], axis=1)
        rb_cats.append(jnp.concatenate(
            [jnp.where(incl, gram[e * c:(e + 1) * c, c:2 * c], zero) for e in range(2)], axis=1).astype(BF16))
        akvs.append(_dot(ak_cat.astype(BF16), v_blk))
        o_rks.append(_dot(rk_cat.astype(BF16), v_blk))

    xs = []
    for i, (p, akv) in enumerate(zip(pre, akvs)):
        t_cat = jnp.concatenate([ts[2 * i], ts[2 * i + 1]], axis=1).astype(BF16)
        y_blk = jnp.concatenate([
            jnp.concatenate([p["a_abs"] * m0, akv * m0], axis=1),
            jnp.concatenate([p["a_abs"] * m1, akv * m1], axis=1)], axis=0).astype(BF16)
        xs.append(_dot(t_cat, y_blk))

    s0s = [s_ref[...] for s_ref in s_refs]
    s0bs = [s0.astype(BF16) for s0 in s0s]
    us = [_dot(x[:, 0:PAIR].astype(BF16), s0b) + x[:, PAIR:2 * PAIR] for x, s0b in zip(xs, s0bs)]
    outs = []
    for i in range(n):
        p, u = pre[i], us[i]
        u_blk = jnp.concatenate([u * m0, u * m1], axis=0).astype(BF16)
        outs.append(_dot(p["r_abs"].astype(BF16), s0bs[i]) + _dot(rb_cats[i], u_blk) + o_rks[i])
        kb_t = jnp.concatenate([p["b_end"].T, p["k_end"].T], axis=1).astype(BF16)
        uv = jnp.concatenate([u, p["v"]], axis=0).astype(BF16)
        d_col = jnp.broadcast_to(p["d_tot"], (PAIR, PAIR)).T
        s_refs[i][...] = jnp.where(same_head, d_col * s0s[i] + _dot(kb_t, uv), zero)
    return outs


def _scan_kernel(rf, vf, kkf, lwf, kf, bf, rb, vb, kkb, lwb, kb, bb, of_ref, ob_ref, s_ref):
    @pl.when(pl.program_id(1) == 0)
    def _():
        s_ref[...] = jnp.zeros_like(s_ref)

    n_p = rf.shape[0]
    items, s_refs = [], []
    for p in range(n_p):
        items.append((rf[p], kkf[p], vf[p], lwf[p], kf[p], bf[p], False))
        s_refs.append(s_ref.at[2 * p])
        items.append((rb[p], kkb[p], vb[p], lwb[p], kb[p], bb[p], True))
        s_refs.append(s_ref.at[2 * p + 1])
    outs = _scan_stage(items, s_refs)
    for p in range(n_p):
        of_ref[p] = outs[2 * p]
        ob_ref[p] = outs[2 * p + 1]


def _scan(r, v, kk, lw_f, lw_b, k_f, k_b, b_f, b_b):
    bsz, n_p, t, _ = r.shape
    nc = t // CHUNK
    fwd = pl.BlockSpec((None, n_p, CHUNK, PAIR), lambda bi, c: (bi, 0, c, 0))
    bwd = pl.BlockSpec((None, n_p, CHUNK, PAIR), lambda bi, c: (bi, 0, nc - 1 - c, 0))
    shape = jax.ShapeDtypeStruct((bsz, n_p, t, PAIR), F32)
    return pl.pallas_call(
        _scan_kernel,
        grid=(bsz, nc),
        in_specs=[fwd] * 6 + [bwd] * 6,
        out_specs=[fwd, bwd],
        out_shape=[shape, shape],
        scratch_shapes=[pltpu.VMEM((2 * n_p, PAIR, PAIR), F32)],
        compiler_params=_params(("arbitrary", "arbitrary")),
        name="rwkv_scan",
    )(r, v, kk, lw_f, k_f, b_f, r, v, kk, lw_b, k_b, b_b)


def _na_kernel(q_ref, k_ref, v_ref, gq_ref, gk_ref, tab_ref, ones_ref, o_ref, qn_s, kn_s, vb_s, *, rows):
    ones = ones_ref[...]
    scale = HEAD_DIM ** -0.5
    q = q_ref[...]
    k = k_ref[...]
    inv_d = 1.0 / HEAD_DIM
    qn = q * lax.rsqrt(_mm_exact_rhs(q * q, ones) * inv_d + RMS_EPS) * (gq_ref[...] * scale)
    kn = k * lax.rsqrt(_mm_exact_rhs(k * k, ones) * inv_d + RMS_EPS) * gk_ref[...]
    qn_s[...] = qn.astype(BF16)
    kn_s[...] = kn.astype(BF16)
    vb_s[...] = v_ref[...].astype(BF16)
    lane = lax.broadcasted_iota(jnp.int32, (1, PAIR), 1)
    head0 = lane < HEAD_DIM
    win = NA_WR * GRID_W

    def row_group(gi, carry):
        rws = [gi * NA_ROWS_PER_STEP + j for j in range(NA_ROWS_PER_STEP)]
        rss = [jnp.clip(r - NA_WR // 2, 0, rows - NA_WR) for r in rws]
        q_rows = [qn_s[pl.ds(pl.multiple_of(r * GRID_W, GRID_W), GRID_W), :] for r in rws]
        k_wins = [kn_s[pl.ds(pl.multiple_of(rs * GRID_W, GRID_W), win), :] for rs in rss]
        v_wins = [vb_s[pl.ds(pl.multiple_of(rs * GRID_W, GRID_W), win), :] for rs in rss]
        ss = []
        for j in range(NA_ROWS_PER_STEP):
            d0 = rss[j] - rws[j] + NA_WR - 1
            for e in range(2):
                mask = head0 if e == 0 else jnp.logical_not(head0)
                qm = jnp.where(mask, q_rows[j], jnp.zeros_like(q_rows[j]))
                bias = jnp.concatenate(
                    [tab_ref[e, pl.ds(d0 + 2 * m, 1)][0] for m in range(NA_WR // 2)], axis=1)
                ss.append(_dot_nt(qm, k_wins[j]) + bias)
        mxs = [jnp.max(s, axis=-1, keepdims=True) for s in ss]
        ps = [jnp.exp(s - mx) for s, mx in zip(ss, mxs)]
        ls = [jnp.sum(p, axis=-1, keepdims=True) for p in ps]
        pvs = [_dot(p.astype(BF16), v_wins[i // 2]) for i, p in enumerate(ps)]
        for j in range(NA_ROWS_PER_STEP):
            o0 = pvs[2 * j] / ls[2 * j]
            o1 = pvs[2 * j + 1] / ls[2 * j + 1]
            o_ref[pl.ds(pl.multiple_of(rws[j] * GRID_W, GRID_W), GRID_W), :] = jnp.where(head0, o0, o1)
        return carry

    lax.fori_loop(0, rows // NA_ROWS_PER_STEP, row_group, 0)


def _na_bias_table(rpb):
    qc = np.arange(GRID_W)
    kc = np.arange(GRID_W)
    cs = np.clip(qc - NA_WC // 2, 0, GRID_W - NA_WC)
    valid = (kc[None, :] >= cs[:, None]) & (kc[None, :] < cs[:, None] + NA_WC)
    dc = np.clip(kc[None, :] - qc[:, None] + NA_WC - 1, 0, 2 * NA_WC - 2)
    b = rpb.astype(F32)[:, :, dc]
    b = jnp.where(jnp.asarray(valid)[None, None], b, NEG_BIG)
    return jnp.concatenate([b[:, :-1], b[:, 1:]], axis=-1)


def _na(pb3, q_gain, k_gain, table, d_b):
    bsz, t, _ = pb3.shape
    rows = t // GRID_W
    n_p = d_b // PAIR
    ones = _block_ones(PAIR, HEAD_DIM)
    gq = jnp.tile(q_gain.reshape(1, HEAD_DIM), (1, 2))
    gk = jnp.tile(k_gain.reshape(1, HEAD_DIM), (1, 2))
    n_d = table.shape[1]
    c2 = lambda bi, p: (0, 0)
    return pl.pallas_call(
        functools.partial(_na_kernel, rows=rows),
        grid=(bsz, n_p),
        in_specs=[
            pl.BlockSpec((None, t, PAIR), lambda bi, p: (bi, 0, p)),
            pl.BlockSpec((None, t, PAIR), lambda bi, p: (bi, 0, n_p + p)),
            pl.BlockSpec((None, t, PAIR), lambda bi, p: (bi, 0, 2 * n_p + p)),
            pl.BlockSpec((1, PAIR), c2),
            pl.BlockSpec((1, PAIR), c2),
            pl.BlockSpec((2, n_d, GRID_W, PAIR), lambda bi, p: (p, 0, 0, 0)),
            pl.BlockSpec((PAIR, PAIR), c2),
        ],
        out_specs=pl.BlockSpec((None, t, PAIR), lambda bi, p: (bi, 0, p)),
        out_shape=jax.ShapeDtypeStruct((bsz, t, d_b), F32),
        scratch_shapes=[pltpu.VMEM((t, PAIR), BF16)] * 3,
        compiler_params=_params(("arbitrary", "arbitrary")),
        name="natten",
    )(pb3, pb3, pb3, gq, gk, table, ones)


def _merge_kernel(of_ref, ob_ref, bonus_ref, g_ref, yb_ref, gates_ref, x_ref,
                  lng_ref, lnb_ref, wa_ref, wb_ref, wo_ref, gffn_ref, wr_ref, br_ref,
                  ones_ref, tri_ref,
                  x1_ref, h2_ref, idx_ref, rank_ref, gw_ref, cnt_ref, carry_ref, *, d_model):
    first = jnp.logical_and(pl.program_id(0) == 0, pl.program_id(1) == 0)

    @pl.when(first)
    def _():
        carry_ref[...] = jnp.zeros_like(carry_ref)

    n_p = of_ref.shape[0]
    o = jnp.concatenate([of_ref[p] + ob_ref[p] for p in range(n_p)], axis=1)
    ones = ones_ref[...]
    inv_d = 1.0 / HEAD_DIM
    mu = _mm_exact_rhs(o, ones) * inv_d
    dv = o - mu
    var = _mm_exact_rhs(dv * dv, ones) * inv_d
    y = dv * lax.rsqrt(var + GN_EPS) * lng_ref[...] + lnb_ref[...] + bonus_ref[...]
    ya = y * g_ref[...]

    gates = gates_ref[...]
    pa = _dot(ya.astype(BF16), wa_ref[...])
    pb = _dot(yb_ref[...].astype(BF16), wb_ref[...])
    merged = _sigmoid(gates[:, :d_model]) * pa + _sigmoid(gates[:, d_model:]) * pb
    x1 = x_ref[...] + _dot(merged.astype(BF16), wo_ref[...])
    x1_ref[...] = x1
    ms = jnp.mean(x1 * x1, axis=-1, keepdims=True)
    h2 = x1 * lax.rsqrt(ms + RMS_EPS) * gffn_ref[...]
    h2_ref[...] = h2

    logits = _mm3(h2, wr_ref) + br_ref[...]
    tm = logits.shape[0]
    lane = lax.broadcasted_iota(jnp.int32, (tm, LANES), 1)
    work = logits
    vals, idxs = [], []
    for _ in range(TOP_K):
        m = jnp.max(work, axis=-1, keepdims=True)
        ix = jnp.min(jnp.where(work == m, lane, LANES), axis=-1, keepdims=True)
        vals.append(m)
        idxs.append(ix)
        work = jnp.where(lane == ix, -jnp.inf, work)
    es = [jnp.exp(vk - vals[0]) for vk in vals]
    den = es[0] + es[1] + es[2] + es[3]
    member = jnp.zeros((tm, LANES), F32)
    for ix in idxs:
        member = member + (lane == ix).astype(F32)
    before = _dot(tri_ref[...], member.astype(BF16)) + carry_ref[...]
    idx_out = jnp.zeros((tm, LANES), jnp.int32)
    rank_out = jnp.zeros((tm, LANES), jnp.int32)
    gw_out = jnp.zeros((tm, LANES), F32)
    for kq in range(TOP_K):
        rk = jnp.sum(jnp.where(lane == idxs[kq], before, 0.0), axis=-1, keepdims=True)
        sel = lane == kq
        idx_out = jnp.where(sel, idxs[kq], idx_out)
        rank_out = jnp.where(sel, rk.astype(jnp.int32), rank_out)
        gw_out = jnp.where(sel, es[kq] / den, gw_out)
    idx_ref[...] = idx_out
    rank_ref[...] = rank_out
    gw_ref[...] = gw_out
    carry_ref[...] = carry_ref[...] + jnp.sum(member, axis=0, keepdims=True)
    cnt_ref[...] = carry_ref[...]


def _merge(o_f, o_b, bonus, g, yb, gates3, x3, lnx_g, lnx_b, w_a, w_b, w_o, g_ffn, wr_pad, br_pad, tm=512):
    bsz, n_p, t, _ = o_f.shape
    d_a = n_p * PAIR
    d_b = yb.shape[-1]
    d_model = x3.shape[-1]
    n_t = t // tm
    m = bsz * t
    ones = _block_ones(d_a, HEAD_DIM)
    tri = jnp.asarray(np.tril(np.ones((tm, tm)), -1), BF16)
    c2 = lambda bi, i: (0, 0)
    tok = lambda w: pl.BlockSpec((None, tm, w), lambda bi, i: (bi, i, 0))
    flat = lambda w: pl.BlockSpec((tm, w), lambda bi, i: (bi * n_t + i, 0))
    pair = pl.BlockSpec((None, n_p, tm, PAIR), lambda bi, i: (bi, 0, i, 0))
    return pl.pallas_call(
        functools.partial(_merge_kernel, d_model=d_model),
        grid=(bsz, n_t),
        in_specs=[
            pair, pair, tok(d_a), tok(d_a), tok(d_b), tok(2 * d_model), tok(d_model),
            pl.BlockSpec((1, d_a), c2), pl.BlockSpec((1, d_a), c2),
            pl.BlockSpec((d_a, d_model), c2), pl.BlockSpec((d_b, d_model), c2),
            pl.BlockSpec((d_model, d_model), c2), pl.BlockSpec((1, d_model), c2),
            pl.BlockSpec((2, d_model, LANES), lambda bi, i: (0, 0, 0)), pl.BlockSpec((1, LANES), c2),
            pl.BlockSpec((d_a, d_a), c2), pl.BlockSpec((tm, tm), c2),
        ],
        out_specs=[flat(d_model), flat(d_model), flat(LANES), flat(LANES), flat(LANES),
                   pl.BlockSpec((1, LANES), c2)],
        out_shape=[
            jax.ShapeDtypeStruct((m, d_model), F32),
            jax.ShapeDtypeStruct((m, d_model), F32),
            jax.ShapeDtypeStruct((m, LANES), jnp.int32),
            jax.ShapeDtypeStruct((m, LANES), jnp.int32),
            jax.ShapeDtypeStruct((m, LANES), F32),
            jax.ShapeDtypeStruct((1, LANES), F32),
        ],
        scratch_shapes=[pltpu.VMEM((1, LANES), F32)],
        compiler_params=_params(("arbitrary", "arbitrary")),
        name="merge_router",
    )(o_f, o_b, bonus, g, yb, gates3, x3, lnx_g, lnx_b, w_a, w_b, w_o, g_ffn, _hilo(wr_pad), br_pad, ones, tri)


def _dispatch_kernel(ps_ref, pn_ref, nu_ref, dest_ref, h_ref, xs_ref, zbuf, sem, zsem, bsem):
    i = pl.program_id(0)
    tm = h_ref.shape[0]
    n_e = ps_ref.shape[0]
    nb = xs_ref.shape[0] // MOE_BLOCK

    def row_copy(t, kq):
        return pltpu.make_async_copy(
            h_ref.at[pl.ds(t, 1), :], xs_ref.at[pl.ds(dest_ref[0, t * TOP_K + kq], 1), :], sem)

    def pad_copy(e, j):
        return pltpu.make_async_copy(
            zbuf.at[pl.ds(0, 1), :], xs_ref.at[pl.ds(ps_ref[e] + j, 1), :], zsem)

    def blk_copy(b):
        return pltpu.make_async_copy(
            zbuf, xs_ref.at[pl.ds(pl.multiple_of(b * MOE_BLOCK, MOE_BLOCK), MOE_BLOCK), :], bsem)

    def pads(fn):
        def per_expert(e, c):
            def body(j, c2):
                fn(pad_copy(e, j))
                return c2
            return lax.fori_loop(0, pn_ref[e], body, c)
        lax.fori_loop(0, n_e, per_expert, 0)

        def per_block(b, c):
            fn(blk_copy(b))
            return c
        lax.fori_loop(nu_ref[0], nb, per_block, 0)

    @pl.when(i == 0)
    def _():
        zbuf[...] = jnp.zeros_like(zbuf)
        pads(lambda cp: cp.start())

    def issue(t, c):
        for kq in range(TOP_K):
            row_copy(t, kq).start()
        return c

    def drain(t, c):
        for kq in range(TOP_K):
            row_copy(t, kq).wait()
        return c

    lax.fori_loop(0, tm, issue, 0, unroll=4)
    lax.fori_loop(0, tm, drain, 0, unroll=8)

    @pl.when(i == 0)
    def _():
        pads(lambda cp: cp.wait())


def _dispatch(dest, pad_start, pad_n, n_used, h2, n_pad, tm=256):
    m, d = h2.shape
    nt = m // tm
    dest3 = dest.reshape(nt, 1, tm * TOP_K)
    grid_spec = pltpu.PrefetchScalarGridSpec(
        num_scalar_prefetch=3,
        grid=(nt,),
        in_specs=[
            pl.BlockSpec((None, 1, tm * TOP_K), lambda i, ps, pn, nu: (i, 0, 0), memory_space=pltpu.SMEM),
            pl.BlockSpec((tm, d), lambda i, ps, pn, nu: (i, 0)),
        ],
        out_specs=pl.BlockSpec(memory_space=pl.ANY),
        scratch_shapes=[pltpu.VMEM((MOE_BLOCK, d), F32), pltpu.SemaphoreType.DMA(()),
                        pltpu.SemaphoreType.DMA(()), pltpu.SemaphoreType.DMA(())],
    )
    return pl.pallas_call(
        _dispatch_kernel,
        grid_spec=grid_spec,
        out_shape=jax.ShapeDtypeStruct((n_pad, d), F32),
        compiler_params=_params(("arbitrary",)),
        name="moe_dispatch",
    )(pad_start, pad_n, n_used, dest3, h2)


def _expert_kernel(blk_ref, be_ref, nu_ref, xs_ref, w1_ref, b1_ref, w2_ref, b2_ref, ys_ref,
                   w1b_ref, w2b_ref, *, d_e):
    del blk_ref
    i = pl.program_id(0)
    new_expert = jnp.logical_or(i == 0, be_ref[i] != be_ref[jnp.maximum(i - 1, 0)])

    @pl.when(new_expert)
    def _():
        w1b_ref[...] = w1_ref[...].astype(BF16)
        w2b_ref[...] = w2_ref[...].astype(BF16)

    @pl.when(i < nu_ref[0])
    def _():
        x = xs_ref[...].astype(BF16)
        u = _dot(x, w1b_ref[...]) + b1_ref[...]
        glu = jnp.minimum(u[:, :d_e], SWIGLU_LIMIT)
        lin = jnp.clip(u[:, d_e:], -SWIGLU_LIMIT, SWIGLU_LIMIT)
        act = glu * _sigmoid(SWIGLU_ALPHA * glu) * (lin + 1.0)
        ys_ref[...] = _dot(act.astype(BF16), w2b_ref[...]) + b2_ref[...]

    @pl.when(i >= nu_ref[0])
    def _():
        ys_ref[...] = jnp.zeros_like(ys_ref)


def _experts(blk_idx, blk_e, n_used, xs, w1, b1, w2, b2):
    n_pad, d = xs.shape
    nb = n_pad // MOE_BLOCK
    n_e, _, d2 = w1.shape
    d_e = d2 // 2
    grid_spec = pltpu.PrefetchScalarGridSpec(
        num_scalar_prefetch=3,
        grid=(nb,),
        in_specs=[
            pl.BlockSpec((MOE_BLOCK, d), lambda i, bi, be, nu: (bi[i], 0)),
            pl.BlockSpec((None, d, d2), lambda i, bi, be, nu: (be[i], 0, 0)),
            pl.BlockSpec((None, 1, d2), lambda i, bi, be, nu: (be[i], 0, 0)),
            pl.BlockSpec((None, d_e, d), lambda i, bi, be, nu: (be[i], 0, 0)),
            pl.BlockSpec((None, 1, d), lambda i, bi, be, nu: (be[i], 0, 0)),
        ],
        out_specs=pl.BlockSpec((MOE_BLOCK, d), lambda i, bi, be, nu: (i, 0)),
        scratch_shapes=[pltpu.VMEM((d, d2), BF16), pltpu.VMEM((d_e, d), BF16)],
    )
    return pl.pallas_call(
        functools.partial(_expert_kernel, d_e=d_e),
        grid_spec=grid_spec,
        out_shape=jax.ShapeDtypeStruct((n_pad, d), F32),
        compiler_params=_params(("arbitrary",)),
        name="moe_experts",
    )(blk_idx, blk_e, n_used, xs, w1, b1.reshape(n_e, 1, d2), w2, b2.reshape(n_e, 1, d))


def _combine_kernel(dest_ref, dnext_ref, ys_ref, x1_ref, gw_ref, o_ref, buf, sems):
    i = pl.program_id(0)
    nt = pl.num_programs(0)
    tm = x1_ref.shape[0]
    slot = i % 2

    def row_copy(d_ref, sl, t, kq):
        return pltpu.make_async_copy(
            ys_ref.at[pl.ds(d_ref[0, t * TOP_K + kq], 1), :], buf.at[sl, kq, pl.ds(t, 1), :], sems.at[sl])

    def issue_all(d_ref, sl):
        def body(t, c):
            for kq in range(TOP_K):
                row_copy(d_ref, sl, t, kq).start()
            return c
        lax.fori_loop(0, tm, body, 0, unroll=4)

    @pl.when(i == 0)
    def _():
        issue_all(dest_ref, 0)

    @pl.when(i + 1 < nt)
    def _():
        issue_all(dnext_ref, 1 - slot)

    def drain(t, c):
        for kq in range(TOP_K):
            row_copy(dest_ref, slot, t, kq).wait()
        return c

    lax.fori_loop(0, tm, drain, 0, unroll=8)
    gw = gw_ref[...]
    acc = x1_ref[...]
    for kq in range(TOP_K):
        acc = acc + gw[:, kq:kq + 1] * buf[slot, kq]
    o_ref[...] = acc


def _combine(dest, ys, x1, gw, tm=128):
    m, d = x1.shape
    nt = m // tm
    dest3 = dest.reshape(nt, 1, tm * TOP_K)
    return pl.pallas_call(
        _combine_kernel,
        grid=(nt,),
        in_specs=[
            pl.BlockSpec((None, 1, tm * TOP_K), lambda i: (i, 0, 0), memory_space=pltpu.SMEM),
            pl.BlockSpec((None, 1, tm * TOP_K), lambda i: (jnp.minimum(i + 1, nt - 1), 0, 0),
                         memory_space=pltpu.SMEM),
            pl.BlockSpec(memory_space=pl.ANY),
            pl.BlockSpec((tm, d), lambda i: (i, 0)),
            pl.BlockSpec((tm, LANES), lambda i: (i, 0)),
        ],
        out_specs=pl.BlockSpec((tm, d), lambda i: (i, 0)),
        out_shape=jax.ShapeDtypeStruct((m, d), F32),
        scratch_shapes=[pltpu.VMEM((2, TOP_K, tm, d), F32), pltpu.SemaphoreType.DMA((2,))],
        compiler_params=_params(("arbitrary",)),
        name="moe_combine",
    )(dest3, dest3, ys, x1, gw)


def _blockdiag2(a, b):
    za = jnp.zeros((a.shape[0], b.shape[1]), a.dtype)
    zb = jnp.zeros((b.shape[0], a.shape[1]), a.dtype)
    return jnp.concatenate([jnp.concatenate([a, za], axis=1), jnp.concatenate([zb, b], axis=1)], axis=0)


def _layer(x, g_mix, w_in, mu_prev, mu_next, w0_f, w2_f, w0_b, w2_b, a0_f, a2_f, a0_b, a2_b,
           g2, k_k, k_a, r_k, lnx_g, lnx_b, q_norm_g, k_norm_g, rpb, w_a, w_b, w_o,
           g_ffn, w_router, b_router, w1, b1, w2, b2):
    bsz, t, d_model = x.shape
    m = bsz * t
    d_a = w_a.shape[0]
    d_b = w_b.shape[0]
    a_cols = mu_prev.shape[0]
    b_cols = 3 * d_b
    row = lambda a: a.reshape(1, -1).astype(F32)

    w_in_b = w_in.astype(BF16)
    pa, pb, gates = _inproj(x.reshape(m, d_model), row(g_mix), w_in_b[:, :a_cols],
                            w_in_b[:, a_cols:a_cols + b_cols], w_in_b[:, a_cols + b_cols:])

    prep = _prep(pa.reshape(bsz, t, a_cols), row(mu_prev), row(mu_next),
                 jnp.concatenate([row(w0_f), row(w0_b)], axis=1), _blockdiag2(w2_f, w2_b),
                 jnp.concatenate([row(a0_f), row(a0_b)], axis=1), _blockdiag2(a2_f, a2_b),
                 g2, row(k_k), row(k_a), row(r_k), d_a)
    r, v, kk, lw_f, lw_b, k_f, k_b, b_f, b_b, bonus, g = prep
    o_f, o_b = _scan(r, v, kk, lw_f, lw_b, k_f, k_b, b_f, b_b)

    yb = _na(pb.reshape(bsz, t, b_cols), q_norm_g, k_norm_g, _na_bias_table(rpb), d_b)

    n_e = w_router.shape[1]
    wr_pad = jnp.zeros((d_model, LANES), F32).at[:, :n_e].set(w_router)
    br_pad = jnp.full((1, LANES), NEG_BIG, F32).at[0, :n_e].set(b_router)
    x1, h2, idx, rank, gw, cnt = _merge(
        o_f, o_b, bonus, g, yb, gates.reshape(bsz, t, 2 * d_model), x,
        row(lnx_g), row(lnx_b), w_a.astype(BF16), w_b.astype(BF16), w_o.astype(BF16),
        row(g_ffn), wr_pad, br_pad)

    counts = cnt[0, :n_e].astype(jnp.int32)
    padded = ((counts + MOE_BLOCK - 1) // MOE_BLOCK) * MOE_BLOCK
    pend = jnp.cumsum(padded)
    pstart = pend - padded
    n_assign = m * TOP_K
    n_blocks = -(-n_assign // MOE_BLOCK) + n_e
    n_pad = n_blocks * MOE_BLOCK
    top_idx = idx[:, :TOP_K]
    dest = (pstart[top_idx] + rank[:, :TOP_K]).reshape(-1).astype(jnp.int32)
    n_used = (pend[-1] // MOE_BLOCK).astype(jnp.int32)
    blk_idx = jnp.minimum(jnp.arange(n_blocks, dtype=jnp.int32), n_used - 1)
    blk_e = jnp.sum((blk_idx[:, None] * MOE_BLOCK >= pend[None, :]).astype(jnp.int32), axis=1)
    blk_e = jnp.minimum(blk_e, n_e - 1)

    xs = _dispatch(dest, (pstart + counts).astype(jnp.int32), (padded - counts).astype(jnp.int32),
                   n_used.reshape(1), h2, n_pad)
    ys = _experts(blk_idx, blk_e, n_used.reshape(1), xs, w1, b1, w2, b2)
    out = _combine(dest, ys, x1, gw)
    return out.reshape(bsz, t, d_model)


def kernel(x, g_mix, w_in, mu_prev, mu_next, w0_f, w2_f, w0_b, w2_b, a0_f, a2_f, a0_b, a2_b, g2, k_k, k_a, r_k, lnx_g, lnx_b, q_norm_g, k_norm_g, rpb, w_a, w_b, w_o, g_ffn, w_router, b_router, w1, b1, w2, b2):
    for l in range(g_mix.shape[0]):
        x = _layer(x, g_mix[l], w_in[l], mu_prev[l], mu_next[l], w0_f[l], w2_f[l], w0_b[l], w2_b[l],
                   a0_f[l], a2_f[l], a0_b[l], a2_b[l], g2[l], k_k[l], k_a[l], r_k[l], lnx_g[l], lnx_b[l],
                   q_norm_g[l], k_norm_g[l], rpb[l], w_a[l], w_b[l], w_o[l], g_ffn[l], w_router[l],
                   b_router[l], w1[l], b1[l], w2[l], b2[l])
    return x
```

```python
import functools
import math

import numpy as np
import jax
import jax.numpy as jnp
from jax import lax
from jax.experimental import pallas as pl
from jax.experimental.pallas import tpu as pltpu

F32 = jnp.float32
BF16 = jnp.bfloat16

LANES = 128
SUBLANES = 8
HEAD_DIM = 64
PAIR = 2 * HEAD_DIM
GRID_W = 64
NA_WR = 8
NA_WC = 16
W_LORA = 64
A_LORA = 64
G_LORA = 128
DECAY_SCALE = math.exp(-0.5)
GN_EPS = 64e-5
RMS_EPS = 1e-5
N_EXPERTS = 32
TOP_K = 4
MOE_BLOCK = 512
SWIGLU_LIMIT = 7.0
SWIGLU_ALPHA = 1.702
NEG_BIG = -1e30
CHUNK = 128
NA_ROWS_PER_STEP = 8
VMEM_LIMIT = 56 * 1024 * 1024


def _dot(a, b):
    return jnp.dot(a, b, preferred_element_type=F32)


def _dot_nt(a, b):
    return lax.dot_general(a, b, (((1,), (1,)), ((), ())), preferred_element_type=F32)


def _split(a):
    hi = a.astype(BF16)
    lo = (a - hi.astype(F32)).astype(BF16)
    return hi, lo


def _mm_exact_rhs(a, b_bf16):
    hi, lo = _split(a)
    return _dot(hi, b_bf16) + _dot(lo, b_bf16)


def _hilo(b):
    return jnp.stack(_split(b))


def _mm3(a, b_hilo):
    ah, al = _split(a)
    bh = b_hilo[0]
    return _dot(ah, bh) + _dot(al, bh) + _dot(ah, b_hilo[1])


def _store_rows_as_tiles(ref, val):
    n = val.shape[0]
    for s in range(SUBLANES):
        ref[pl.ds(s, n, stride=SUBLANES), :] = val[:, s * LANES:(s + 1) * LANES]


def _load_tiles_as_rows(ref, n):
    return jnp.concatenate([ref[pl.ds(s, n, stride=SUBLANES), :] for s in range(SUBLANES)], axis=1)


def _sigmoid(x):
    return 1.0 / (1.0 + jnp.exp(-x))


def _params(sem):
    return pltpu.CompilerParams(dimension_semantics=sem, vmem_limit_bytes=VMEM_LIMIT)


def _block_ones(n, blk):
    i = np.arange(n) // blk
    return jnp.asarray(i[:, None] == i[None, :], BF16)


def _inproj_kernel(x_ref, g_ref, wa_ref, wb_ref, wg_ref, pa_ref, pb_ref, pg_ref):
    x = x_ref[...]
    ms = jnp.mean(x * x, axis=-1, keepdims=True)
    h = (x * lax.rsqrt(ms + RMS_EPS) * g_ref[...]).astype(BF16)
    pa_ref[...] = _dot(h, wa_ref[...])
    pb_ref[...] = _dot(h, wb_ref[...])
    pg_ref[...] = _dot(h, wg_ref[...])


def _inproj(x2, g_mix, w_a, w_b, w_g, tm=256):
    m, d = x2.shape
    na, nb, ng = w_a.shape[1], w_b.shape[1], w_g.shape[1]
    full = lambda i: (0, 0)
    return pl.pallas_call(
        _inproj_kernel,
        grid=(m // tm,),
        in_specs=[
            pl.BlockSpec((tm, d), lambda i: (i, 0)),
            pl.BlockSpec((1, d), full),
            pl.BlockSpec((d, na), full),
            pl.BlockSpec((d, nb), full),
            pl.BlockSpec((d, ng), full),
        ],
        out_specs=[
            pl.BlockSpec((tm, na), lambda i: (i, 0)),
            pl.BlockSpec((tm, nb), lambda i: (i, 0)),
            pl.BlockSpec((tm, ng), lambda i: (i, 0)),
        ],
        out_shape=[
            jax.ShapeDtypeStruct((m, na), F32),
            jax.ShapeDtypeStruct((m, nb), F32),
            jax.ShapeDtypeStruct((m, ng), F32),
        ],
        compiler_params=_params(("arbitrary",)),
        name="inproj",
    )(x2, g_mix, w_a, w_b, w_g)


def _prep_kernel(p_ref, prev_ref, next_ref, mup_ref, mun_ref, w0_ref, w2_ref, a0_ref, a2_ref,
                 g2_ref, kk_ref, ka_ref, rk_ref, ones_ref,
                 r_o, v_o, kk_o, lwf_o, lwb_o, kf_o, kb_o, bf_o, bb_o, bonus_o, g_o, *, d_a):
    i = pl.program_id(1)
    n_t = pl.num_programs(1)
    p = p_ref[...]
    tt = p.shape[0]
    prow = jnp.where(i > 0, prev_ref[7:8, :], 0.0)
    nrow = jnp.where(i < n_t - 1, next_ref[0:1, :], 0.0)
    rid = lax.broadcasted_iota(jnp.int32, (tt, 1), 0)
    prev = jnp.where(rid == 0, prow, pltpu.roll(p, 1, axis=0))
    nxt = jnp.where(rid == tt - 1, nrow, pltpu.roll(p, tt - 1, axis=0))
    xa = p + mup_ref[...] * (prev - p) + mun_ref[...] * (nxt - p)

    r = xa[:, 0:d_a]
    k = xa[:, d_a:2 * d_a]
    v = xa[:, 2 * d_a:3 * d_a]
    o = 3 * d_a
    lw = xa[:, o:o + 2 * W_LORA]
    la = xa[:, o + 2 * W_LORA:o + 2 * W_LORA + 2 * A_LORA]
    lg = xa[:, o + 2 * W_LORA + 2 * A_LORA:]

    dpre = w0_ref[...] + _mm3(jnp.tanh(lw), w2_ref)
    apre = a0_ref[...] + _mm3(la, a2_ref)
    g = _mm3(_sigmoid(lg), g2_ref)
    logw = -DECAY_SCALE * _sigmoid(dpre)
    a = _sigmoid(apre)

    ones = ones_ref[...]
    kkr = k * kk_ref[...]
    ss = _mm_exact_rhs(kkr * kkr, ones)
    kk = kkr / jnp.maximum(jnp.sqrt(ss), 1e-12)

    ka = ka_ref[...]
    k_f = k * (1.0 + (a[:, :d_a] - 1.0) * ka)
    k_b = k * (1.0 + (a[:, d_a:] - 1.0) * ka)
    b_f = kk * a[:, :d_a]
    b_b = kk * a[:, d_a:]
    rk = rk_ref[...]
    bon = _mm_exact_rhs(r * (k_f + k_b) * rk, ones) * v

    bonus_o[...] = bon
    g_o[...] = g
    for pi in range(d_a // PAIR):
        sl = slice(pi * PAIR, (pi + 1) * PAIR)
        r_o[pi] = r[:, sl]
        v_o[pi] = v[:, sl]
        kk_o[pi] = kk[:, sl]
        lwf_o[pi] = logw[:, sl]
        lwb_o[pi] = logw[:, d_a + pi * PAIR:d_a + (pi + 1) * PAIR]
        kf_o[pi] = k_f[:, sl]
        kb_o[pi] = k_b[:, sl]
        bf_o[pi] = b_f[:, sl]
        bb_o[pi] = b_b[:, sl]


def _prep(pa3, mu_prev, mu_next, w0c, w2blk, a0c, a2blk, g2, k_k, k_a, r_k, d_a, tt=256):
    b, t, ac = pa3.shape
    n_t = t // tt
    n_p = d_a // PAIR
    ones = _block_ones(d_a, HEAD_DIM)
    c2 = lambda bi, i: (0, 0)
    c3 = lambda bi, i: (0, 0, 0)
    pair_spec = pl.BlockSpec((None, n_p, tt, PAIR), lambda bi, i: (bi, 0, i, 0))
    pair_shape = jax.ShapeDtypeStruct((b, n_p, t, PAIR), F32)
    flat_spec = pl.BlockSpec((None, tt, d_a), lambda bi, i: (bi, i, 0))
    flat_shape = jax.ShapeDtypeStruct((b, t, d_a), F32)
    r8 = tt // 8
    return pl.pallas_call(
        functools.partial(_prep_kernel, d_a=d_a),
        grid=(b, n_t),
        in_specs=[
            pl.BlockSpec((None, tt, ac), lambda bi, i: (bi, i, 0)),
            pl.BlockSpec((None, 8, ac), lambda bi, i: (bi, jnp.maximum(i * r8 - 1, 0), 0)),
            pl.BlockSpec((None, 8, ac), lambda bi, i: (bi, jnp.minimum((i + 1) * r8, t // 8 - 1), 0)),
            pl.BlockSpec((1, ac), c2),
            pl.BlockSpec((1, ac), c2),
            pl.BlockSpec((1, 2 * d_a), c2),
            pl.BlockSpec((2, 2 * W_LORA, 2 * d_a), c3),
            pl.BlockSpec((1, 2 * d_a), c2),
            pl.BlockSpec((2, 2 * A_LORA, 2 * d_a), c3),
            pl.BlockSpec((2, G_LORA, d_a), c3),
            pl.BlockSpec((1, d_a), c2),
            pl.BlockSpec((1, d_a), c2),
            pl.BlockSpec((1, d_a), c2),
            pl.BlockSpec((d_a, d_a), c2),
        ],
        out_specs=[pair_spec] * 9 + [flat_spec] * 2,
        out_shape=[pair_shape] * 9 + [flat_shape] * 2,
        compiler_params=_params(("arbitrary", "arbitrary")),
        name="rwkv_prep",
    )(pa3, pa3, pa3, mu_prev, mu_next, w0c, _hilo(w2blk), a0c, _hilo(a2blk), _hilo(g2), k_k, k_a, r_k, ones)


def _mm_exact_rhs_left(tri_bf16, x):
    hi, lo = _split(x)
    return _dot(tri_bf16, hi) + _dot(tri_bf16, lo)


def _scan_stage(items, s_refs):
    c = items[0][0].shape[0]
    ri = lax.broadcasted_iota(jnp.int32, (c, c), 0)
    ci = lax.broadcasted_iota(jnp.int32, (c, c), 1)
    lane = lax.broadcasted_iota(jnp.int32, (1, PAIR), 1)
    m0 = (lane < HEAD_DIM).astype(F32)
    m1 = 1.0 - m0
    eye = (ri == ci).astype(F32)
    hi_ = lax.broadcasted_iota(jnp.int32, (PAIR, PAIR), 0) // HEAD_DIM
    hj_ = lax.broadcasted_iota(jnp.int32, (PAIR, PAIR), 1) // HEAD_DIM
    same_head = hi_ == hj_
    zero = jnp.zeros((), F32)
    n = len(items)

    def masks(reverse):
        if reverse:
            return ci >= ri, ci > ri, 0
        return ci <= ri, ci < ri, c - 1

    cums = [_mm_exact_rhs_left(masks(it[6])[0].astype(BF16), it[3]) for it in items]

    pre = []
    for (r, kk, v, lw, k, b, reverse), cum in zip(items, cums):
        end = masks(reverse)[2]
        cmid = cum[c // 2:c // 2 + 1, :]
        cend = cum[end:end + 1, :]
        r_abs = r * jnp.exp(cum)
        a_abs = -kk * jnp.exp(cum - lw)
        to_mid = jnp.exp(-cmid)
        from_mid = jnp.exp(cmid - cum)
        to_end = jnp.exp(cend - cum)
        pre.append(dict(
            r_abs=r_abs, a_abs=a_abs, r_rel=r_abs * to_mid, a_rel=a_abs * to_mid,
            k_rel=k * from_mid, b_rel=b * from_mid, k_end=k * to_end, b_end=b * to_end,
            d_tot=jnp.exp(cend), v=v))

    grams = []
    for p in pre:
        lhs = jnp.concatenate([p["r_rel"] * m0, p["r_rel"] * m1, p["a_rel"] * m0, p["a_rel"] * m1],
                              axis=0).astype(BF16)
        rhs = jnp.concatenate([p["k_rel"], p["b_rel"]], axis=0).astype(BF16)
        grams.append(_dot_nt(lhs, rhs))

    pws, ts = [], []
    for it, gram in zip(items, grams):
        strict = masks(it[6])[1]
        for e in range(2):
            a_ab = jnp.where(strict, gram[(2 + e) * c:(3 + e) * c, c:2 * c], zero)
            pws.append(a_ab)
            ts.append(eye + a_ab)
    for _ in range(int(round(math.log2(c))) - 1):
        pws = [_dot(pw.astype(BF16), pw.astype(BF16)) for pw in pws]
        ts = [t + _dot(t.astype(BF16), pw.astype(BF16)) for t, pw in zip(ts, pws)]

    v_blks = [jnp.concatenate([p["v"] * m0, p["v"] * m1], axis=0).astype(BF16) for p in pre]
    akvs, o_rks, rb_cats = [], [], []
    for it, gram, v_blk in zip(items, grams, v_blks):
        incl, strict, _ = masks(it[6])
        ak_cat = jnp.concatenate(
            [jnp.where(strict, gram[(2 + e) * c:(3 + e) * c, 0:c], zero) for e in range(2)], axis=1)
        rk_cat = jnp.concatenate(
            [jnp.where(incl, gram[e * c:(e + 1) * c, 0:c], zero) for e in range(2)], axis=1)
        rb_cats.append(jnp.concatenate(
            [jnp.where(incl, gram[e * c:(e + 1) * c, c:2 * c], zero) for e in range(2)], axis=1).astype(BF16))
        akvs.append(_dot(ak_cat.astype(BF16), v_blk))
        o_rks.append(_dot(rk_cat.astype(BF16), v_blk))

    xs = []
    for i, (p, akv) in enumerate(zip(pre, akvs)):
        t_cat = jnp.concatenate([ts[2 * i], ts[2 * i + 1]], axis=1).astype(BF16)
        y_blk = jnp.concatenate([
            jnp.concatenate([p["a_abs"] * m0, akv * m0], axis=1),
            jnp.concatenate([p["a_abs"] * m1, akv * m1], axis=1)], axis=0).astype(BF16)
        xs.append(_dot(t_cat, y_blk))

    s0s = [s_ref[...] for s_ref in s_refs]
    s0bs = [s0.astype(BF16) for s0 in s0s]
    us = [_dot(x[:, 0:PAIR].astype(BF16), s0b) + x[:, PAIR:2 * PAIR] for x, s0b in zip(xs, s0bs)]
    outs = []
    for i in range(n):
        p, u = pre[i], us[i]
        u_blk = jnp.concatenate([u * m0, u * m1], axis=0).astype(BF16)
        outs.append(_dot(p["r_abs"].astype(BF16), s0bs[i]) + _dot(rb_cats[i], u_blk) + o_rks[i])
        kb_t = jnp.concatenate([p["b_end"].T, p["k_end"].T], axis=1).astype(BF16)
        uv = jnp.concatenate([u, p["v"]], axis=0).astype(BF16)
        d_col = jnp.broadcast_to(p["d_tot"], (PAIR, PAIR)).T
        s_refs[i][...] = jnp.where(same_head, d_col * s0s[i] + _dot(kb_t, uv), zero)
    return outs


def _scan_kernel(rf, vf, kkf, lwf, kf, bf, rb, vb, kkb, lwb, kb, bb, of_ref, ob_ref, s_ref):
    @pl.when(pl.program_id(1) == 0)
    def _():
        s_ref[...] = jnp.zeros_like(s_ref)

    n_p = rf.shape[0]
    items, s_refs = [], []
    for p in range(n_p):
        items.append((rf[p], kkf[p], vf[p], lwf[p], kf[p], bf[p], False))
        s_refs.append(s_ref.at[2 * p])
        items.append((rb[p], kkb[p], vb[p], lwb[p], kb[p], bb[p], True))
        s_refs.append(s_ref.at[2 * p + 1])
    outs = _scan_stage(items, s_refs)
    for p in range(n_p):
        of_ref[p] = outs[2 * p]
        ob_ref[p] = outs[2 * p + 1]


def _scan(r, v, kk, lw_f, lw_b, k_f, k_b, b_f, b_b):
    bsz, n_p, t, _ = r.shape
    nc = t // CHUNK
    fwd = pl.BlockSpec((None, n_p, CHUNK, PAIR), lambda bi, c: (bi, 0, c, 0))
    bwd = pl.BlockSpec((None, n_p, CHUNK, PAIR), lambda bi, c: (bi, 0, nc - 1 - c, 0))
    shape = jax.ShapeDtypeStruct((bsz, n_p, t, PAIR), F32)
    return pl.pallas_call(
        _scan_kernel,
        grid=(bsz, nc),
        in_specs=[fwd] * 6 + [bwd] * 6,
        out_specs=[fwd, bwd],
        out_shape=[shape, shape],
        scratch_shapes=[pltpu.VMEM((2 * n_p, PAIR, PAIR), F32)],
        compiler_params=_params(("arbitrary", "arbitrary")),
        name="rwkv_scan",
    )(r, v, kk, lw_f, k_f, b_f, r, v, kk, lw_b, k_b, b_b)


def _na_kernel(q_ref, k_ref, v_ref, gq_ref, gk_ref, tab_ref, ones_ref, o_ref, qn_s, kn_s, vb_s, *, rows):
    ones = ones_ref[...]
    scale = HEAD_DIM ** -0.5
    q = q_ref[...]
    k = k_ref[...]
    inv_d = 1.0 / HEAD_DIM
    qn = q * lax.rsqrt(_mm_exact_rhs(q * q, ones) * inv_d + RMS_EPS) * (gq_ref[...] * scale)
    kn = k * lax.rsqrt(_mm_exact_rhs(k * k, ones) * inv_d + RMS_EPS) * gk_ref[...]
    qn_s[...] = qn.astype(BF16)
    kn_s[...] = kn.astype(BF16)
    vb_s[...] = v_ref[...].astype(BF16)
    lane = lax.broadcasted_iota(jnp.int32, (1, PAIR), 1)
    head0 = lane < HEAD_DIM
    win = NA_WR * GRID_W

    def row_group(gi, carry):
        rws = [gi * NA_ROWS_PER_STEP + j for j in range(NA_ROWS_PER_STEP)]
        rss = [jnp.clip(r - NA_WR // 2, 0, rows - NA_WR) for r in rws]
        q_rows = [qn_s[pl.ds(pl.multiple_of(r * GRID_W, GRID_W), GRID_W), :] for r in rws]
        k_wins = [kn_s[pl.ds(pl.multiple_of(rs * GRID_W, GRID_W), win), :] for rs in rss]
        v_wins = [vb_s[pl.ds(pl.multiple_of(rs * GRID_W, GRID_W), win), :] for rs in rss]
        ss = []
        for j in range(NA_ROWS_PER_STEP):
            d0 = rss[j] - rws[j] + NA_WR - 1
            for e in range(2):
                mask = head0 if e == 0 else jnp.logical_not(head0)
                qm = jnp.where(mask, q_rows[j], jnp.zeros_like(q_rows[j]))
                bias = jnp.concatenate(
                    [tab_ref[e, pl.ds(d0 + 2 * m, 1)][0] for m in range(NA_WR // 2)], axis=1)
                ss.append(_dot_nt(qm, k_wins[j]) + bias)
        mxs = [jnp.max(s, axis=-1, keepdims=True) for s in ss]
        ps = [jnp.exp(s - mx) for s, mx in zip(ss, mxs)]
        ls = [jnp.sum(p, axis=-1, keepdims=True) for p in ps]
        pvs = [_dot(p.astype(BF16), v_wins[i // 2]) for i, p in enumerate(ps)]
        for j in range(NA_ROWS_PER_STEP):
            o0 = pvs[2 * j] / ls[2 * j]
            o1 = pvs[2 * j + 1] / ls[2 * j + 1]
            o_ref[pl.ds(pl.multiple_of(rws[j] * GRID_W, GRID_W), GRID_W), :] = jnp.where(head0, o0, o1)
        return carry

    lax.fori_loop(0, rows // NA_ROWS_PER_STEP, row_group, 0)


def _na_bias_table(rpb):
    qc = np.arange(GRID_W)
    kc = np.arange(GRID_W)
    cs = np.clip(qc - NA_WC // 2, 0, GRID_W - NA_WC)
    valid = (kc[None, :] >= cs[:, None]) & (kc[None, :] < cs[:, None] + NA_WC)
    dc = np.clip(kc[None, :] - qc[:, None] + NA_WC - 1, 0, 2 * NA_WC - 2)
    b = rpb.astype(F32)[:, :, dc]
    b = jnp.where(jnp.asarray(valid)[None, None], b, NEG_BIG)
    return jnp.concatenate([b[:, :-1], b[:, 1:]], axis=-1)


def _na(pb3, q_gain, k_gain, table, d_b):
    bsz, t, _ = pb3.shape
    rows = t // GRID_W
    n_p = d_b // PAIR
    ones = _block_ones(PAIR, HEAD_DIM)
    gq = jnp.tile(q_gain.reshape(1, HEAD_DIM), (1, 2))
    gk = jnp.tile(k_gain.reshape(1, HEAD_DIM), (1, 2))
    n_d = table.shape[1]
    c2 = lambda bi, p: (0, 0)
    return pl.pallas_call(
        functools.partial(_na_kernel, rows=rows),
        grid=(bsz, n_p),
        in_specs=[
            pl.BlockSpec((None, t, PAIR), lambda bi, p: (bi, 0, p)),
            pl.BlockSpec((None, t, PAIR), lambda bi, p: (bi, 0, n_p + p)),
            pl.BlockSpec((None, t, PAIR), lambda bi, p: (bi, 0, 2 * n_p + p)),
            pl.BlockSpec((1, PAIR), c2),
            pl.BlockSpec((1, PAIR), c2),
            pl.BlockSpec((2, n_d, GRID_W, PAIR), lambda bi, p: (p, 0, 0, 0)),
            pl.BlockSpec((PAIR, PAIR), c2),
        ],
        out_specs=pl.BlockSpec((None, t, PAIR), lambda bi, p: (bi, 0, p)),
        out_shape=jax.ShapeDtypeStruct((bsz, t, d_b), F32),
        scratch_shapes=[pltpu.VMEM((t, PAIR), BF16)] * 3,
        compiler_params=_params(("arbitrary", "arbitrary")),
        name="natten",
    )(pb3, pb3, pb3, gq, gk, table, ones)


def _merge_kernel(of_ref, ob_ref, bonus_ref, g_ref, yb_ref, gates_ref, x_ref,
                  lng_ref, lnb_ref, wa_ref, wb_ref, wo_ref, gffn_ref, wr_ref, br_ref,
                  ones_ref, tri_ref,
                  x1_ref, h2_ref, idx_ref, rank_ref, gw_ref, cnt_ref, carry_ref, *, d_model):
    first = jnp.logical_and(pl.program_id(0) == 0, pl.program_id(1) == 0)

    @pl.when(first)
    def _():
        carry_ref[...] = jnp.zeros_like(carry_ref)

    n_p = of_ref.shape[0]
    o = jnp.concatenate([of_ref[p] + ob_ref[p] for p in range(n_p)], axis=1)
    ones = ones_ref[...]
    inv_d = 1.0 / HEAD_DIM
    mu = _mm_exact_rhs(o, ones) * inv_d
    dv = o - mu
    var = _mm_exact_rhs(dv * dv, ones) * inv_d
    y = dv * lax.rsqrt(var + GN_EPS) * lng_ref[...] + lnb_ref[...] + bonus_ref[...]
    ya = y * g_ref[...]

    gates = gates_ref[...]
    pa = _dot(ya.astype(BF16), wa_ref[...])
    pb = _dot(yb_ref[...].astype(BF16), wb_ref[...])
    merged = _sigmoid(gates[:, :d_model]) * pa + _sigmoid(gates[:, d_model:]) * pb
    x1 = x_ref[...] + _dot(merged.astype(BF16), wo_ref[...])
    x1_ref[...] = x1
    ms = jnp.mean(x1 * x1, axis=-1, keepdims=True)
    h2 = x1 * lax.rsqrt(ms + RMS_EPS) * gffn_ref[...]
    _store_rows_as_tiles(h2_ref, h2)

    logits = _mm3(h2, wr_ref) + br_ref[...]
    tm = logits.shape[0]
    lane = lax.broadcasted_iota(jnp.int32, (tm, LANES), 1)
    work = logits
    vals, idxs = [], []
    for _ in range(TOP_K):
        m = jnp.max(work, axis=-1, keepdims=True)
        ix = jnp.min(jnp.where(work == m, lane, LANES), axis=-1, keepdims=True)
        vals.append(m)
        idxs.append(ix)
        work = jnp.where(lane == ix, -jnp.inf, work)
    es = [jnp.exp(vk - vals[0]) for vk in vals]
    den = es[0] + es[1] + es[2] + es[3]
    member = jnp.zeros((tm, LANES), F32)
    for ix in idxs:
        member = member + (lane == ix).astype(F32)
    before = _dot(tri_ref[...], member.astype(BF16)) + carry_ref[...]
    idx_out = jnp.zeros((tm, LANES), jnp.int32)
    rank_out = jnp.zeros((tm, LANES), jnp.int32)
    gw_out = jnp.zeros((tm, LANES), F32)
    for kq in range(TOP_K):
        rk = jnp.sum(jnp.where(lane == idxs[kq], before, 0.0), axis=-1, keepdims=True)
        sel = lane == kq
        idx_out = jnp.where(sel, idxs[kq], idx_out)
        rank_out = jnp.where(sel, rk.astype(jnp.int32), rank_out)
        gw_out = jnp.where(sel, es[kq] / den, gw_out)
    idx_ref[...] = idx_out
    rank_ref[...] = rank_out
    gw_ref[...] = gw_out
    carry_ref[...] = carry_ref[...] + jnp.sum(member, axis=0, keepdims=True)
    cnt_ref[...] = carry_ref[...]


def _merge(o_f, o_b, bonus, g, yb, gates3, x3, lnx_g, lnx_b, w_a, w_b, w_o, g_ffn, wr_pad, br_pad, tm=512):
    bsz, n_p, t, _ = o_f.shape
    d_a = n_p * PAIR
    d_b = yb.shape[-1]
    d_model = x3.shape[-1]
    n_t = t // tm
    m = bsz * t
    ones = _block_ones(d_a, HEAD_DIM)
    tri = jnp.asarray(np.tril(np.ones((tm, tm)), -1), BF16)
    c2 = lambda bi, i: (0, 0)
    tok = lambda w: pl.BlockSpec((None, tm, w), lambda bi, i: (bi, i, 0))
    flat = lambda w: pl.BlockSpec((tm, w), lambda bi, i: (bi * n_t + i, 0))
    pair = pl.BlockSpec((None, n_p, tm, PAIR), lambda bi, i: (bi, 0, i, 0))
    return pl.pallas_call(
        functools.partial(_merge_kernel, d_model=d_model),
        grid=(bsz, n_t),
        in_specs=[
            pair, pair, tok(d_a), tok(d_a), tok(d_b), tok(2 * d_model), tok(d_model),
            pl.BlockSpec((1, d_a), c2), pl.BlockSpec((1, d_a), c2),
            pl.BlockSpec((d_a, d_model), c2), pl.BlockSpec((d_b, d_model), c2),
            pl.BlockSpec((d_model, d_model), c2), pl.BlockSpec((1, d_model), c2),
            pl.BlockSpec((2, d_model, LANES), lambda bi, i: (0, 0, 0)), pl.BlockSpec((1, LANES), c2),
            pl.BlockSpec((d_a, d_a), c2), pl.BlockSpec((tm, tm), c2),
        ],
        out_specs=[flat(d_model),
                   pl.BlockSpec((tm * SUBLANES, LANES), lambda bi, i: (bi * n_t + i, 0)),
                   flat(LANES), flat(LANES), flat(LANES),
                   pl.BlockSpec((1, LANES), c2)],
        out_shape=[
            jax.ShapeDtypeStruct((m, d_model), F32),
            jax.ShapeDtypeStruct((m * SUBLANES, LANES), F32),
            jax.ShapeDtypeStruct((m, LANES), jnp.int32),
            jax.ShapeDtypeStruct((m, LANES), jnp.int32),
            jax.ShapeDtypeStruct((m, LANES), F32),
            jax.ShapeDtypeStruct((1, LANES), F32),
        ],
        scratch_shapes=[pltpu.VMEM((1, LANES), F32)],
        compiler_params=_params(("arbitrary", "arbitrary")),
        name="merge_router",
    )(o_f, o_b, bonus, g, yb, gates3, x3, lnx_g, lnx_b, w_a, w_b, w_o, g_ffn, _hilo(wr_pad), br_pad, ones, tri)


def _dispatch_kernel(ps_ref, pn_ref, nu_ref, dest_ref, h_ref, xs_ref, zbuf, sem, zsem, csem, bsem):
    i = pl.program_id(0)
    tm = h_ref.shape[0]
    n_e = ps_ref.shape[0]
    nb = xs_ref.shape[0] // MOE_BLOCK

    def row_copy(t, kq):
        return pltpu.make_async_copy(
            h_ref.at[t], xs_ref.at[dest_ref[0, t * TOP_K + kq]], sem)

    def pad_copy(r):
        return pltpu.make_async_copy(zbuf.at[0], xs_ref.at[r], zsem)

    def oct_copy(o):
        return pltpu.make_async_copy(
            zbuf.at[pl.ds(0, SUBLANES)], xs_ref.at[pl.ds(o * SUBLANES, SUBLANES)], csem)

    def blk_copy(b):
        return pltpu.make_async_copy(zbuf, xs_ref.at[pl.ds(b * MOE_BLOCK, MOE_BLOCK)], bsem)

    def pads(fn):
        def per_expert(e, c):
            start = ps_ref[e]
            end = start + pn_ref[e]
            first_oct = (start + SUBLANES - 1) // SUBLANES

            def single(r, c2):
                fn(pad_copy(r))
                return c2

            def octet(o, c2):
                fn(oct_copy(o))
                return c2
            c = lax.fori_loop(start, jnp.minimum(first_oct * SUBLANES, end), single, c)
            return lax.fori_loop(first_oct, end // SUBLANES, octet, c)
        lax.fori_loop(0, n_e, per_expert, 0)

        def per_block(b, c):
            fn(blk_copy(b))
            return c
        lax.fori_loop(nu_ref[0], nb, per_block, 0)

    @pl.when(i == 0)
    def _():
        zbuf[...] = jnp.zeros_like(zbuf)
        pads(lambda cp: cp.start())

    def issue(t, c):
        for kq in range(TOP_K):
            row_copy(t, kq).start()
        return c

    def drain(t, c):
        for kq in range(TOP_K):
            row_copy(t, kq).wait()
        return c

    lax.fori_loop(0, tm, issue, 0, unroll=8)
    lax.fori_loop(0, tm, drain, 0, unroll=8)

    @pl.when(i == 0)
    def _():
        pads(lambda cp: cp.wait())


def _dispatch(dest, pad_start, pad_n, n_used, h2, n_pad, tm=256):
    m, ds, dl = h2.shape
    nt = m // tm
    dest3 = dest.reshape(nt, 1, tm * TOP_K)
    grid_spec = pltpu.PrefetchScalarGridSpec(
        num_scalar_prefetch=3,
        grid=(nt,),
        in_specs=[
            pl.BlockSpec((None, 1, tm * TOP_K), lambda i, ps, pn, nu: (i, 0, 0), memory_space=pltpu.SMEM),
            pl.BlockSpec((tm, ds, dl), lambda i, ps, pn, nu: (i, 0, 0)),
        ],
        out_specs=pl.BlockSpec(memory_space=pl.ANY),
        scratch_shapes=[pltpu.VMEM((MOE_BLOCK, ds, dl), F32)] + [pltpu.SemaphoreType.DMA(())] * 4,
    )
    return pl.pallas_call(
        _dispatch_kernel,
        grid_spec=grid_spec,
        out_shape=jax.ShapeDtypeStruct((n_pad, ds, dl), F32),
        compiler_params=_params(("arbitrary",)),
        name="moe_dispatch",
    )(pad_start, pad_n, n_used, dest3, h2)


def _expert_kernel(blk_ref, be_ref, nu_ref, xs_ref, w1_ref, b1_ref, w2_ref, b2_ref, ys_ref,
                   w1b_ref, w2b_ref, *, d_e):
    del blk_ref
    i = pl.program_id(0)
    new_expert = jnp.logical_or(i == 0, be_ref[i] != be_ref[jnp.maximum(i - 1, 0)])

    @pl.when(new_expert)
    def _():
        w1b_ref[...] = w1_ref[...].astype(BF16)
        w2b_ref[...] = w2_ref[...].astype(BF16)

    @pl.when(i < nu_ref[0])
    def _():
        x = _load_tiles_as_rows(xs_ref, MOE_BLOCK).astype(BF16)
        u = _dot(x, w1b_ref[...]) + b1_ref[...]
        glu = jnp.minimum(u[:, :d_e], SWIGLU_LIMIT)
        lin = jnp.clip(u[:, d_e:], -SWIGLU_LIMIT, SWIGLU_LIMIT)
        act = glu * _sigmoid(SWIGLU_ALPHA * glu) * (lin + 1.0)
        _store_rows_as_tiles(ys_ref, _dot(act.astype(BF16), w2b_ref[...]) + b2_ref[...])

    @pl.when(i >= nu_ref[0])
    def _():
        ys_ref[...] = jnp.zeros_like(ys_ref)


def _experts(blk_idx, blk_e, n_used, xs, w1, b1, w2, b2):
    n_pad = xs.shape[0] // SUBLANES
    d = SUBLANES * LANES
    nb = n_pad // MOE_BLOCK
    n_e, _, d2 = w1.shape
    d_e = d2 // 2
    grid_spec = pltpu.PrefetchScalarGridSpec(
        num_scalar_prefetch=3,
        grid=(nb,),
        in_specs=[
            pl.BlockSpec((MOE_BLOCK * SUBLANES, LANES), lambda i, bi, be, nu: (bi[i], 0)),
            pl.BlockSpec((None, d, d2), lambda i, bi, be, nu: (be[i], 0, 0)),
            pl.BlockSpec((None, 1, d2), lambda i, bi, be, nu: (be[i], 0, 0)),
            pl.BlockSpec((None, d_e, d), lambda i, bi, be, nu: (be[i], 0, 0)),
            pl.BlockSpec((None, 1, d), lambda i, bi, be, nu: (be[i], 0, 0)),
        ],
        out_specs=pl.BlockSpec((MOE_BLOCK * SUBLANES, LANES), lambda i, bi, be, nu: (i, 0)),
        scratch_shapes=[pltpu.VMEM((d, d2), BF16), pltpu.VMEM((d_e, d), BF16)],
    )
    return pl.pallas_call(
        functools.partial(_expert_kernel, d_e=d_e),
        grid_spec=grid_spec,
        out_shape=jax.ShapeDtypeStruct((n_pad * SUBLANES, LANES), F32),
        compiler_params=_params(("arbitrary",)),
        name="moe_experts",
    )(blk_idx, blk_e, n_used, xs, w1, b1.reshape(n_e, 1, d2), w2, b2.reshape(n_e, 1, d))


def _combine_kernel(dest_ref, dnext_ref, ys_ref, x1_ref, gw_ref, o_ref, buf, sems):
    i = pl.program_id(0)
    nt = pl.num_programs(0)
    tm = x1_ref.shape[0]
    slot = i % 2

    def row_copy(d_ref, sl, t, kq):
        return pltpu.make_async_copy(
            ys_ref.at[d_ref[0, t * TOP_K + kq]],
            buf.at[sl, kq, pl.ds(pl.multiple_of(t * SUBLANES, SUBLANES), SUBLANES)], sems.at[sl])

    def issue_all(d_ref, sl):
        def body(t, c):
            for kq in range(TOP_K):
                row_copy(d_ref, sl, t, kq).start()
            return c
        lax.fori_loop(0, tm, body, 0, unroll=8)

    @pl.when(i == 0)
    def _():
        issue_all(dest_ref, 0)

    @pl.when(i + 1 < nt)
    def _():
        issue_all(dnext_ref, 1 - slot)

    def drain(t, c):
        for kq in range(TOP_K):
            row_copy(dest_ref, slot, t, kq).wait()
        return c

    lax.fori_loop(0, tm, drain, 0, unroll=8)
    gw = gw_ref[...]
    for s in range(SUBLANES):
        acc = x1_ref[:, s * LANES:(s + 1) * LANES]
        for kq in range(TOP_K):
            acc = acc + gw[:, kq:kq + 1] * buf[slot, kq, pl.ds(s, tm, stride=SUBLANES), :]
        o_ref[:, s * LANES:(s + 1) * LANES] = acc


def _combine(dest, ys, x1, gw, tm=128):
    m, d = x1.shape
    nt = m // tm
    dest3 = dest.reshape(nt, 1, tm * TOP_K)
    return pl.pallas_call(
        _combine_kernel,
        grid=(nt,),
        in_specs=[
            pl.BlockSpec((None, 1, tm * TOP_K), lambda i: (i, 0, 0), memory_space=pltpu.SMEM),
            pl.BlockSpec((None, 1, tm * TOP_K), lambda i: (jnp.minimum(i + 1, nt - 1), 0, 0),
                         memory_space=pltpu.SMEM),
            pl.BlockSpec(memory_space=pl.ANY),
            pl.BlockSpec((tm, d), lambda i: (i, 0)),
            pl.BlockSpec((tm, LANES), lambda i: (i, 0)),
        ],
        out_specs=pl.BlockSpec((tm, d), lambda i: (i, 0)),
        out_shape=jax.ShapeDtypeStruct((m, d), F32),
        scratch_shapes=[pltpu.VMEM((2, TOP_K, tm * SUBLANES, LANES), F32), pltpu.SemaphoreType.DMA((2,))],
        compiler_params=_params(("arbitrary",)),
        name="moe_combine",
    )(dest3, dest3, ys, x1, gw)


def _blockdiag2(a, b):
    za = jnp.zeros((a.shape[0], b.shape[1]), a.dtype)
    zb = jnp.zeros((b.shape[0], a.shape[1]), a.dtype)
    return jnp.concatenate([jnp.concatenate([a, za], axis=1), jnp.concatenate([zb, b], axis=1)], axis=0)


def _layer(x, g_mix, w_in, mu_prev, mu_next, w0_f, w2_f, w0_b, w2_b, a0_f, a2_f, a0_b, a2_b,
           g2, k_k, k_a, r_k, lnx_g, lnx_b, q_norm_g, k_norm_g, rpb, w_a, w_b, w_o,
           g_ffn, w_router, b_router, w1, b1, w2, b2):
    bsz, t, d_model = x.shape
    m = bsz * t
    d_a = w_a.shape[0]
    d_b = w_b.shape[0]
    a_cols = mu_prev.shape[0]
    b_cols = 3 * d_b
    row = lambda a: a.reshape(1, -1).astype(F32)

    w_in_b = w_in.astype(BF16)
    pa, pb, gates = _inproj(x.reshape(m, d_model), row(g_mix), w_in_b[:, :a_cols],
                            w_in_b[:, a_cols:a_cols + b_cols], w_in_b[:, a_cols + b_cols:])

    prep = _prep(pa.reshape(bsz, t, a_cols), row(mu_prev), row(mu_next),
                 jnp.concatenate([row(w0_f), row(w0_b)], axis=1), _blockdiag2(w2_f, w2_b),
                 jnp.concatenate([row(a0_f), row(a0_b)], axis=1), _blockdiag2(a2_f, a2_b),
                 g2, row(k_k), row(k_a), row(r_k), d_a)
    r, v, kk, lw_f, lw_b, k_f, k_b, b_f, b_b, bonus, g = prep
    o_f, o_b = _scan(r, v, kk, lw_f, lw_b, k_f, k_b, b_f, b_b)

    yb = _na(pb.reshape(bsz, t, b_cols), q_norm_g, k_norm_g, _na_bias_table(rpb), d_b)

    n_e = w_router.shape[1]
    wr_pad = jnp.zeros((d_model, LANES), F32).at[:, :n_e].set(w_router)
    br_pad = jnp.full((1, LANES), NEG_BIG, F32).at[0, :n_e].set(b_router)
    x1, h2, idx, rank, gw, cnt = _merge(
        o_f, o_b, bonus, g, yb, gates.reshape(bsz, t, 2 * d_model), x,
        row(lnx_g), row(lnx_b), w_a.astype(BF16), w_b.astype(BF16), w_o.astype(BF16),
        row(g_ffn), wr_pad, br_pad)

    counts = cnt[0, :n_e].astype(jnp.int32)
    padded = ((counts + MOE_BLOCK - 1) // MOE_BLOCK) * MOE_BLOCK
    pend = jnp.cumsum(padded)
    pstart = pend - padded
    n_assign = m * TOP_K
    n_blocks = -(-n_assign // MOE_BLOCK) + n_e
    n_pad = n_blocks * MOE_BLOCK
    top_idx = idx[:, :TOP_K]
    dest = (pstart[top_idx] + rank[:, :TOP_K]).reshape(-1).astype(jnp.int32)
    n_used = (pend[-1] // MOE_BLOCK).astype(jnp.int32)
    blk_idx = jnp.minimum(jnp.arange(n_blocks, dtype=jnp.int32), n_used - 1)
    blk_e = jnp.sum((blk_idx[:, None] * MOE_BLOCK >= pend[None, :]).astype(jnp.int32), axis=1)
    blk_e = jnp.minimum(blk_e, n_e - 1)

    xs = _dispatch(dest, (pstart + counts).astype(jnp.int32), (padded - counts).astype(jnp.int32),
                   n_used.reshape(1), h2.reshape(m, SUBLANES, LANES), n_pad)
    ys = _experts(blk_idx, blk_e, n_used.reshape(1), xs.reshape(n_pad * SUBLANES, LANES), w1, b1, w2, b2)
    out = _combine(dest, ys.reshape(n_pad, SUBLANES, LANES), x1, gw)
    return out.reshape(bsz, t, d_model)


def kernel(x, g_mix, w_in, mu_prev, mu_next, w0_f, w2_f, w0_b, w2_b, a0_f, a2_f, a0_b, a2_b, g2, k_k, k_a, r_k, lnx_g, lnx_b, q_norm_g, k_norm_g, rpb, w_a, w_b, w_o, g_ffn, w_router, b_router, w1, b1, w2, b2):
    for l in range(g_mix.shape[0]):
        x = _layer(x, g_mix[l], w_in[l], mu_prev[l], mu_next[l], w0_f[l], w2_f[l], w0_b[l], w2_b[l],
                   a0_f[l], a2_f[l], a0_b[l], a2_b[l], g2[l], k_k[l], k_a[l], r_k[l], lnx_g[l], lnx_b[l],
                   q_norm_g[l], k_norm_g[l], rpb[l], w_a[l], w_b[l], w_o[l], g_ffn[l], w_router[l],
                   b_router[l], w1[l], b1[l], w2[l], b2[l])
    return x
```

```python
import functools
import math

import numpy as np
import jax
import jax.numpy as jnp
from jax import lax
from jax.experimental import pallas as pl
from jax.experimental.pallas import tpu as pltpu

F32 = jnp.float32
BF16 = jnp.bfloat16
ACT = jnp.bfloat16

LANES = 128
SUBLANES = 8
HEAD_DIM = 64
PAIR = 2 * HEAD_DIM
GRID_W = 64
NA_WR = 8
NA_WC = 16
W_LORA = 64
A_LORA = 64
G_LORA = 128
DECAY_SCALE = math.exp(-0.5)
GN_EPS = 64e-5
RMS_EPS = 1e-5
N_EXPERTS = 32
TOP_K = 4
MOE_BLOCK = 512
SWIGLU_LIMIT = 7.0
SWIGLU_ALPHA = 1.702
NEG_BIG = -1e30
CHUNK = 128
NA_ROWS_PER_STEP = 8
VMEM_LIMIT = 56 * 1024 * 1024


def _dot(a, b):
    return jnp.dot(a, b, preferred_element_type=F32)


def _dot_nt(a, b):
    return lax.dot_general(a, b, (((1,), (1,)), ((), ())), preferred_element_type=F32)


def _split(a):
    hi = a.astype(BF16)
    lo = (a - hi.astype(F32)).astype(BF16)
    return hi, lo


def _mm_exact_rhs(a, b_bf16):
    hi, lo = _split(a)
    return _dot(hi, b_bf16) + _dot(lo, b_bf16)


def _hilo(b):
    return jnp.stack(_split(b))


def _mm3(a, b_hilo):
    ah, al = _split(a)
    bh = b_hilo[0]
    return _dot(ah, bh) + _dot(al, bh) + _dot(ah, b_hilo[1])


def _store_rows_as_tiles(ref, val):
    n = val.shape[0]
    for s in range(SUBLANES):
        ref[pl.ds(s, n, stride=SUBLANES), :] = val[:, s * LANES:(s + 1) * LANES]


def _load_tiles_as_rows(ref, n):
    return jnp.concatenate([ref[pl.ds(s, n, stride=SUBLANES), :] for s in range(SUBLANES)], axis=1)


def _sigmoid(x):
    return 1.0 / (1.0 + jnp.exp(-x))


def _params(sem):
    return pltpu.CompilerParams(dimension_semantics=sem, vmem_limit_bytes=VMEM_LIMIT)


def _block_ones(n, blk):
    i = np.arange(n) // blk
    return jnp.asarray(i[:, None] == i[None, :], BF16)


def _inproj_kernel(x_ref, g_ref, wa_ref, wb_ref, wg_ref, pa_ref, pb_ref, pg_ref):
    x = x_ref[...]
    ms = jnp.mean(x * x, axis=-1, keepdims=True)
    h = (x * lax.rsqrt(ms + RMS_EPS) * g_ref[...]).astype(BF16)
    pa_ref[...] = _dot(h, wa_ref[...]).astype(pa_ref.dtype)
    pb_ref[...] = _dot(h, wb_ref[...]).astype(pb_ref.dtype)
    pg_ref[...] = _dot(h, wg_ref[...]).astype(pg_ref.dtype)


def _inproj(x2, g_mix, w_a, w_b, w_g, tm=256):
    m, d = x2.shape
    na, nb, ng = w_a.shape[1], w_b.shape[1], w_g.shape[1]
    full = lambda i: (0, 0)
    return pl.pallas_call(
        _inproj_kernel,
        grid=(m // tm,),
        in_specs=[
            pl.BlockSpec((tm, d), lambda i: (i, 0)),
            pl.BlockSpec((1, d), full),
            pl.BlockSpec((d, na), full),
            pl.BlockSpec((d, nb), full),
            pl.BlockSpec((d, ng), full),
        ],
        out_specs=[
            pl.BlockSpec((tm, na), lambda i: (i, 0)),
            pl.BlockSpec((tm, nb), lambda i: (i, 0)),
            pl.BlockSpec((tm, ng), lambda i: (i, 0)),
        ],
        out_shape=[
            jax.ShapeDtypeStruct((m, na), ACT),
            jax.ShapeDtypeStruct((m, nb), ACT),
            jax.ShapeDtypeStruct((m, ng), ACT),
        ],
        compiler_params=_params(("arbitrary",)),
        name="inproj",
    )(x2, g_mix, w_a, w_b, w_g)


def _prep_kernel(p_ref, prev_ref, next_ref, mup_ref, mun_ref, w0_ref, w2_ref, a0_ref, a2_ref,
                 g2_ref, kk_ref, ka_ref, rk_ref, ones_ref,
                 r_o, v_o, kk_o, lwf_o, lwb_o, kf_o, kb_o, bf_o, bb_o, bonus_o, g_o, *, d_a):
    i = pl.program_id(1)
    n_t = pl.num_programs(1)
    p = p_ref[...].astype(F32)
    tt = p.shape[0]
    halo = prev_ref.shape[0]
    prow = jnp.where(i > 0, prev_ref[halo - 1:halo, :].astype(F32), 0.0)
    nrow = jnp.where(i < n_t - 1, next_ref[0:1, :].astype(F32), 0.0)
    rid = lax.broadcasted_iota(jnp.int32, (tt, 1), 0)
    prev = jnp.where(rid == 0, prow, pltpu.roll(p, 1, axis=0))
    nxt = jnp.where(rid == tt - 1, nrow, pltpu.roll(p, tt - 1, axis=0))
    xa = p + mup_ref[...] * (prev - p) + mun_ref[...] * (nxt - p)

    r = xa[:, 0:d_a]
    k = xa[:, d_a:2 * d_a]
    v = xa[:, 2 * d_a:3 * d_a]
    o = 3 * d_a
    lw = xa[:, o:o + 2 * W_LORA]
    la = xa[:, o + 2 * W_LORA:o + 2 * W_LORA + 2 * A_LORA]
    lg = xa[:, o + 2 * W_LORA + 2 * A_LORA:]

    dpre = w0_ref[...] + _mm3(jnp.tanh(lw), w2_ref)
    apre = a0_ref[...] + _mm3(la, a2_ref)
    g = _mm3(_sigmoid(lg), g2_ref)
    logw = -DECAY_SCALE * _sigmoid(dpre)
    a = _sigmoid(apre)

    ones = ones_ref[...]
    kkr = k * kk_ref[...]
    ss = _mm_exact_rhs(kkr * kkr, ones)
    kk = kkr / jnp.maximum(jnp.sqrt(ss), 1e-12)

    ka = ka_ref[...]
    k_f = k * (1.0 + (a[:, :d_a] - 1.0) * ka)
    k_b = k * (1.0 + (a[:, d_a:] - 1.0) * ka)
    b_f = kk * a[:, :d_a]
    b_b = kk * a[:, d_a:]
    rk = rk_ref[...]
    bon = _mm_exact_rhs(r * (k_f + k_b) * rk, ones) * v

    bonus_o[...] = bon.astype(bonus_o.dtype)
    g_o[...] = g.astype(g_o.dtype)
    for pi in range(d_a // PAIR):
        sl = slice(pi * PAIR, (pi + 1) * PAIR)
        r_o[pi] = r[:, sl].astype(r_o.dtype)
        v_o[pi] = v[:, sl].astype(v_o.dtype)
        kk_o[pi] = kk[:, sl].astype(kk_o.dtype)
        lwf_o[pi] = logw[:, sl]
        lwb_o[pi] = logw[:, d_a + pi * PAIR:d_a + (pi + 1) * PAIR]
        kf_o[pi] = k_f[:, sl].astype(kf_o.dtype)
        kb_o[pi] = k_b[:, sl].astype(kb_o.dtype)
        bf_o[pi] = b_f[:, sl].astype(bf_o.dtype)
        bb_o[pi] = b_b[:, sl].astype(bb_o.dtype)


def _prep(pa3, mu_prev, mu_next, w0c, w2blk, a0c, a2blk, g2, k_k, k_a, r_k, d_a, tt=256):
    b, t, ac = pa3.shape
    n_t = t // tt
    n_p = d_a // PAIR
    ones = _block_ones(d_a, HEAD_DIM)
    c2 = lambda bi, i: (0, 0)
    c3 = lambda bi, i: (0, 0, 0)
    pair_spec = pl.BlockSpec((None, n_p, tt, PAIR), lambda bi, i: (bi, 0, i, 0))
    pair_act = jax.ShapeDtypeStruct((b, n_p, t, PAIR), ACT)
    pair_f32 = jax.ShapeDtypeStruct((b, n_p, t, PAIR), F32)
    flat_spec = pl.BlockSpec((None, tt, d_a), lambda bi, i: (bi, i, 0))
    flat_shape = jax.ShapeDtypeStruct((b, t, d_a), ACT)
    halo = 16
    r8 = tt // halo
    return pl.pallas_call(
        functools.partial(_prep_kernel, d_a=d_a),
        grid=(b, n_t),
        in_specs=[
            pl.BlockSpec((None, tt, ac), lambda bi, i: (bi, i, 0)),
            pl.BlockSpec((None, halo, ac), lambda bi, i: (bi, jnp.maximum(i * r8 - 1, 0), 0)),
            pl.BlockSpec((None, halo, ac), lambda bi, i: (bi, jnp.minimum((i + 1) * r8, t // halo - 1), 0)),
            pl.BlockSpec((1, ac), c2),
            pl.BlockSpec((1, ac), c2),
            pl.BlockSpec((1, 2 * d_a), c2),
            pl.BlockSpec((2, 2 * W_LORA, 2 * d_a), c3),
            pl.BlockSpec((1, 2 * d_a), c2),
            pl.BlockSpec((2, 2 * A_LORA, 2 * d_a), c3),
            pl.BlockSpec((2, G_LORA, d_a), c3),
            pl.BlockSpec((1, d_a), c2),
            pl.BlockSpec((1, d_a), c2),
            pl.BlockSpec((1, d_a), c2),
            pl.BlockSpec((d_a, d_a), c2),
        ],
        out_specs=[pair_spec] * 9 + [flat_spec] * 2,
        out_shape=[pair_act] * 3 + [pair_f32] * 2 + [pair_act] * 4 + [flat_shape] * 2,
        compiler_params=_params(("arbitrary", "arbitrary")),
        name="rwkv_prep",
    )(pa3, pa3, pa3, mu_prev, mu_next, w0c, _hilo(w2blk), a0c, _hilo(a2blk), _hilo(g2), k_k, k_a, r_k, ones)


def _mm_exact_rhs_left(tri_bf16, x):
    hi, lo = _split(x)
    return _dot(tri_bf16, hi) + _dot(tri_bf16, lo)


def _scan_stage(items, s_refs):
    c = items[0][0].shape[0]
    ri = lax.broadcasted_iota(jnp.int32, (c, c), 0)
    ci = lax.broadcasted_iota(jnp.int32, (c, c), 1)
    lane = lax.broadcasted_iota(jnp.int32, (1, PAIR), 1)
    m0 = (lane < HEAD_DIM).astype(F32)
    m1 = 1.0 - m0
    eye = (ri == ci).astype(F32)
    hi_ = lax.broadcasted_iota(jnp.int32, (PAIR, PAIR), 0) // HEAD_DIM
    hj_ = lax.broadcasted_iota(jnp.int32, (PAIR, PAIR), 1) // HEAD_DIM
    same_head = hi_ == hj_
    zero = jnp.zeros((), F32)
    n = len(items)

    def masks(reverse):
        if reverse:
            return ci >= ri, ci > ri, 0
        return ci <= ri, ci < ri, c - 1

    cums = [_mm_exact_rhs_left(masks(it[6])[0].astype(BF16), it[3]) for it in items]

    pre = []
    for (r, kk, v, lw, k, b, reverse), cum in zip(items, cums):
        end = masks(reverse)[2]
        cmid = cum[c // 2:c // 2 + 1, :]
        cend = cum[end:end + 1, :]
        r_abs = r * jnp.exp(cum)
        a_abs = -kk * jnp.exp(cum - lw)
        to_mid = jnp.exp(-cmid)
        from_mid = jnp.exp(cmid - cum)
        to_end = jnp.exp(cend - cum)
        pre.append(dict(
            r_abs=r_abs, a_abs=a_abs, r_rel=r_abs * to_mid, a_rel=a_abs * to_mid,
            k_rel=k * from_mid, b_rel=b * from_mid, k_end=k * to_end, b_end=b * to_end,
            d_tot=jnp.exp(cend), v=v))

    grams = []
    for p in pre:
        lhs = jnp.concatenate([p["r_rel"] * m0, p["r_rel"] * m1, p["a_rel"] * m0, p["a_rel"] * m1],
                              axis=0).astype(BF16)
        rhs = jnp.concatenate([p["k_rel"], p["b_rel"]], axis=0).astype(BF16)
        grams.append(_dot_nt(lhs, rhs))

    pws, ts = [], []
    for it, gram in zip(items, grams):
        strict = masks(it[6])[1]
        for e in range(2):
            a_ab = jnp.where(strict, gram[(2 + e) * c:(3 + e) * c, c:2 * c], zero)
            pws.append(a_ab)
            ts.append(eye + a_ab)
    for _ in range(int(round(math.log2(c))) - 1):
        pws = [_dot(pw.astype(BF16), pw.astype(BF16)) for pw in pws]
        ts = [t + _dot(t.astype(BF16), pw.astype(BF16)) for t, pw in zip(ts, pws)]

    v_blks = [jnp.concatenate([p["v"] * m0, p["v"] * m1], axis=0).astype(BF16) for p in pre]
    akvs, o_rks, rb_cats = [], [], []
    for it, gram, v_blk in zip(items, grams, v_blks):
        incl, strict, _ = masks(it[6])
        ak_cat = jnp.concatenate(
            [jnp.where(strict, gram[(2 + e) * c:(3 + e) * c, 0:c], zero) for e in range(2)], axis=1)
        rk_cat = jnp.concatenate(
            [jnp.where(incl, gram[e * c:(e + 1) * c, 0:c], zero) for e in range(2)], axis=1)
        rb_cats.append(jnp.concatenate(
            [jnp.where(incl, gram[e * c:(e + 1) * c, c:2 * c], zero) for e in range(2)], axis=1).astype(BF16))
        akvs.append(_dot(ak_cat.astype(BF16), v_blk))
        o_rks.append(_dot(rk_cat.astype(BF16), v_blk))

    xs = []
    for i, (p, akv) in enumerate(zip(pre, akvs)):
        t_cat = jnp.concatenate([ts[2 * i], ts[2 * i + 1]], axis=1).astype(BF16)
        y_blk = jnp.concatenate([
            jnp.concatenate([p["a_abs"] * m0, akv * m0], axis=1),
            jnp.concatenate([p["a_abs"] * m1, akv * m1], axis=1)], axis=0).astype(BF16)
        xs.append(_dot(t_cat, y_blk))

    s0s = [s_ref[...] for s_ref in s_refs]
    s0bs = [s0.astype(BF16) for s0 in s0s]
    us = [_dot(x[:, 0:PAIR].astype(BF16), s0b) + x[:, PAIR:2 * PAIR] for x, s0b in zip(xs, s0bs)]
    outs = []
    for i in range(n):
        p, u = pre[i], us[i]
        u_blk = jnp.concatenate([u * m0, u * m1], axis=0).astype(BF16)
        outs.append(_dot(p["r_abs"].astype(BF16), s0bs[i]) + _dot(rb_cats[i], u_blk) + o_rks[i])
        kb_t = jnp.concatenate([p["b_end"].T, p["k_end"].T], axis=1).astype(BF16)
        uv = jnp.concatenate([u, p["v"]], axis=0).astype(BF16)
        d_col = jnp.broadcast_to(p["d_tot"], (PAIR, PAIR)).T
        s_refs[i][...] = jnp.where(same_head, d_col * s0s[i] + _dot(kb_t, uv), zero)
    return outs


def _scan_kernel(rf, vf, kkf, lwf, kf, bf, rb, vb, kkb, lwb, kb, bb, of_ref, ob_ref, s_ref):
    @pl.when(pl.program_id(1) == 0)
    def _():
        s_ref[...] = jnp.zeros_like(s_ref)

    n_p = rf.shape[0]
    items, s_refs = [], []
    for p in range(n_p):
        ld = lambda ref: ref[p].astype(F32)
        items.append((ld(rf), ld(kkf), ld(vf), lwf[p], ld(kf), ld(bf), False))
        s_refs.append(s_ref.at[2 * p])
        items.append((ld(rb), ld(kkb), ld(vb), lwb[p], ld(kb), ld(bb), True))
        s_refs.append(s_ref.at[2 * p + 1])
    outs = _scan_stage(items, s_refs)
    for p in range(n_p):
        of_ref[p] = outs[2 * p].astype(of_ref.dtype)
        ob_ref[p] = outs[2 * p + 1].astype(ob_ref.dtype)


def _scan(r, v, kk, lw_f, lw_b, k_f, k_b, b_f, b_b):
    bsz, n_p, t, _ = r.shape
    nc = t // CHUNK
    fwd = pl.BlockSpec((None, n_p, CHUNK, PAIR), lambda bi, c: (bi, 0, c, 0))
    bwd = pl.BlockSpec((None, n_p, CHUNK, PAIR), lambda bi, c: (bi, 0, nc - 1 - c, 0))
    shape = jax.ShapeDtypeStruct((bsz, n_p, t, PAIR), ACT)
    return pl.pallas_call(
        _scan_kernel,
        grid=(bsz, nc),
        in_specs=[fwd] * 6 + [bwd] * 6,
        out_specs=[fwd, bwd],
        out_shape=[shape, shape],
        scratch_shapes=[pltpu.VMEM((2 * n_p, PAIR, PAIR), F32)],
        compiler_params=_params(("arbitrary", "arbitrary")),
        name="rwkv_scan",
    )(r, v, kk, lw_f, k_f, b_f, r, v, kk, lw_b, k_b, b_b)


def _na_kernel(q_ref, k_ref, v_ref, gq_ref, gk_ref, tab_ref, ones_ref, o_ref, qn_s, kn_s, vb_s, *, rows):
    ones = ones_ref[...]
    scale = HEAD_DIM ** -0.5
    q = q_ref[...].astype(F32)
    k = k_ref[...].astype(F32)
    inv_d = 1.0 / HEAD_DIM
    qn = q * lax.rsqrt(_mm_exact_rhs(q * q, ones) * inv_d + RMS_EPS) * (gq_ref[...] * scale)
    kn = k * lax.rsqrt(_mm_exact_rhs(k * k, ones) * inv_d + RMS_EPS) * gk_ref[...]
    qn_s[...] = qn.astype(BF16)
    kn_s[...] = kn.astype(BF16)
    vb_s[...] = v_ref[...].astype(BF16)
    lane = lax.broadcasted_iota(jnp.int32, (1, PAIR), 1)
    head0 = lane < HEAD_DIM
    win = NA_WR * GRID_W

    def row_group(gi, carry):
        rws = [gi * NA_ROWS_PER_STEP + j for j in range(NA_ROWS_PER_STEP)]
        rss = [jnp.clip(r - NA_WR // 2, 0, rows - NA_WR) for r in rws]
        q_rows = [qn_s[pl.ds(pl.multiple_of(r * GRID_W, GRID_W), GRID_W), :] for r in rws]
        k_wins = [kn_s[pl.ds(pl.multiple_of(rs * GRID_W, GRID_W), win), :] for rs in rss]
        v_wins = [vb_s[pl.ds(pl.multiple_of(rs * GRID_W, GRID_W), win), :] for rs in rss]
        ss = []
        for j in range(NA_ROWS_PER_STEP):
            d0 = rss[j] - rws[j] + NA_WR - 1
            for e in range(2):
                mask = head0 if e == 0 else jnp.logical_not(head0)
                qm = jnp.where(mask, q_rows[j], jnp.zeros_like(q_rows[j]))
                bias = jnp.concatenate(
                    [tab_ref[e, pl.ds(d0 + 2 * m, 1)][0] for m in range(NA_WR // 2)], axis=1)
                ss.append(_dot_nt(qm, k_wins[j]) + bias)
        mxs = [jnp.max(s, axis=-1, keepdims=True) for s in ss]
        ps = [jnp.exp(s - mx) for s, mx in zip(ss, mxs)]
        ls = [jnp.sum(p, axis=-1, keepdims=True) for p in ps]
        pvs = [_dot(p.astype(BF16), v_wins[i // 2]) for i, p in enumerate(ps)]
        for j in range(NA_ROWS_PER_STEP):
            o0 = pvs[2 * j] / ls[2 * j]
            o1 = pvs[2 * j + 1] / ls[2 * j + 1]
            o_ref[pl.ds(pl.multiple_of(rws[j] * GRID_W, GRID_W), GRID_W), :] = (
                jnp.where(head0, o0, o1).astype(o_ref.dtype))
        return carry

    lax.fori_loop(0, rows // NA_ROWS_PER_STEP, row_group, 0)


def _na_bias_table(rpb):
    qc = np.arange(GRID_W)
    kc = np.arange(GRID_W)
    cs = np.clip(qc - NA_WC // 2, 0, GRID_W - NA_WC)
    valid = (kc[None, :] >= cs[:, None]) & (kc[None, :] < cs[:, None] + NA_WC)
    dc = np.clip(kc[None, :] - qc[:, None] + NA_WC - 1, 0, 2 * NA_WC - 2)
    b = rpb.astype(F32)[:, :, dc]
    b = jnp.where(jnp.asarray(valid)[None, None], b, NEG_BIG)
    return jnp.concatenate([b[:, :-1], b[:, 1:]], axis=-1)


def _na(pb3, q_gain, k_gain, table, d_b):
    bsz, t, _ = pb3.shape
    rows = t // GRID_W
    n_p = d_b // PAIR
    ones = _block_ones(PAIR, HEAD_DIM)
    gq = jnp.tile(q_gain.reshape(1, HEAD_DIM), (1, 2))
    gk = jnp.tile(k_gain.reshape(1, HEAD_DIM), (1, 2))
    n_d = table.shape[1]
    c2 = lambda bi, p: (0, 0)
    return pl.pallas_call(
        functools.partial(_na_kernel, rows=rows),
        grid=(bsz, n_p),
        in_specs=[
            pl.BlockSpec((None, t, PAIR), lambda bi, p: (bi, 0, p)),
            pl.BlockSpec((None, t, PAIR), lambda bi, p: (bi, 0, n_p + p)),
            pl.BlockSpec((None, t, PAIR), lambda bi, p: (bi, 0, 2 * n_p + p)),
            pl.BlockSpec((1, PAIR), c2),
            pl.BlockSpec((1, PAIR), c2),
            pl.BlockSpec((2, n_d, GRID_W, PAIR), lambda bi, p: (p, 0, 0, 0)),
            pl.BlockSpec((PAIR, PAIR), c2),
        ],
        out_specs=pl.BlockSpec((None, t, PAIR), lambda bi, p: (bi, 0, p)),
        out_shape=jax.ShapeDtypeStruct((bsz, t, d_b), ACT),
        scratch_shapes=[pltpu.VMEM((t, PAIR), BF16)] * 3,
        compiler_params=_params(("arbitrary", "arbitrary")),
        name="natten",
    )(pb3, pb3, pb3, gq, gk, table, ones)


def _merge_kernel(of_ref, ob_ref, bonus_ref, g_ref, yb_ref, gates_ref, x_ref,
                  lng_ref, lnb_ref, wa_ref, wb_ref, wo_ref, gffn_ref, wr_ref, br_ref,
                  ones_ref, tri_ref,
                  x1_ref, h2_ref, idx_ref, rank_ref, gw_ref, cnt_ref, carry_ref, *, d_model):
    first = jnp.logical_and(pl.program_id(0) == 0, pl.program_id(1) == 0)

    @pl.when(first)
    def _():
        carry_ref[...] = jnp.zeros_like(carry_ref)

    n_p = of_ref.shape[0]
    o = jnp.concatenate([of_ref[p].astype(F32) + ob_ref[p].astype(F32) for p in range(n_p)],
                        axis=1)
    ones = ones_ref[...]
    inv_d = 1.0 / HEAD_DIM
    mu = _mm_exact_rhs(o, ones) * inv_d
    dv = o - mu
    var = _mm_exact_rhs(dv * dv, ones) * inv_d
    y = dv * lax.rsqrt(var + GN_EPS) * lng_ref[...] + lnb_ref[...] + bonus_ref[...].astype(F32)
    ya = y * g_ref[...].astype(F32)

    gates = gates_ref[...].astype(F32)
    pa = _dot(ya.astype(BF16), wa_ref[...])
    pb = _dot(yb_ref[...].astype(BF16), wb_ref[...])
    merged = _sigmoid(gates[:, :d_model]) * pa + _sigmoid(gates[:, d_model:]) * pb
    x1 = x_ref[...] + _dot(merged.astype(BF16), wo_ref[...])
    x1_ref[...] = x1
    ms = jnp.mean(x1 * x1, axis=-1, keepdims=True)
    h2 = x1 * lax.rsqrt(ms + RMS_EPS) * gffn_ref[...]
    _store_rows_as_tiles(h2_ref, h2)

    logits = _mm3(h2, wr_ref) + br_ref[...]
    tm = logits.shape[0]
    lane = lax.broadcasted_iota(jnp.int32, (tm, LANES), 1)
    work = logits
    vals, idxs = [], []
    for _ in range(TOP_K):
        m = jnp.max(work, axis=-1, keepdims=True)
        ix = jnp.min(jnp.where(work == m, lane, LANES), axis=-1, keepdims=True)
        vals.append(m)
        idxs.append(ix)
        work = jnp.where(lane == ix, -jnp.inf, work)
    es = [jnp.exp(vk - vals[0]) for vk in vals]
    den = es[0] + es[1] + es[2] + es[3]
    member = jnp.zeros((tm, LANES), F32)
    for ix in idxs:
        member = member + (lane == ix).astype(F32)
    before = _dot(tri_ref[...], member.astype(BF16)) + carry_ref[...]
    idx_out = jnp.zeros((tm, LANES), jnp.int32)
    rank_out = jnp.zeros((tm, LANES), jnp.int32)
    gw_out = jnp.zeros((tm, LANES), F32)
    for kq in range(TOP_K):
        rk = jnp.sum(jnp.where(lane == idxs[kq], before, 0.0), axis=-1, keepdims=True)
        sel = lane == kq
        idx_out = jnp.where(sel, idxs[kq], idx_out)
        rank_out = jnp.where(sel, rk.astype(jnp.int32), rank_out)
        gw_out = jnp.where(sel, es[kq] / den, gw_out)
    idx_ref[...] = idx_out
    rank_ref[...] = rank_out
    gw_ref[...] = gw_out
    carry_ref[...] = carry_ref[...] + jnp.sum(member, axis=0, keepdims=True)
    cnt_ref[...] = carry_ref[...]


def _merge(o_f, o_b, bonus, g, yb, gates3, x3, lnx_g, lnx_b, w_a, w_b, w_o, g_ffn, wr_pad, br_pad, tm=512):
    bsz, n_p, t, _ = o_f.shape
    d_a = n_p * PAIR
    d_b = yb.shape[-1]
    d_model = x3.shape[-1]
    n_t = t // tm
    m = bsz * t
    ones = _block_ones(d_a, HEAD_DIM)
    tri = jnp.asarray(np.tril(np.ones((tm, tm)), -1), BF16)
    c2 = lambda bi, i: (0, 0)
    tok = lambda w: pl.BlockSpec((None, tm, w), lambda bi, i: (bi, i, 0))
    flat = lambda w: pl.BlockSpec((tm, w), lambda bi, i: (bi * n_t + i, 0))
    pair = pl.BlockSpec((None, n_p, tm, PAIR), lambda bi, i: (bi, 0, i, 0))
    return pl.pallas_call(
        functools.partial(_merge_kernel, d_model=d_model),
        grid=(bsz, n_t),
        in_specs=[
            pair, pair, tok(d_a), tok(d_a), tok(d_b), tok(2 * d_model), tok(d_model),
            pl.BlockSpec((1, d_a), c2), pl.BlockSpec((1, d_a), c2),
            pl.BlockSpec((d_a, d_model), c2), pl.BlockSpec((d_b, d_model), c2),
            pl.BlockSpec((d_model, d_model), c2), pl.BlockSpec((1, d_model), c2),
            pl.BlockSpec((2, d_model, LANES), lambda bi, i: (0, 0, 0)), pl.BlockSpec((1, LANES), c2),
            pl.BlockSpec((d_a, d_a), c2), pl.BlockSpec((tm, tm), c2),
        ],
        out_specs=[flat(d_model),
                   pl.BlockSpec((tm * SUBLANES, LANES), lambda bi, i: (bi * n_t + i, 0)),
                   flat(LANES), flat(LANES), flat(LANES),
                   pl.BlockSpec((1, LANES), c2)],
        out_shape=[
            jax.ShapeDtypeStruct((m, d_model), F32),
            jax.ShapeDtypeStruct((m * SUBLANES, LANES), F32),
            jax.ShapeDtypeStruct((m, LANES), jnp.int32),
            jax.ShapeDtypeStruct((m, LANES), jnp.int32),
            jax.ShapeDtypeStruct((m, LANES), F32),
            jax.ShapeDtypeStruct((1, LANES), F32),
        ],
        scratch_shapes=[pltpu.VMEM((1, LANES), F32)],
        compiler_params=_params(("arbitrary", "arbitrary")),
        name="merge_router",
    )(o_f, o_b, bonus, g, yb, gates3, x3, lnx_g, lnx_b, w_a, w_b, w_o, g_ffn, _hilo(wr_pad), br_pad, ones, tri)


def _dispatch_kernel(ps_ref, pn_ref, nu_ref, dest_ref, h_ref, xs_ref, zbuf, sem, zsem, csem, bsem):
    i = pl.program_id(0)
    tm = h_ref.shape[0]
    n_e = ps_ref.shape[0]
    nb = xs_ref.shape[0] // MOE_BLOCK

    def row_copy(t, kq):
        return pltpu.make_async_copy(
            h_ref.at[t], xs_ref.at[dest_ref[0, t * TOP_K + kq]], sem)

    def pad_copy(r):
        return pltpu.make_async_copy(zbuf.at[0], xs_ref.at[r], zsem)

    def oct_copy(o):
        return pltpu.make_async_copy(
            zbuf.at[pl.ds(0, SUBLANES)], xs_ref.at[pl.ds(o * SUBLANES, SUBLANES)], csem)

    def blk_copy(b):
        return pltpu.make_async_copy(zbuf, xs_ref.at[pl.ds(b * MOE_BLOCK, MOE_BLOCK)], bsem)

    def pads(fn):
        def per_expert(e, c):
            start = ps_ref[e]
            end = start + pn_ref[e]
            first_oct = (start + SUBLANES - 1) // SUBLANES

            def single(r, c2):
                fn(pad_copy(r))
                return c2

            def octet(o, c2):
                fn(oct_copy(o))
                return c2
            c = lax.fori_loop(start, jnp.minimum(first_oct * SUBLANES, end), single, c)
            return lax.fori_loop(first_oct, end // SUBLANES, octet, c)
        lax.fori_loop(0, n_e, per_expert, 0)

        def per_block(b, c):
            fn(blk_copy(b))
            return c
        lax.fori_loop(nu_ref[0], nb, per_block, 0)

    @pl.when(i == 0)
    def _():
        zbuf[...] = jnp.zeros_like(zbuf)
        pads(lambda cp: cp.start())

    def issue(t, c):
        for kq in range(TOP_K):
            row_copy(t, kq).start()
        return c

    def drain(t, c):
        for kq in range(TOP_K):
            row_copy(t, kq).wait()
        return c

    lax.fori_loop(0, tm, issue, 0, unroll=8)
    lax.fori_loop(0, tm, drain, 0, unroll=8)

    @pl.when(i == 0)
    def _():
        pads(lambda cp: cp.wait())


def _dispatch(dest, pad_start, pad_n, n_used, h2, n_pad, tm=256):
    m, ds, dl = h2.shape
    nt = m // tm
    dest3 = dest.reshape(nt, 1, tm * TOP_K)
    grid_spec = pltpu.PrefetchScalarGridSpec(
        num_scalar_prefetch=3,
        grid=(nt,),
        in_specs=[
            pl.BlockSpec((None, 1, tm * TOP_K), lambda i, ps, pn, nu: (i, 0, 0), memory_space=pltpu.SMEM),
            pl.BlockSpec((tm, ds, dl), lambda i, ps, pn, nu: (i, 0, 0)),
        ],
        out_specs=pl.BlockSpec(memory_space=pl.ANY),
        scratch_shapes=[pltpu.VMEM((MOE_BLOCK, ds, dl), F32)] + [pltpu.SemaphoreType.DMA(())] * 4,
    )
    return pl.pallas_call(
        _dispatch_kernel,
        grid_spec=grid_spec,
        out_shape=jax.ShapeDtypeStruct((n_pad, ds, dl), F32),
        compiler_params=_params(("arbitrary",)),
        name="moe_dispatch",
    )(pad_start, pad_n, n_used, dest3, h2)


def _expert_kernel(blk_ref, be_ref, nu_ref, xs_ref, w1_ref, b1_ref, w2_ref, b2_ref, ys_ref,
                   w1b_ref, w2b_ref, *, d_e):
    del blk_ref
    i = pl.program_id(0)
    new_expert = jnp.logical_or(i == 0, be_ref[i] != be_ref[jnp.maximum(i - 1, 0)])

    @pl.when(new_expert)
    def _():
        w1b_ref[...] = w1_ref[...].astype(BF16)
        w2b_ref[...] = w2_ref[...].astype(BF16)

    @pl.when(i < nu_ref[0])
    def _():
        x = _load_tiles_as_rows(xs_ref, MOE_BLOCK).astype(BF16)
        u = _dot(x, w1b_ref[...]) + b1_ref[...]
        glu = jnp.minimum(u[:, :d_e], SWIGLU_LIMIT)
        lin = jnp.clip(u[:, d_e:], -SWIGLU_LIMIT, SWIGLU_LIMIT)
        act = glu * _sigmoid(SWIGLU_ALPHA * glu) * (lin + 1.0)
        _store_rows_as_tiles(ys_ref, _dot(act.astype(BF16), w2b_ref[...]) + b2_ref[...])

    @pl.when(i >= nu_ref[0])
    def _():
        ys_ref[...] = jnp.zeros_like(ys_ref)


def _experts(blk_idx, blk_e, n_used, xs, w1, b1, w2, b2):
    n_pad = xs.shape[0] // SUBLANES
    d = SUBLANES * LANES
    nb = n_pad // MOE_BLOCK
    n_e, _, d2 = w1.shape
    d_e = d2 // 2
    grid_spec = pltpu.PrefetchScalarGridSpec(
        num_scalar_prefetch=3,
        grid=(nb,),
        in_specs=[
            pl.BlockSpec((MOE_BLOCK * SUBLANES, LANES), lambda i, bi, be, nu: (bi[i], 0)),
            pl.BlockSpec((None, d, d2), lambda i, bi, be, nu: (be[i], 0, 0)),
            pl.BlockSpec((None, 1, d2), lambda i, bi, be, nu: (be[i], 0, 0)),
            pl.BlockSpec((None, d_e, d), lambda i, bi, be, nu: (be[i], 0, 0)),
            pl.BlockSpec((None, 1, d), lambda i, bi, be, nu: (be[i], 0, 0)),
        ],
        out_specs=pl.BlockSpec((MOE_BLOCK * SUBLANES, LANES), lambda i, bi, be, nu: (i, 0)),
        scratch_shapes=[pltpu.VMEM((d, d2), BF16), pltpu.VMEM((d_e, d), BF16)],
    )
    return pl.pallas_call(
        functools.partial(_expert_kernel, d_e=d_e),
        grid_spec=grid_spec,
        out_shape=jax.ShapeDtypeStruct((n_pad * SUBLANES, LANES), F32),
        compiler_params=_params(("arbitrary",)),
        name="moe_experts",
    )(blk_idx, blk_e, n_used, xs, w1, b1.reshape(n_e, 1, d2), w2, b2.reshape(n_e, 1, d))


def _combine_kernel(dest_ref, dnext_ref, ys_ref, x1_ref, gw_ref, o_ref, buf, sems):
    i = pl.program_id(0)
    nt = pl.num_programs(0)
    tm = x1_ref.shape[0]
    slot = i % 2

    def row_copy(d_ref, sl, t, kq):
        return pltpu.make_async_copy(
            ys_ref.at[d_ref[0, t * TOP_K + kq]],
            buf.at[sl, kq, pl.ds(pl.multiple_of(t * SUBLANES, SUBLANES), SUBLANES)], sems.at[sl])

    def issue_all(d_ref, sl):
        def body(t, c):
            for kq in range(TOP_K):
                row_copy(d_ref, sl, t, kq).start()
            return c
        lax.fori_loop(0, tm, body, 0, unroll=8)

    @pl.when(i == 0)
    def _():
        issue_all(dest_ref, 0)

    @pl.when(i + 1 < nt)
    def _():
        issue_all(dnext_ref, 1 - slot)

    def drain(t, c):
        for kq in range(TOP_K):
            row_copy(dest_ref, slot, t, kq).wait()
        return c

    lax.fori_loop(0, tm, drain, 0, unroll=8)
    gw = gw_ref[...]
    for s in range(SUBLANES):
        acc = x1_ref[:, s * LANES:(s + 1) * LANES]
        for kq in range(TOP_K):
            acc = acc + gw[:, kq:kq + 1] * buf[slot, kq, pl.ds(s, tm, stride=SUBLANES), :]
        o_ref[:, s * LANES:(s + 1) * LANES] = acc


def _combine(dest, ys, x1, gw, tm=128):
    m, d = x1.shape
    nt = m // tm
    dest3 = dest.reshape(nt, 1, tm * TOP_K)
    return pl.pallas_call(
        _combine_kernel,
        grid=(nt,),
        in_specs=[
            pl.BlockSpec((None, 1, tm * TOP_K), lambda i: (i, 0, 0), memory_space=pltpu.SMEM),
            pl.BlockSpec((None, 1, tm * TOP_K), lambda i: (jnp.minimum(i + 1, nt - 1), 0, 0),
                         memory_space=pltpu.SMEM),
            pl.BlockSpec(memory_space=pl.ANY),
            pl.BlockSpec((tm, d), lambda i: (i, 0)),
            pl.BlockSpec((tm, LANES), lambda i: (i, 0)),
        ],
        out_specs=pl.BlockSpec((tm, d), lambda i: (i, 0)),
        out_shape=jax.ShapeDtypeStruct((m, d), F32),
        scratch_shapes=[pltpu.VMEM((2, TOP_K, tm * SUBLANES, LANES), F32), pltpu.SemaphoreType.DMA((2,))],
        compiler_params=_params(("arbitrary",)),
        name="moe_combine",
    )(dest3, dest3, ys, x1, gw)


def _blockdiag2(a, b):
    za = jnp.zeros((a.shape[0], b.shape[1]), a.dtype)
    zb = jnp.zeros((b.shape[0], a.shape[1]), a.dtype)
    return jnp.concatenate([jnp.concatenate([a, za], axis=1), jnp.concatenate([zb, b], axis=1)], axis=0)


def _layer(x, g_mix, w_in, mu_prev, mu_next, w0_f, w2_f, w0_b, w2_b, a0_f, a2_f, a0_b, a2_b,
           g2, k_k, k_a, r_k, lnx_g, lnx_b, q_norm_g, k_norm_g, rpb, w_a, w_b, w_o,
           g_ffn, w_router, b_router, w1, b1, w2, b2):
    bsz, t, d_model = x.shape
    m = bsz * t
    d_a = w_a.shape[0]
    d_b = w_b.shape[0]
    a_cols = mu_prev.shape[0]
    b_cols = 3 * d_b
    row = lambda a: a.reshape(1, -1).astype(F32)

    w_in_b = w_in.astype(BF16)
    pa, pb, gates = _inproj(x.reshape(m, d_model), row(g_mix), w_in_b[:, :a_cols],
                            w_in_b[:, a_cols:a_cols + b_cols], w_in_b[:, a_cols + b_cols:])

    prep = _prep(pa.reshape(bsz, t, a_cols), row(mu_prev), row(mu_next),
                 jnp.concatenate([row(w0_f), row(w0_b)], axis=1), _blockdiag2(w2_f, w2_b),
                 jnp.concatenate([row(a0_f), row(a0_b)], axis=1), _blockdiag2(a2_f, a2_b),
                 g2, row(k_k), row(k_a), row(r_k), d_a)
    r, v, kk, lw_f, lw_b, k_f, k_b, b_f, b_b, bonus, g = prep
    o_f, o_b = _scan(r, v, kk, lw_f, lw_b, k_f, k_b, b_f, b_b)

    yb = _na(pb.reshape(bsz, t, b_cols), q_norm_g, k_norm_g, _na_bias_table(rpb), d_b)

    n_e = w_router.shape[1]
    wr_pad = jnp.zeros((d_model, LANES), F32).at[:, :n_e].set(w_router)
    br_pad = jnp.full((1, LANES), NEG_BIG, F32).at[0, :n_e].set(b_router)
    x1, h2, idx, rank, gw, cnt = _merge(
        o_f, o_b, bonus, g, yb, gates.reshape(bsz, t, 2 * d_model), x,
        row(lnx_g), row(lnx_b), w_a.astype(BF16), w_b.astype(BF16), w_o.astype(BF16),
        row(g_ffn), wr_pad, br_pad)

    counts = cnt[0, :n_e].astype(jnp.int32)
    padded = ((counts + MOE_BLOCK - 1) // MOE_BLOCK) * MOE_BLOCK
    pend = jnp.cumsum(padded)
    pstart = pend - padded
    n_assign = m * TOP_K
    n_blocks = -(-n_assign // MOE_BLOCK) + n_e
    n_pad = n_blocks * MOE_BLOCK
    top_idx = idx[:, :TOP_K]
    dest = (pstart[top_idx] + rank[:, :TOP_K]).reshape(-1).astype(jnp.int32)
    n_used = (pend[-1] // MOE_BLOCK).astype(jnp.int32)
    blk_idx = jnp.minimum(jnp.arange(n_blocks, dtype=jnp.int32), n_used - 1)
    blk_e = jnp.sum((blk_idx[:, None] * MOE_BLOCK >= pend[None, :]).astype(jnp.int32), axis=1)
    blk_e = jnp.minimum(blk_e, n_e - 1)

    xs = _dispatch(dest, (pstart + counts).astype(jnp.int32), (padded - counts).astype(jnp.int32),
                   n_used.reshape(1), h2.reshape(m, SUBLANES, LANES), n_pad)
    ys = _experts(blk_idx, blk_e, n_used.reshape(1), xs.reshape(n_pad * SUBLANES, LANES), w1, b1, w2, b2)
    out = _combine(dest, ys.reshape(n_pad, SUBLANES, LANES), x1, gw)
    return out.reshape(bsz, t, d_model)


def kernel(x, g_mix, w_in, mu_prev, mu_next, w0_f, w2_f, w0_b, w2_b, a0_f, a2_f, a0_b, a2_b, g2, k_k, k_a, r_k, lnx_g, lnx_b, q_norm_g, k_norm_g, rpb, w_a, w_b, w_o, g_ffn, w_router, b_router, w1, b1, w2, b2):
    for l in range(g_mix.shape[0]):
        x = _layer(x, g_mix[l], w_in[l], mu_prev[l], mu_next[l], w0_f[l], w2_f[l], w0_b[l], w2_b[l],
                   a0_f[l], a2_f[l], a0_b[l], a2_b[l], g2[l], k_k[l], k_a[l], r_k[l], lnx_g[l], lnx_b[l],
                   q_norm_g[l], k_norm_g[l], rpb[l], w_a[l], w_b[l], w_o[l], g_ffn[l], w_router[l],
                   b_router[l], w1[l], b1[l], w2[l], b2[l])
    return x
```

```python
import functools
import math

import numpy as np
import jax
import jax.numpy as jnp
from jax import lax
from jax.experimental import pallas as pl
from jax.experimental.pallas import tpu as pltpu

F32 = jnp.float32
BF16 = jnp.bfloat16
ACT = jnp.bfloat16

LANES = 128
SUBLANES = 8
HEAD_DIM = 64
PAIR = 2 * HEAD_DIM
GRID_W = 64
NA_WR = 8
NA_WC = 16
W_LORA = 64
A_LORA = 64
G_LORA = 128
DECAY_SCALE = math.exp(-0.5)
GN_EPS = 64e-5
RMS_EPS = 1e-5
N_EXPERTS = 32
TOP_K = 4
MOE_BLOCK = 512
SWIGLU_LIMIT = 7.0
SWIGLU_ALPHA = 1.702
NEG_BIG = -1e30
CHUNK = 128
NA_ROWS_PER_STEP = 8
VMEM_LIMIT = 56 * 1024 * 1024


def _dot(a, b):
    return jnp.dot(a, b, preferred_element_type=F32)


def _dot_nt(a, b):
    return lax.dot_general(a, b, (((1,), (1,)), ((), ())), preferred_element_type=F32)


def _split(a):
    hi = a.astype(BF16)
    lo = (a - hi.astype(F32)).astype(BF16)
    return hi, lo


def _mm_exact_rhs(a, b_bf16):
    hi, lo = _split(a)
    return _dot(hi, b_bf16) + _dot(lo, b_bf16)


def _hilo(b):
    return jnp.stack(_split(b))


def _mm3(a, b_hilo):
    ah, al = _split(a)
    bh = b_hilo[0]
    return _dot(ah, bh) + _dot(al, bh) + _dot(ah, b_hilo[1])


def _store_rows_as_tiles(ref, val):
    n = val.shape[0]
    for s in range(SUBLANES):
        ref[pl.ds(s, n, stride=SUBLANES), :] = val[:, s * LANES:(s + 1) * LANES]


def _load_tiles_as_rows(ref, n):
    return jnp.concatenate([ref[pl.ds(s, n, stride=SUBLANES), :] for s in range(SUBLANES)], axis=1)


def _sigmoid(x):
    return 1.0 / (1.0 + jnp.exp(-x))


def _params(sem):
    return pltpu.CompilerParams(dimension_semantics=sem, vmem_limit_bytes=VMEM_LIMIT)


def _block_ones(n, blk):
    i = np.arange(n) // blk
    return jnp.asarray(i[:, None] == i[None, :], BF16)


def _inproj_kernel(x_ref, g_ref, wa_ref, wb_ref, wg_ref, pa_ref, pb_ref, pg_ref):
    x = x_ref[...]
    ms = jnp.mean(x * x, axis=-1, keepdims=True)
    h = (x * lax.rsqrt(ms + RMS_EPS) * g_ref[...]).astype(BF16)
    pa_ref[...] = _dot(h, wa_ref[...]).astype(pa_ref.dtype)
    pb_ref[...] = _dot(h, wb_ref[...]).astype(pb_ref.dtype)
    pg_ref[...] = _dot(h, wg_ref[...]).astype(pg_ref.dtype)


def _inproj(x2, g_mix, w_a, w_b, w_g, tm=256):
    m, d = x2.shape
    na, nb, ng = w_a.shape[1], w_b.shape[1], w_g.shape[1]
    full = lambda i: (0, 0)
    return pl.pallas_call(
        _inproj_kernel,
        grid=(m // tm,),
        in_specs=[
            pl.BlockSpec((tm, d), lambda i: (i, 0)),
            pl.BlockSpec((1, d), full),
            pl.BlockSpec((d, na), full),
            pl.BlockSpec((d, nb), full),
            pl.BlockSpec((d, ng), full),
        ],
        out_specs=[
            pl.BlockSpec((tm, na), lambda i: (i, 0)),
            pl.BlockSpec((tm, nb), lambda i: (i, 0)),
            pl.BlockSpec((tm, ng), lambda i: (i, 0)),
        ],
        out_shape=[
            jax.ShapeDtypeStruct((m, na), ACT),
            jax.ShapeDtypeStruct((m, nb), ACT),
            jax.ShapeDtypeStruct((m, ng), ACT),
        ],
        compiler_params=_params(("arbitrary",)),
        name="inproj",
    )(x2, g_mix, w_a, w_b, w_g)


def _prep_kernel(p_ref, prev_ref, next_ref, mup_ref, mun_ref, w0_ref, w2_ref, a0_ref, a2_ref,
                 g2_ref, kk_ref, ka_ref, rk_ref, ones_ref,
                 r_o, v_o, kk_o, lwf_o, lwb_o, kf_o, kb_o, bf_o, bb_o, bonus_o, g_o, *, d_a):
    i = pl.program_id(1)
    n_t = pl.num_programs(1)
    p = p_ref[...].astype(F32)
    tt = p.shape[0]
    halo = prev_ref.shape[0]
    prow = jnp.where(i > 0, prev_ref[halo - 1:halo, :].astype(F32), 0.0)
    nrow = jnp.where(i < n_t - 1, next_ref[0:1, :].astype(F32), 0.0)
    rid = lax.broadcasted_iota(jnp.int32, (tt, 1), 0)
    prev = jnp.where(rid == 0, prow, pltpu.roll(p, 1, axis=0))
    nxt = jnp.where(rid == tt - 1, nrow, pltpu.roll(p, tt - 1, axis=0))
    xa = p + mup_ref[...] * (prev - p) + mun_ref[...] * (nxt - p)

    r = xa[:, 0:d_a]
    k = xa[:, d_a:2 * d_a]
    v = xa[:, 2 * d_a:3 * d_a]
    o = 3 * d_a
    lw = xa[:, o:o + 2 * W_LORA]
    la = xa[:, o + 2 * W_LORA:o + 2 * W_LORA + 2 * A_LORA]
    lg = xa[:, o + 2 * W_LORA + 2 * A_LORA:]

    dpre = w0_ref[...] + _mm3(jnp.tanh(lw), w2_ref)
    apre = a0_ref[...] + _mm3(la, a2_ref)
    g = _mm3(_sigmoid(lg), g2_ref)
    logw = -DECAY_SCALE * _sigmoid(dpre)
    a = _sigmoid(apre)

    ones = ones_ref[...]
    kkr = k * kk_ref[...]
    ss = _mm_exact_rhs(kkr * kkr, ones)
    kk = kkr / jnp.maximum(jnp.sqrt(ss), 1e-12)

    ka = ka_ref[...]
    k_f = k * (1.0 + (a[:, :d_a] - 1.0) * ka)
    k_b = k * (1.0 + (a[:, d_a:] - 1.0) * ka)
    b_f = kk * a[:, :d_a]
    b_b = kk * a[:, d_a:]
    rk = rk_ref[...]
    bon = _mm_exact_rhs(r * (k_f + k_b) * rk, ones) * v

    bonus_o[...] = bon.astype(bonus_o.dtype)
    g_o[...] = g.astype(g_o.dtype)
    for pi in range(d_a // PAIR):
        sl = slice(pi * PAIR, (pi + 1) * PAIR)
        r_o[pi] = r[:, sl].astype(r_o.dtype)
        v_o[pi] = v[:, sl].astype(v_o.dtype)
        kk_o[pi] = kk[:, sl].astype(kk_o.dtype)
        lwf_o[pi] = logw[:, sl]
        lwb_o[pi] = logw[:, d_a + pi * PAIR:d_a + (pi + 1) * PAIR]
        kf_o[pi] = k_f[:, sl].astype(kf_o.dtype)
        kb_o[pi] = k_b[:, sl].astype(kb_o.dtype)
        bf_o[pi] = b_f[:, sl].astype(bf_o.dtype)
        bb_o[pi] = b_b[:, sl].astype(bb_o.dtype)


def _prep(pa3, mu_prev, mu_next, w0c, w2blk, a0c, a2blk, g2, k_k, k_a, r_k, d_a, tt=256):
    b, t, ac = pa3.shape
    n_t = t // tt
    n_p = d_a // PAIR
    ones = _block_ones(d_a, HEAD_DIM)
    c2 = lambda bi, i: (0, 0)
    c3 = lambda bi, i: (0, 0, 0)
    pair_spec = pl.BlockSpec((None, n_p, tt, PAIR), lambda bi, i: (bi, 0, i, 0))
    pair_act = jax.ShapeDtypeStruct((b, n_p, t, PAIR), ACT)
    pair_f32 = jax.ShapeDtypeStruct((b, n_p, t, PAIR), F32)
    flat_spec = pl.BlockSpec((None, tt, d_a), lambda bi, i: (bi, i, 0))
    flat_shape = jax.ShapeDtypeStruct((b, t, d_a), ACT)
    halo = 16
    r8 = tt // halo
    return pl.pallas_call(
        functools.partial(_prep_kernel, d_a=d_a),
        grid=(b, n_t),
        in_specs=[
            pl.BlockSpec((None, tt, ac), lambda bi, i: (bi, i, 0)),
            pl.BlockSpec((None, halo, ac), lambda bi, i: (bi, jnp.maximum(i * r8 - 1, 0), 0)),
            pl.BlockSpec((None, halo, ac), lambda bi, i: (bi, jnp.minimum((i + 1) * r8, t // halo - 1), 0)),
            pl.BlockSpec((1, ac), c2),
            pl.BlockSpec((1, ac), c2),
            pl.BlockSpec((1, 2 * d_a), c2),
            pl.BlockSpec((2, 2 * W_LORA, 2 * d_a), c3),
            pl.BlockSpec((1, 2 * d_a), c2),
            pl.BlockSpec((2, 2 * A_LORA, 2 * d_a), c3),
            pl.BlockSpec((2, G_LORA, d_a), c3),
            pl.BlockSpec((1, d_a), c2),
            pl.BlockSpec((1, d_a), c2),
            pl.BlockSpec((1, d_a), c2),
            pl.BlockSpec((d_a, d_a), c2),
        ],
        out_specs=[pair_spec] * 9 + [flat_spec] * 2,
        out_shape=[pair_act] * 3 + [pair_f32] * 2 + [pair_act] * 4 + [flat_shape] * 2,
        compiler_params=_params(("arbitrary", "arbitrary")),
        name="rwkv_prep",
    )(pa3, pa3, pa3, mu_prev, mu_next, w0c, _hilo(w2blk), a0c, _hilo(a2blk), _hilo(g2), k_k, k_a, r_k, ones)


def _mm_exact_rhs_left(tri_bf16, x):
    hi, lo = _split(x)
    return _dot(tri_bf16, hi) + _dot(tri_bf16, lo)


def _scan_stage(items, s_refs):
    c = items[0][0].shape[0]
    ri = lax.broadcasted_iota(jnp.int32, (c, c), 0)
    ci = lax.broadcasted_iota(jnp.int32, (c, c), 1)
    lane = lax.broadcasted_iota(jnp.int32, (1, PAIR), 1)
    m0 = (lane < HEAD_DIM).astype(F32)
    m1 = 1.0 - m0
    eye = (ri == ci).astype(F32)
    hi_ = lax.broadcasted_iota(jnp.int32, (PAIR, PAIR), 0) // HEAD_DIM
    hj_ = lax.broadcasted_iota(jnp.int32, (PAIR, PAIR), 1) // HEAD_DIM
    same_head = hi_ == hj_
    zero = jnp.zeros((), F32)
    n = len(items)

    def masks(reverse):
        if reverse:
            return ci >= ri, ci > ri, 0
        return ci <= ri, ci < ri, c - 1

    cums = [_mm_exact_rhs_left(masks(it[6])[0].astype(BF16), it[3]) for it in items]

    pre = []
    for (r, kk, v, lw, k, b, reverse), cum in zip(items, cums):
        end = masks(reverse)[2]
        cmid = cum[c // 2:c // 2 + 1, :]
        cend = cum[end:end + 1, :]
        r_abs = r * jnp.exp(cum)
        a_abs = -kk * jnp.exp(cum - lw)
        to_mid = jnp.exp(-cmid)
        from_mid = jnp.exp(cmid - cum)
        to_end = jnp.exp(cend - cum)
        pre.append(dict(
            r_abs=r_abs, a_abs=a_abs, r_rel=r_abs * to_mid, a_rel=a_abs * to_mid,
            k_rel=k * from_mid, b_rel=b * from_mid, k_end=k * to_end, b_end=b * to_end,
            d_tot=jnp.exp(cend), v=v))

    grams = []
    for p in pre:
        lhs = jnp.concatenate([p["r_rel"] * m0, p["r_rel"] * m1, p["a_rel"] * m0, p["a_rel"] * m1],
                              axis=0).astype(BF16)
        rhs = jnp.concatenate([p["k_rel"], p["b_rel"]], axis=0).astype(BF16)
        grams.append(_dot_nt(lhs, rhs))

    pws, ts = [], []
    for it, gram in zip(items, grams):
        strict = masks(it[6])[1]
        for e in range(2):
            a_ab = jnp.where(strict, gram[(2 + e) * c:(3 + e) * c, c:2 * c], zero)
            pws.append(a_ab)
            ts.append(eye + a_ab)
    for _ in range(int(round(math.log2(c))) - 1):
        pws = [_dot(pw.astype(BF16), pw.astype(BF16)) for pw in pws]
        ts = [t + _dot(t.astype(BF16), pw.astype(BF16)) for t, pw in zip(ts, pws)]

    v_blks = [jnp.concatenate([p["v"] * m0, p["v"] * m1], axis=0).astype(BF16) for p in pre]
    akvs, o_rks, rb_cats = [], [], []
    for it, gram, v_blk in zip(items, grams, v_blks):
        incl, strict, _ = masks(it[6])
        ak_cat = jnp.concatenate(
            [jnp.where(strict, gram[(2 + e) * c:(3 + e) * c, 0:c], zero) for e in range(2)], axis=1)
        rk_cat = jnp.concatenate(
            [jnp.where(incl, gram[e * c:(e + 1) * c, 0:c], zero) for e in range(2)], axis=1)
        rb_cats.append(jnp.concatenate(
            [jnp.where(incl, gram[e * c:(e + 1) * c, c:2 * c], zero) for e in range(2)], axis=1).astype(BF16))
        akvs.append(_dot(ak_cat.astype(BF16), v_blk))
        o_rks.append(_dot(rk_cat.astype(BF16), v_blk))

    xs = []
    for i, (p, akv) in enumerate(zip(pre, akvs)):
        t_cat = jnp.concatenate([ts[2 * i], ts[2 * i + 1]], axis=1).astype(BF16)
        y_blk = jnp.concatenate([
            jnp.concatenate([p["a_abs"] * m0, akv * m0], axis=1),
            jnp.concatenate([p["a_abs"] * m1, akv * m1], axis=1)], axis=0).astype(BF16)
        xs.append(_dot(t_cat, y_blk))

    s0s = [s_ref[...] for s_ref in s_refs]
    s0bs = [s0.astype(BF16) for s0 in s0s]
    us = [_dot(x[:, 0:PAIR].astype(BF16), s0b) + x[:, PAIR:2 * PAIR] for x, s0b in zip(xs, s0bs)]
    outs = []
    for i in range(n):
        p, u = pre[i], us[i]
        u_blk = jnp.concatenate([u * m0, u * m1], axis=0).astype(BF16)
        outs.append(_dot(p["r_abs"].astype(BF16), s0bs[i]) + _dot(rb_cats[i], u_blk) + o_rks[i])
        kb_t = jnp.concatenate([p["b_end"].T, p["k_end"].T], axis=1).astype(BF16)
        uv = jnp.concatenate([u, p["v"]], axis=0).astype(BF16)
        d_col = jnp.broadcast_to(p["d_tot"], (PAIR, PAIR)).T
        s_refs[i][...] = jnp.where(same_head, d_col * s0s[i] + _dot(kb_t, uv), zero)
    return outs


def _scan_kernel(rf, vf, kkf, lwf, kf, bf, rb, vb, kkb, lwb, kb, bb, of_ref, ob_ref, s_ref):
    @pl.when(pl.program_id(1) == 0)
    def _():
        s_ref[...] = jnp.zeros_like(s_ref)

    n_p = rf.shape[0]
    items, s_refs = [], []
    for p in range(n_p):
        ld = lambda ref: ref[p].astype(F32)
        items.append((ld(rf), ld(kkf), ld(vf), lwf[p], ld(kf), ld(bf), False))
        s_refs.append(s_ref.at[2 * p])
        items.append((ld(rb), ld(kkb), ld(vb), lwb[p], ld(kb), ld(bb), True))
        s_refs.append(s_ref.at[2 * p + 1])
    outs = _scan_stage(items, s_refs)
    for p in range(n_p):
        of_ref[p] = outs[2 * p].astype(of_ref.dtype)
        ob_ref[p] = outs[2 * p + 1].astype(ob_ref.dtype)


def _scan(r, v, kk, lw_f, lw_b, k_f, k_b, b_f, b_b):
    bsz, n_p, t, _ = r.shape
    nc = t // CHUNK
    fwd = pl.BlockSpec((None, n_p, CHUNK, PAIR), lambda bi, c: (bi, 0, c, 0))
    bwd = pl.BlockSpec((None, n_p, CHUNK, PAIR), lambda bi, c: (bi, 0, nc - 1 - c, 0))
    shape = jax.ShapeDtypeStruct((bsz, n_p, t, PAIR), ACT)
    return pl.pallas_call(
        _scan_kernel,
        grid=(bsz, nc),
        in_specs=[fwd] * 6 + [bwd] * 6,
        out_specs=[fwd, bwd],
        out_shape=[shape, shape],
        scratch_shapes=[pltpu.VMEM((2 * n_p, PAIR, PAIR), F32)],
        compiler_params=_params(("arbitrary", "arbitrary")),
        name="rwkv_scan",
    )(r, v, kk, lw_f, k_f, b_f, r, v, kk, lw_b, k_b, b_b)


def _na_kernel(q_ref, k_ref, v_ref, gq_ref, gk_ref, tab_ref, ones_ref, o_ref, qn_s, kn_s, vb_s, *, rows):
    ones = ones_ref[...]
    scale = HEAD_DIM ** -0.5
    q = q_ref[...].astype(F32)
    k = k_ref[...].astype(F32)
    inv_d = 1.0 / HEAD_DIM
    qn = q * lax.rsqrt(_mm_exact_rhs(q * q, ones) * inv_d + RMS_EPS) * (gq_ref[...] * scale)
    kn = k * lax.rsqrt(_mm_exact_rhs(k * k, ones) * inv_d + RMS_EPS) * gk_ref[...]
    qn_s[...] = qn.astype(BF16)
    kn_s[...] = kn.astype(BF16)
    vb_s[...] = v_ref[...].astype(BF16)
    lane = lax.broadcasted_iota(jnp.int32, (1, PAIR), 1)
    head0 = lane < HEAD_DIM
    win = NA_WR * GRID_W

    def row_group(gi, carry):
        rws = [gi * NA_ROWS_PER_STEP + j for j in range(NA_ROWS_PER_STEP)]
        rss = [jnp.clip(r - NA_WR // 2, 0, rows - NA_WR) for r in rws]
        q_rows = [qn_s[pl.ds(pl.multiple_of(r * GRID_W, GRID_W), GRID_W), :] for r in rws]
        k_wins = [kn_s[pl.ds(pl.multiple_of(rs * GRID_W, GRID_W), win), :] for rs in rss]
        v_wins = [vb_s[pl.ds(pl.multiple_of(rs * GRID_W, GRID_W), win), :] for rs in rss]
        ss = []
        for j in range(NA_ROWS_PER_STEP):
            d0 = rss[j] - rws[j] + NA_WR - 1
            for e in range(2):
                mask = head0 if e == 0 else jnp.logical_not(head0)
                qm = jnp.where(mask, q_rows[j], jnp.zeros_like(q_rows[j]))
                bias = jnp.concatenate(
                    [tab_ref[e, pl.ds(d0 + 2 * m, 1)][0] for m in range(NA_WR // 2)], axis=1)
                ss.append(_dot_nt(qm, k_wins[j]) + bias)
        mxs = [jnp.max(s, axis=-1, keepdims=True) for s in ss]
        ps = [jnp.exp(s - mx) for s, mx in zip(ss, mxs)]
        ls = [jnp.sum(p, axis=-1, keepdims=True) for p in ps]
        pvs = [_dot(p.astype(BF16), v_wins[i // 2]) for i, p in enumerate(ps)]
        for j in range(NA_ROWS_PER_STEP):
            o0 = pvs[2 * j] / ls[2 * j]
            o1 = pvs[2 * j + 1] / ls[2 * j + 1]
            o_ref[pl.ds(pl.multiple_of(rws[j] * GRID_W, GRID_W), GRID_W), :] = (
                jnp.where(head0, o0, o1).astype(o_ref.dtype))
        return carry

    lax.fori_loop(0, rows // NA_ROWS_PER_STEP, row_group, 0)


def _na_bias_table(rpb):
    qc = np.arange(GRID_W)
    kc = np.arange(GRID_W)
    cs = np.clip(qc - NA_WC // 2, 0, GRID_W - NA_WC)
    valid = (kc[None, :] >= cs[:, None]) & (kc[None, :] < cs[:, None] + NA_WC)
    dc = np.clip(kc[None, :] - qc[:, None] + NA_WC - 1, 0, 2 * NA_WC - 2)
    b = rpb.astype(F32)[:, :, dc]
    b = jnp.where(jnp.asarray(valid)[None, None], b, NEG_BIG)
    return jnp.concatenate([b[:, :-1], b[:, 1:]], axis=-1)


def _na(pb3, q_gain, k_gain, table, d_b):
    bsz, t, _ = pb3.shape
    rows = t // GRID_W
    n_p = d_b // PAIR
    ones = _block_ones(PAIR, HEAD_DIM)
    gq = jnp.tile(q_gain.reshape(1, HEAD_DIM), (1, 2))
    gk = jnp.tile(k_gain.reshape(1, HEAD_DIM), (1, 2))
    n_d = table.shape[1]
    c2 = lambda bi, p: (0, 0)
    return pl.pallas_call(
        functools.partial(_na_kernel, rows=rows),
        grid=(bsz, n_p),
        in_specs=[
            pl.BlockSpec((None, t, PAIR), lambda bi, p: (bi, 0, p)),
            pl.BlockSpec((None, t, PAIR), lambda bi, p: (bi, 0, n_p + p)),
            pl.BlockSpec((None, t, PAIR), lambda bi, p: (bi, 0, 2 * n_p + p)),
            pl.BlockSpec((1, PAIR), c2),
            pl.BlockSpec((1, PAIR), c2),
            pl.BlockSpec((2, n_d, GRID_W, PAIR), lambda bi, p: (p, 0, 0, 0)),
            pl.BlockSpec((PAIR, PAIR), c2),
        ],
        out_specs=pl.BlockSpec((None, t, PAIR), lambda bi, p: (bi, 0, p)),
        out_shape=jax.ShapeDtypeStruct((bsz, t, d_b), ACT),
        scratch_shapes=[pltpu.VMEM((t, PAIR), BF16)] * 3,
        compiler_params=_params(("arbitrary", "arbitrary")),
        name="natten",
    )(pb3, pb3, pb3, gq, gk, table, ones)


def _merge_kernel(of_ref, ob_ref, bonus_ref, g_ref, yb_ref, gates_ref, x_ref,
                  lng_ref, lnb_ref, wa_ref, wb_ref, wo_ref, gffn_ref, wr_ref, br_ref,
                  ones_ref, tri_ref,
                  x1_ref, h2_ref, idx_ref, rank_ref, gw_ref, cnt_ref, carry_ref, *, d_model):
    first = jnp.logical_and(pl.program_id(0) == 0, pl.program_id(1) == 0)

    @pl.when(first)
    def _():
        carry_ref[...] = jnp.zeros_like(carry_ref)

    n_p = of_ref.shape[0]
    o = jnp.concatenate([of_ref[p].astype(F32) + ob_ref[p].astype(F32) for p in range(n_p)],
                        axis=1)
    ones = ones_ref[...]
    inv_d = 1.0 / HEAD_DIM
    mu = _mm_exact_rhs(o, ones) * inv_d
    dv = o - mu
    var = _mm_exact_rhs(dv * dv, ones) * inv_d
    y = dv * lax.rsqrt(var + GN_EPS) * lng_ref[...] + lnb_ref[...] + bonus_ref[...].astype(F32)
    ya = y * g_ref[...].astype(F32)

    gates = gates_ref[...].astype(F32)
    pa = _dot(ya.astype(BF16), wa_ref[...])
    pb = _dot(yb_ref[...].astype(BF16), wb_ref[...])
    merged = _sigmoid(gates[:, :d_model]) * pa + _sigmoid(gates[:, d_model:]) * pb
    x1 = x_ref[...] + _dot(merged.astype(BF16), wo_ref[...])
    x1_ref[...] = x1
    ms = jnp.mean(x1 * x1, axis=-1, keepdims=True)
    h2 = x1 * lax.rsqrt(ms + RMS_EPS) * gffn_ref[...]
    _store_rows_as_tiles(h2_ref, h2)

    logits = _mm3(h2, wr_ref) + br_ref[...]
    tm = logits.shape[0]
    lane = lax.broadcasted_iota(jnp.int32, (tm, LANES), 1)
    work = logits
    vals, idxs = [], []
    for _ in range(TOP_K):
        m = jnp.max(work, axis=-1, keepdims=True)
        ix = jnp.min(jnp.where(work == m, lane, LANES), axis=-1, keepdims=True)
        vals.append(m)
        idxs.append(ix)
        work = jnp.where(lane == ix, -jnp.inf, work)
    es = [jnp.exp(vk - vals[0]) for vk in vals]
    den = es[0] + es[1] + es[2] + es[3]
    member = jnp.zeros((tm, LANES), F32)
    for ix in idxs:
        member = member + (lane == ix).astype(F32)
    before = _dot(tri_ref[...], member.astype(BF16)) + carry_ref[...]
    idx_out = jnp.zeros((tm, LANES), jnp.int32)
    rank_out = jnp.zeros((tm, LANES), jnp.int32)
    gw_out = jnp.zeros((tm, LANES), F32)
    for kq in range(TOP_K):
        rk = jnp.sum(jnp.where(lane == idxs[kq], before, 0.0), axis=-1, keepdims=True)
        sel = lane == kq
        idx_out = jnp.where(sel, idxs[kq], idx_out)
        rank_out = jnp.where(sel, rk.astype(jnp.int32), rank_out)
        gw_out = jnp.where(sel, es[kq] / den, gw_out)
    idx_ref[...] = idx_out
    rank_ref[...] = rank_out
    gw_ref[...] = gw_out
    carry_ref[...] = carry_ref[...] + jnp.sum(member, axis=0, keepdims=True)
    cnt_ref[...] = carry_ref[...]


def _merge(o_f, o_b, bonus, g, yb, gates3, x3, lnx_g, lnx_b, w_a, w_b, w_o, g_ffn, wr_pad, br_pad, tm=512):
    bsz, n_p, t, _ = o_f.shape
    d_a = n_p * PAIR
    d_b = yb.shape[-1]
    d_model = x3.shape[-1]
    n_t = t // tm
    m = bsz * t
    ones = _block_ones(d_a, HEAD_DIM)
    tri = jnp.asarray(np.tril(np.ones((tm, tm)), -1), BF16)
    c2 = lambda bi, i: (0, 0)
    tok = lambda w: pl.BlockSpec((None, tm, w), lambda bi, i: (bi, i, 0))
    flat = lambda w: pl.BlockSpec((tm, w), lambda bi, i: (bi * n_t + i, 0))
    pair = pl.BlockSpec((None, n_p, tm, PAIR), lambda bi, i: (bi, 0, i, 0))
    return pl.pallas_call(
        functools.partial(_merge_kernel, d_model=d_model),
        grid=(bsz, n_t),
        in_specs=[
            pair, pair, tok(d_a), tok(d_a), tok(d_b), tok(2 * d_model), tok(d_model),
            pl.BlockSpec((1, d_a), c2), pl.BlockSpec((1, d_a), c2),
            pl.BlockSpec((d_a, d_model), c2), pl.BlockSpec((d_b, d_model), c2),
            pl.BlockSpec((d_model, d_model), c2), pl.BlockSpec((1, d_model), c2),
            pl.BlockSpec((2, d_model, LANES), lambda bi, i: (0, 0, 0)), pl.BlockSpec((1, LANES), c2),
            pl.BlockSpec((d_a, d_a), c2), pl.BlockSpec((tm, tm), c2),
        ],
        out_specs=[flat(d_model),
                   pl.BlockSpec((tm * SUBLANES, LANES), lambda bi, i: (bi * n_t + i, 0)),
                   flat(LANES), flat(LANES), flat(LANES),
                   pl.BlockSpec((1, LANES), c2)],
        out_shape=[
            jax.ShapeDtypeStruct((m, d_model), F32),
            jax.ShapeDtypeStruct((m * SUBLANES, LANES), F32),
            jax.ShapeDtypeStruct((m, LANES), jnp.int32),
            jax.ShapeDtypeStruct((m, LANES), jnp.int32),
            jax.ShapeDtypeStruct((m, LANES), F32),
            jax.ShapeDtypeStruct((1, LANES), F32),
        ],
        scratch_shapes=[pltpu.VMEM((1, LANES), F32)],
        compiler_params=_params(("arbitrary", "arbitrary")),
        name="merge_router",
    )(o_f, o_b, bonus, g, yb, gates3, x3, lnx_g, lnx_b, w_a, w_b, w_o, g_ffn, _hilo(wr_pad), br_pad, ones, tri)


def _expert_kernel(be_ref, nu_ref, tok_ref, tnext_ref, h_ref, w1_ref, b1_ref, w2_ref, b2_ref, ys_ref,
                   xbuf, w1b_ref, w2b_ref, sems, *, d_e):
    i = pl.program_id(0)
    n_used = nu_ref[0]
    slot = i % 2
    new_expert = jnp.logical_or(i == 0, be_ref[i] != be_ref[jnp.maximum(i - 1, 0)])

    def row_copy(t_ref, sl, j):
        return pltpu.make_async_copy(
            h_ref.at[t_ref[0, j]], xbuf.at[sl, pl.ds(j * SUBLANES, SUBLANES)], sems.at[sl])

    def gather(t_ref, sl):
        for j in range(MOE_BLOCK):
            row_copy(t_ref, sl, j).start()

    def drain(sl):
        for j in range(MOE_BLOCK):
            row_copy(tok_ref, sl, j).wait()

    @pl.when(i == 0)
    def _():
        gather(tok_ref, 0)

    @pl.when(new_expert)
    def _():
        w1b_ref[...] = w1_ref[...].astype(BF16)
        w2b_ref[...] = w2_ref[...].astype(BF16)

    @pl.when(i < n_used)
    def _():
        drain(slot)
        x = _load_tiles_as_rows(xbuf.at[slot], MOE_BLOCK).astype(BF16)
        gather(tnext_ref, 1 - slot)
        u = _dot(x, w1b_ref[...]) + b1_ref[...]
        glu = jnp.minimum(u[:, :d_e], SWIGLU_LIMIT)
        lin = jnp.clip(u[:, d_e:], -SWIGLU_LIMIT, SWIGLU_LIMIT)
        act = glu * _sigmoid(SWIGLU_ALPHA * glu) * (lin + 1.0)
        _store_rows_as_tiles(ys_ref, _dot(act.astype(BF16), w2b_ref[...]) + b2_ref[...])

    @pl.when(i == n_used)
    def _():
        drain(slot)

    @pl.when(i >= n_used)
    def _():
        ys_ref[...] = jnp.zeros_like(ys_ref)


def _experts(blk_e, n_used, tok_list, h2_tiles, w1, b1, w2, b2):
    nb = tok_list.shape[0]
    n_pad = nb * MOE_BLOCK
    d = SUBLANES * LANES
    n_e, _, d2 = w1.shape
    d_e = d2 // 2
    nxt = lambda i, be, nu: (jnp.minimum(i + 1, jnp.maximum(nu[0] - 1, 0)), 0, 0)
    grid_spec = pltpu.PrefetchScalarGridSpec(
        num_scalar_prefetch=2,
        grid=(nb,),
        in_specs=[
            pl.BlockSpec((None, 1, MOE_BLOCK), lambda i, be, nu: (i, 0, 0), memory_space=pltpu.SMEM),
            pl.BlockSpec((None, 1, MOE_BLOCK), nxt, memory_space=pltpu.SMEM),
            pl.BlockSpec(memory_space=pl.ANY),
            pl.BlockSpec((None, d, d2), lambda i, be, nu: (be[i], 0, 0)),
            pl.BlockSpec((None, 1, d2), lambda i, be, nu: (be[i], 0, 0)),
            pl.BlockSpec((None, d_e, d), lambda i, be, nu: (be[i], 0, 0)),
            pl.BlockSpec((None, 1, d), lambda i, be, nu: (be[i], 0, 0)),
        ],
        out_specs=pl.BlockSpec((MOE_BLOCK * SUBLANES, LANES), lambda i, be, nu: (i, 0)),
        scratch_shapes=[pltpu.VMEM((2, MOE_BLOCK * SUBLANES, LANES), F32),
                        pltpu.VMEM((d, d2), BF16), pltpu.VMEM((d_e, d), BF16),
                        pltpu.SemaphoreType.DMA((2,))],
    )
    return pl.pallas_call(
        functools.partial(_expert_kernel, d_e=d_e),
        grid_spec=grid_spec,
        out_shape=jax.ShapeDtypeStruct((n_pad * SUBLANES, LANES), F32),
        compiler_params=_params(("arbitrary",)),
        name="moe_experts",
    )(blk_e, n_used, tok_list, tok_list, h2_tiles, w1, b1.reshape(n_e, 1, d2), w2, b2.reshape(n_e, 1, d))


def _combine_kernel(dest_ref, dnext_ref, ys_ref, x1_ref, gw_ref, o_ref, buf, sems):
    i = pl.program_id(0)
    nt = pl.num_programs(0)
    tm = x1_ref.shape[0]
    slot = i % 2

    def row_copy(d_ref, sl, t, kq):
        return pltpu.make_async_copy(
            ys_ref.at[d_ref[0, t * TOP_K + kq]],
            buf.at[sl, kq, pl.ds(pl.multiple_of(t * SUBLANES, SUBLANES), SUBLANES)], sems.at[sl])

    def issue_all(d_ref, sl):
        def body(t, c):
            for kq in range(TOP_K):
                row_copy(d_ref, sl, t, kq).start()
            return c
        lax.fori_loop(0, tm, body, 0, unroll=8)

    @pl.when(i == 0)
    def _():
        issue_all(dest_ref, 0)

    @pl.when(i + 1 < nt)
    def _():
        issue_all(dnext_ref, 1 - slot)

    def drain(t, c):
        for kq in range(TOP_K):
            row_copy(dest_ref, slot, t, kq).wait()
        return c

    lax.fori_loop(0, tm, drain, 0, unroll=8)
    gw = gw_ref[...]
    for s in range(SUBLANES):
        acc = x1_ref[:, s * LANES:(s + 1) * LANES]
        for kq in range(TOP_K):
            acc = acc + gw[:, kq:kq + 1] * buf[slot, kq, pl.ds(s, tm, stride=SUBLANES), :]
        o_ref[:, s * LANES:(s + 1) * LANES] = acc


def _combine(dest, ys, x1, gw, tm=128):
    m, d = x1.shape
    nt = m // tm
    dest3 = dest.reshape(nt, 1, tm * TOP_K)
    return pl.pallas_call(
        _combine_kernel,
        grid=(nt,),
        in_specs=[
            pl.BlockSpec((None, 1, tm * TOP_K), lambda i: (i, 0, 0), memory_space=pltpu.SMEM),
            pl.BlockSpec((None, 1, tm * TOP_K), lambda i: (jnp.minimum(i + 1, nt - 1), 0, 0),
                         memory_space=pltpu.SMEM),
            pl.BlockSpec(memory_space=pl.ANY),
            pl.BlockSpec((tm, d), lambda i: (i, 0)),
            pl.BlockSpec((tm, LANES), lambda i: (i, 0)),
        ],
        out_specs=pl.BlockSpec((tm, d), lambda i: (i, 0)),
        out_shape=jax.ShapeDtypeStruct((m, d), F32),
        scratch_shapes=[pltpu.VMEM((2, TOP_K, tm * SUBLANES, LANES), F32), pltpu.SemaphoreType.DMA((2,))],
        compiler_params=_params(("arbitrary",)),
        name="moe_combine",
    )(dest3, dest3, ys, x1, gw)


def _blockdiag2(a, b):
    za = jnp.zeros((a.shape[0], b.shape[1]), a.dtype)
    zb = jnp.zeros((b.shape[0], a.shape[1]), a.dtype)
    return jnp.concatenate([jnp.concatenate([a, za], axis=1), jnp.concatenate([zb, b], axis=1)], axis=0)


def _layer(x, g_mix, w_in, mu_prev, mu_next, w0_f, w2_f, w0_b, w2_b, a0_f, a2_f, a0_b, a2_b,
           g2, k_k, k_a, r_k, lnx_g, lnx_b, q_norm_g, k_norm_g, rpb, w_a, w_b, w_o,
           g_ffn, w_router, b_router, w1, b1, w2, b2):
    bsz, t, d_model = x.shape
    m = bsz * t
    d_a = w_a.shape[0]
    d_b = w_b.shape[0]
    a_cols = mu_prev.shape[0]
    b_cols = 3 * d_b
    row = lambda a: a.reshape(1, -1).astype(F32)

    w_in_b = w_in.astype(BF16)
    pa, pb, gates = _inproj(x.reshape(m, d_model), row(g_mix), w_in_b[:, :a_cols],
                            w_in_b[:, a_cols:a_cols + b_cols], w_in_b[:, a_cols + b_cols:])

    prep = _prep(pa.reshape(bsz, t, a_cols), row(mu_prev), row(mu_next),
                 jnp.concatenate([row(w0_f), row(w0_b)], axis=1), _blockdiag2(w2_f, w2_b),
                 jnp.concatenate([row(a0_f), row(a0_b)], axis=1), _blockdiag2(a2_f, a2_b),
                 g2, row(k_k), row(k_a), row(r_k), d_a)
    r, v, kk, lw_f, lw_b, k_f, k_b, b_f, b_b, bonus, g = prep
    o_f, o_b = _scan(r, v, kk, lw_f, lw_b, k_f, k_b, b_f, b_b)

    yb = _na(pb.reshape(bsz, t, b_cols), q_norm_g, k_norm_g, _na_bias_table(rpb), d_b)

    n_e = w_router.shape[1]
    wr_pad = jnp.zeros((d_model, LANES), F32).at[:, :n_e].set(w_router)
    br_pad = jnp.full((1, LANES), NEG_BIG, F32).at[0, :n_e].set(b_router)
    x1, h2, idx, rank, gw, cnt = _merge(
        o_f, o_b, bonus, g, yb, gates.reshape(bsz, t, 2 * d_model), x,
        row(lnx_g), row(lnx_b), w_a.astype(BF16), w_b.astype(BF16), w_o.astype(BF16),
        row(g_ffn), wr_pad, br_pad)

    counts = cnt[0, :n_e].astype(jnp.int32)
    padded = ((counts + MOE_BLOCK - 1) // MOE_BLOCK) * MOE_BLOCK
    pend = jnp.cumsum(padded)
    pstart = pend - padded
    n_assign = m * TOP_K
    n_blocks = -(-n_assign // MOE_BLOCK) + n_e
    n_pad = n_blocks * MOE_BLOCK
    top_idx = idx[:, :TOP_K]
    dest = (pstart[top_idx] + rank[:, :TOP_K]).reshape(-1).astype(jnp.int32)
    n_used = (pend[-1] // MOE_BLOCK).astype(jnp.int32)
    blk_idx = jnp.minimum(jnp.arange(n_blocks, dtype=jnp.int32), n_used - 1)
    blk_e = jnp.sum((blk_idx[:, None] * MOE_BLOCK >= pend[None, :]).astype(jnp.int32), axis=1)
    blk_e = jnp.minimum(blk_e, n_e - 1)

    tok_list = jnp.zeros((n_pad,), jnp.int32).at[dest].set(
        jnp.arange(n_assign, dtype=jnp.int32) // TOP_K).reshape(n_blocks, 1, MOE_BLOCK)
    ys = _experts(blk_e, n_used.reshape(1), tok_list, h2.reshape(m, SUBLANES, LANES), w1, b1, w2, b2)
    out = _combine(dest, ys.reshape(n_pad, SUBLANES, LANES), x1, gw)
    return out.reshape(bsz, t, d_model)


def kernel(x, g_mix, w_in, mu_prev, mu_next, w0_f, w2_f, w0_b, w2_b, a0_f, a2_f, a0_b, a2_b, g2, k_k, k_a, r_k, lnx_g, lnx_b, q_norm_g, k_norm_g, rpb, w_a, w_b, w_o, g_ffn, w_router, b_router, w1, b1, w2, b2):
    for l in range(g_mix.shape[0]):
        x = _layer(x, g_mix[l], w_in[l], mu_prev[l], mu_next[l], w0_f[l], w2_f[l], w0_b[l], w2_b[l],
                   a0_f[l], a2_f[l], a0_b[l], a2_b[l], g2[l], k_k[l], k_a[l], r_k[l], lnx_g[l], lnx_b[l],
                   q_norm_g[l], k_norm_g[l], rpb[l], w_a[l], w_b[l], w_o[l], g_ffn[l], w_router[l],
                   b_router[l], w1[l], b1[l], w2[l], b2[l])
    return x
```

```python
import functools
import math

import numpy as np
import jax
import jax.numpy as jnp
from jax import lax
from jax.experimental import pallas as pl
from jax.experimental.pallas import tpu as pltpu

F32 = jnp.float32
BF16 = jnp.bfloat16
ACT = jnp.bfloat16

LANES = 128
SUBLANES = 8
HEAD_DIM = 64
PAIR = 2 * HEAD_DIM
GRID_W = 64
NA_WR = 8
NA_WC = 16
W_LORA = 64
A_LORA = 64
G_LORA = 128
DECAY_SCALE = math.exp(-0.5)
GN_EPS = 64e-5
RMS_EPS = 1e-5
N_EXPERTS = 32
TOP_K = 4
MOE_BLOCK = 512
SWIGLU_LIMIT = 7.0
SWIGLU_ALPHA = 1.702
NEG_BIG = -1e30
CHUNK = 128
NA_ROWS_PER_STEP = 8
VMEM_LIMIT = 56 * 1024 * 1024


def _dot(a, b):
    return jnp.dot(a, b, preferred_element_type=F32)


def _dot_nt(a, b):
    return lax.dot_general(a, b, (((1,), (1,)), ((), ())), preferred_element_type=F32)


def _split(a):
    hi = a.astype(BF16)
    lo = (a - hi.astype(F32)).astype(BF16)
    return hi, lo


def _mm_exact_rhs(a, b_bf16):
    hi, lo = _split(a)
    return _dot(hi, b_bf16) + _dot(lo, b_bf16)


def _hilo(b):
    return jnp.stack(_split(b))


def _mm3(a, b_hilo):
    ah, al = _split(a)
    bh = b_hilo[0]
    return _dot(ah, bh) + _dot(al, bh) + _dot(ah, b_hilo[1])


def _store_rows_as_tiles(ref, val):
    n = val.shape[0]
    for s in range(SUBLANES):
        ref[pl.ds(s, n, stride=SUBLANES), :] = val[:, s * LANES:(s + 1) * LANES]


def _load_tiles_as_rows(ref, n):
    return jnp.concatenate([ref[pl.ds(s, n, stride=SUBLANES), :] for s in range(SUBLANES)], axis=1)


def _sigmoid(x):
    return 1.0 / (1.0 + jnp.exp(-x))


def _params(sem):
    return pltpu.CompilerParams(dimension_semantics=sem, vmem_limit_bytes=VMEM_LIMIT)


def _block_ones(n, blk):
    i = np.arange(n) // blk
    return jnp.asarray(i[:, None] == i[None, :], BF16)


def _inproj_kernel(x_ref, g_ref, wa_ref, wb_ref, wg_ref, pa_ref, pb_ref, pg_ref):
    x = x_ref[...]
    ms = jnp.mean(x * x, axis=-1, keepdims=True)
    h = (x * lax.rsqrt(ms + RMS_EPS) * g_ref[...]).astype(BF16)
    pa_ref[...] = _dot(h, wa_ref[...]).astype(pa_ref.dtype)
    pb_ref[...] = _dot(h, wb_ref[...]).astype(pb_ref.dtype)
    pg_ref[...] = _dot(h, wg_ref[...]).astype(pg_ref.dtype)


def _inproj(x2, g_mix, w_a, w_b, w_g, tm=256):
    m, d = x2.shape
    na, nb, ng = w_a.shape[1], w_b.shape[1], w_g.shape[1]
    full = lambda i: (0, 0)
    return pl.pallas_call(
        _inproj_kernel,
        grid=(m // tm,),
        in_specs=[
            pl.BlockSpec((tm, d), lambda i: (i, 0)),
            pl.BlockSpec((1, d), full),
            pl.BlockSpec((d, na), full),
            pl.BlockSpec((d, nb), full),
            pl.BlockSpec((d, ng), full),
        ],
        out_specs=[
            pl.BlockSpec((tm, na), lambda i: (i, 0)),
            pl.BlockSpec((tm, nb), lambda i: (i, 0)),
            pl.BlockSpec((tm, ng), lambda i: (i, 0)),
        ],
        out_shape=[
            jax.ShapeDtypeStruct((m, na), ACT),
            jax.ShapeDtypeStruct((m, nb), ACT),
            jax.ShapeDtypeStruct((m, ng), ACT),
        ],
        compiler_params=_params(("arbitrary",)),
        name="inproj",
    )(x2, g_mix, w_a, w_b, w_g)


def _prep_kernel(p_ref, prev_ref, next_ref, mup_ref, mun_ref, w0_ref, w2_ref, a0_ref, a2_ref,
                 g2_ref, kk_ref, ka_ref, rk_ref, ones_ref,
                 r_o, v_o, kk_o, lwf_o, lwb_o, kf_o, kb_o, bf_o, bb_o, bonus_o, g_o, *, d_a):
    i = pl.program_id(1)
    n_t = pl.num_programs(1)
    p = p_ref[...].astype(F32)
    tt = p.shape[0]
    halo = prev_ref.shape[0]
    prow = jnp.where(i > 0, prev_ref[halo - 1:halo, :].astype(F32), 0.0)
    nrow = jnp.where(i < n_t - 1, next_ref[0:1, :].astype(F32), 0.0)
    rid = lax.broadcasted_iota(jnp.int32, (tt, 1), 0)
    prev = jnp.where(rid == 0, prow, pltpu.roll(p, 1, axis=0))
    nxt = jnp.where(rid == tt - 1, nrow, pltpu.roll(p, tt - 1, axis=0))
    xa = p + mup_ref[...] * (prev - p) + mun_ref[...] * (nxt - p)

    r = xa[:, 0:d_a]
    k = xa[:, d_a:2 * d_a]
    v = xa[:, 2 * d_a:3 * d_a]
    o = 3 * d_a
    lw = xa[:, o:o + 2 * W_LORA]
    la = xa[:, o + 2 * W_LORA:o + 2 * W_LORA + 2 * A_LORA]
    lg = xa[:, o + 2 * W_LORA + 2 * A_LORA:]

    dpre = w0_ref[...] + _mm3(jnp.tanh(lw), w2_ref)
    apre = a0_ref[...] + _mm3(la, a2_ref)
    g = _mm3(_sigmoid(lg), g2_ref)
    logw = -DECAY_SCALE * _sigmoid(dpre)
    a = _sigmoid(apre)

    ones = ones_ref[...]
    kkr = k * kk_ref[...]
    ss = _mm_exact_rhs(kkr * kkr, ones)
    kk = kkr / jnp.maximum(jnp.sqrt(ss), 1e-12)

    ka = ka_ref[...]
    k_f = k * (1.0 + (a[:, :d_a] - 1.0) * ka)
    k_b = k * (1.0 + (a[:, d_a:] - 1.0) * ka)
    b_f = kk * a[:, :d_a]
    b_b = kk * a[:, d_a:]
    rk = rk_ref[...]
    bon = _mm_exact_rhs(r * (k_f + k_b) * rk, ones) * v

    bonus_o[...] = bon.astype(bonus_o.dtype)
    g_o[...] = g.astype(g_o.dtype)
    for pi in range(d_a // PAIR):
        sl = slice(pi * PAIR, (pi + 1) * PAIR)
        r_o[pi] = r[:, sl].astype(r_o.dtype)
        v_o[pi] = v[:, sl].astype(v_o.dtype)
        kk_o[pi] = kk[:, sl].astype(kk_o.dtype)
        lwf_o[pi] = logw[:, sl]
        lwb_o[pi] = logw[:, d_a + pi * PAIR:d_a + (pi + 1) * PAIR]
        kf_o[pi] = k_f[:, sl].astype(kf_o.dtype)
        kb_o[pi] = k_b[:, sl].astype(kb_o.dtype)
        bf_o[pi] = b_f[:, sl].astype(bf_o.dtype)
        bb_o[pi] = b_b[:, sl].astype(bb_o.dtype)


def _prep(pa3, mu_prev, mu_next, w0c, w2blk, a0c, a2blk, g2, k_k, k_a, r_k, d_a, tt=256):
    b, t, ac = pa3.shape
    n_t = t // tt
    n_p = d_a // PAIR
    ones = _block_ones(d_a, HEAD_DIM)
    c2 = lambda bi, i: (0, 0)
    c3 = lambda bi, i: (0, 0, 0)
    pair_spec = pl.BlockSpec((None, n_p, tt, PAIR), lambda bi, i: (bi, 0, i, 0))
    pair_act = jax.ShapeDtypeStruct((b, n_p, t, PAIR), ACT)
    pair_f32 = jax.ShapeDtypeStruct((b, n_p, t, PAIR), F32)
    flat_spec = pl.BlockSpec((None, tt, d_a), lambda bi, i: (bi, i, 0))
    flat_shape = jax.ShapeDtypeStruct((b, t, d_a), ACT)
    halo = 16
    r8 = tt // halo
    return pl.pallas_call(
        functools.partial(_prep_kernel, d_a=d_a),
        grid=(b, n_t),
        in_specs=[
            pl.BlockSpec((None, tt, ac), lambda bi, i: (bi, i, 0)),
            pl.BlockSpec((None, halo, ac), lambda bi, i: (bi, jnp.maximum(i * r8 - 1, 0), 0)),
            pl.BlockSpec((None, halo, ac), lambda bi, i: (bi, jnp.minimum((i + 1) * r8, t // halo - 1), 0)),
            pl.BlockSpec((1, ac), c2),
            pl.BlockSpec((1, ac), c2),
            pl.BlockSpec((1, 2 * d_a), c2),
            pl.BlockSpec((2, 2 * W_LORA, 2 * d_a), c3),
            pl.BlockSpec((1, 2 * d_a), c2),
            pl.BlockSpec((2, 2 * A_LORA, 2 * d_a), c3),
            pl.BlockSpec((2, G_LORA, d_a), c3),
            pl.BlockSpec((1, d_a), c2),
            pl.BlockSpec((1, d_a), c2),
            pl.BlockSpec((1, d_a), c2),
            pl.BlockSpec((d_a, d_a), c2),
        ],
        out_specs=[pair_spec] * 9 + [flat_spec] * 2,
        out_shape=[pair_act] * 3 + [pair_f32] * 2 + [pair_act] * 4 + [flat_shape] * 2,
        compiler_params=_params(("arbitrary", "arbitrary")),
        name="rwkv_prep",
    )(pa3, pa3, pa3, mu_prev, mu_next, w0c, _hilo(w2blk), a0c, _hilo(a2blk), _hilo(g2), k_k, k_a, r_k, ones)


def _mm_exact_rhs_left(tri_bf16, x):
    hi, lo = _split(x)
    return _dot(tri_bf16, hi) + _dot(tri_bf16, lo)


def _scan_stage(items, s_refs):
    c = items[0][0].shape[0]
    ri = lax.broadcasted_iota(jnp.int32, (c, c), 0)
    ci = lax.broadcasted_iota(jnp.int32, (c, c), 1)
    lane = lax.broadcasted_iota(jnp.int32, (1, PAIR), 1)
    m0 = (lane < HEAD_DIM).astype(F32)
    m1 = 1.0 - m0
    eye = (ri == ci).astype(F32)
    hi_ = lax.broadcasted_iota(jnp.int32, (PAIR, PAIR), 0) // HEAD_DIM
    hj_ = lax.broadcasted_iota(jnp.int32, (PAIR, PAIR), 1) // HEAD_DIM
    same_head = hi_ == hj_
    zero = jnp.zeros((), F32)
    n = len(items)

    def masks(reverse):
        if reverse:
            return ci >= ri, ci > ri, 0
        return ci <= ri, ci < ri, c - 1

    cums = [_mm_exact_rhs_left(masks(it[6])[0].astype(BF16), it[3]) for it in items]

    pre = []
    for (r, kk, v, lw, k, b, reverse), cum in zip(items, cums):
        end = masks(reverse)[2]
        cmid = cum[c // 2:c // 2 + 1, :]
        cend = cum[end:end + 1, :]
        r_abs = r * jnp.exp(cum)
        a_abs = -kk * jnp.exp(cum - lw)
        to_mid = jnp.exp(-cmid)
        from_mid = jnp.exp(cmid - cum)
        to_end = jnp.exp(cend - cum)
        pre.append(dict(
            r_abs=r_abs, a_abs=a_abs, r_rel=r_abs * to_mid, a_rel=a_abs * to_mid,
            k_rel=k * from_mid, b_rel=b * from_mid, k_end=k * to_end, b_end=b * to_end,
            d_tot=jnp.exp(cend), v=v))

    grams = []
    for p in pre:
        lhs = jnp.concatenate([p["r_rel"] * m0, p["r_rel"] * m1, p["a_rel"] * m0, p["a_rel"] * m1],
                              axis=0).astype(BF16)
        rhs = jnp.concatenate([p["k_rel"], p["b_rel"]], axis=0).astype(BF16)
        grams.append(_dot_nt(lhs, rhs))

    pws, ts = [], []
    for it, gram in zip(items, grams):
        strict = masks(it[6])[1]
        for e in range(2):
            a_ab = jnp.where(strict, gram[(2 + e) * c:(3 + e) * c, c:2 * c], zero)
            pws.append(a_ab)
            ts.append(eye + a_ab)
    for _ in range(int(round(math.log2(c))) - 1):
        pws = [_dot(pw.astype(BF16), pw.astype(BF16)) for pw in pws]
        ts = [t + _dot(t.astype(BF16), pw.astype(BF16)) for t, pw in zip(ts, pws)]

    v_blks = [jnp.concatenate([p["v"] * m0, p["v"] * m1], axis=0).astype(BF16) for p in pre]
    akvs, o_rks, rb_cats = [], [], []
    for it, gram, v_blk in zip(items, grams, v_blks):
        incl, strict, _ = masks(it[6])
        ak_cat = jnp.concatenate(
            [jnp.where(strict, gram[(2 + e) * c:(3 + e) * c, 0:c], zero) for e in range(2)], axis=1)
        rk_cat = jnp.concatenate(
            [jnp.where(incl, gram[e * c:(e + 1) * c, 0:c], zero) for e in range(2)], axis=1)
        rb_cats.append(jnp.concatenate(
            [jnp.where(incl, gram[e * c:(e + 1) * c, c:2 * c], zero) for e in range(2)], axis=1).astype(BF16))
        akvs.append(_dot(ak_cat.astype(BF16), v_blk))
        o_rks.append(_dot(rk_cat.astype(BF16), v_blk))

    xs = []
    for i, (p, akv) in enumerate(zip(pre, akvs)):
        t_cat = jnp.concatenate([ts[2 * i], ts[2 * i + 1]], axis=1).astype(BF16)
        y_blk = jnp.concatenate([
            jnp.concatenate([p["a_abs"] * m0, akv * m0], axis=1),
            jnp.concatenate([p["a_abs"] * m1, akv * m1], axis=1)], axis=0).astype(BF16)
        xs.append(_dot(t_cat, y_blk))

    s0s = [s_ref[...] for s_ref in s_refs]
    s0bs = [s0.astype(BF16) for s0 in s0s]
    us = [_dot(x[:, 0:PAIR].astype(BF16), s0b) + x[:, PAIR:2 * PAIR] for x, s0b in zip(xs, s0bs)]
    outs = []
    for i in range(n):
        p, u = pre[i], us[i]
        u_blk = jnp.concatenate([u * m0, u * m1], axis=0).astype(BF16)
        outs.append(_dot(p["r_abs"].astype(BF16), s0bs[i]) + _dot(rb_cats[i], u_blk) + o_rks[i])
        kb_t = jnp.concatenate([p["b_end"].T, p["k_end"].T], axis=1).astype(BF16)
        uv = jnp.concatenate([u, p["v"]], axis=0).astype(BF16)
        d_col = jnp.broadcast_to(p["d_tot"], (PAIR, PAIR)).T
        s_refs[i][...] = jnp.where(same_head, d_col * s0s[i] + _dot(kb_t, uv), zero)
    return outs


def _scan_kernel(rf, vf, kkf, lwf, kf, bf, rb, vb, kkb, lwb, kb, bb, of_ref, ob_ref, s_ref):
    @pl.when(pl.program_id(1) == 0)
    def _():
        s_ref[...] = jnp.zeros_like(s_ref)

    n_p = rf.shape[0]
    items, s_refs = [], []
    for p in range(n_p):
        ld = lambda ref: ref[p].astype(F32)
        items.append((ld(rf), ld(kkf), ld(vf), lwf[p], ld(kf), ld(bf), False))
        s_refs.append(s_ref.at[2 * p])
        items.append((ld(rb), ld(kkb), ld(vb), lwb[p], ld(kb), ld(bb), True))
        s_refs.append(s_ref.at[2 * p + 1])
    outs = _scan_stage(items, s_refs)
    for p in range(n_p):
        of_ref[p] = outs[2 * p].astype(of_ref.dtype)
        ob_ref[p] = outs[2 * p + 1].astype(ob_ref.dtype)


def _scan(r, v, kk, lw_f, lw_b, k_f, k_b, b_f, b_b):
    bsz, n_p, t, _ = r.shape
    nc = t // CHUNK
    fwd = pl.BlockSpec((None, n_p, CHUNK, PAIR), lambda bi, c: (bi, 0, c, 0))
    bwd = pl.BlockSpec((None, n_p, CHUNK, PAIR), lambda bi, c: (bi, 0, nc - 1 - c, 0))
    shape = jax.ShapeDtypeStruct((bsz, n_p, t, PAIR), ACT)
    return pl.pallas_call(
        _scan_kernel,
        grid=(bsz, nc),
        in_specs=[fwd] * 6 + [bwd] * 6,
        out_specs=[fwd, bwd],
        out_shape=[shape, shape],
        scratch_shapes=[pltpu.VMEM((2 * n_p, PAIR, PAIR), F32)],
        compiler_params=_params(("arbitrary", "arbitrary")),
        name="rwkv_scan",
    )(r, v, kk, lw_f, k_f, b_f, r, v, kk, lw_b, k_b, b_b)


def _na_kernel(q_ref, k_ref, v_ref, gq_ref, gk_ref, tab_ref, ones_ref, o_ref, qn_s, kn_s, vb_s, *, rows):
    ones = ones_ref[...]
    scale = HEAD_DIM ** -0.5
    q = q_ref[...].astype(F32)
    k = k_ref[...].astype(F32)
    inv_d = 1.0 / HEAD_DIM
    qn = q * lax.rsqrt(_mm_exact_rhs(q * q, ones) * inv_d + RMS_EPS) * (gq_ref[...] * scale)
    kn = k * lax.rsqrt(_mm_exact_rhs(k * k, ones) * inv_d + RMS_EPS) * gk_ref[...]
    qn_s[...] = qn.astype(BF16)
    kn_s[...] = kn.astype(BF16)
    vb_s[...] = v_ref[...].astype(BF16)
    lane = lax.broadcasted_iota(jnp.int32, (1, PAIR), 1)
    head0 = lane < HEAD_DIM
    win = NA_WR * GRID_W

    def row_group(gi, carry):
        rws = [gi * NA_ROWS_PER_STEP + j for j in range(NA_ROWS_PER_STEP)]
        rss = [jnp.clip(r - NA_WR // 2, 0, rows - NA_WR) for r in rws]
        q_rows = [qn_s[pl.ds(pl.multiple_of(r * GRID_W, GRID_W), GRID_W), :] for r in rws]
        k_wins = [kn_s[pl.ds(pl.multiple_of(rs * GRID_W, GRID_W), win), :] for rs in rss]
        v_wins = [vb_s[pl.ds(pl.multiple_of(rs * GRID_W, GRID_W), win), :] for rs in rss]
        ss = []
        for j in range(NA_ROWS_PER_STEP):
            d0 = rss[j] - rws[j] + NA_WR - 1
            for e in range(2):
                mask = head0 if e == 0 else jnp.logical_not(head0)
                qm = jnp.where(mask, q_rows[j], jnp.zeros_like(q_rows[j]))
                bias = jnp.concatenate(
                    [tab_ref[e, pl.ds(d0 + 2 * m, 1)][0] for m in range(NA_WR // 2)], axis=1)
                ss.append(_dot_nt(qm, k_wins[j]) + bias)
        mxs = [jnp.max(s, axis=-1, keepdims=True) for s in ss]
        ps = [jnp.exp(s - mx) for s, mx in zip(ss, mxs)]
        ls = [jnp.sum(p, axis=-1, keepdims=True) for p in ps]
        pvs = [_dot(p.astype(BF16), v_wins[i // 2]) for i, p in enumerate(ps)]
        for j in range(NA_ROWS_PER_STEP):
            o0 = pvs[2 * j] / ls[2 * j]
            o1 = pvs[2 * j + 1] / ls[2 * j + 1]
            o_ref[pl.ds(pl.multiple_of(rws[j] * GRID_W, GRID_W), GRID_W), :] = (
                jnp.where(head0, o0, o1).astype(o_ref.dtype))
        return carry

    lax.fori_loop(0, rows // NA_ROWS_PER_STEP, row_group, 0)


def _na_bias_table(rpb):
    qc = np.arange(GRID_W)
    kc = np.arange(GRID_W)
    cs = np.clip(qc - NA_WC // 2, 0, GRID_W - NA_WC)
    valid = (kc[None, :] >= cs[:, None]) & (kc[None, :] < cs[:, None] + NA_WC)
    dc = np.clip(kc[None, :] - qc[:, None] + NA_WC - 1, 0, 2 * NA_WC - 2)
    b = rpb.astype(F32)[:, :, dc]
    b = jnp.where(jnp.asarray(valid)[None, None], b, NEG_BIG)
    return jnp.concatenate([b[:, :-1], b[:, 1:]], axis=-1)


def _na(pb3, q_gain, k_gain, table, d_b):
    bsz, t, _ = pb3.shape
    rows = t // GRID_W
    n_p = d_b // PAIR
    ones = _block_ones(PAIR, HEAD_DIM)
    gq = jnp.tile(q_gain.reshape(1, HEAD_DIM), (1, 2))
    gk = jnp.tile(k_gain.reshape(1, HEAD_DIM), (1, 2))
    n_d = table.shape[1]
    c2 = lambda bi, p: (0, 0)
    return pl.pallas_call(
        functools.partial(_na_kernel, rows=rows),
        grid=(bsz, n_p),
        in_specs=[
            pl.BlockSpec((None, t, PAIR), lambda bi, p: (bi, 0, p)),
            pl.BlockSpec((None, t, PAIR), lambda bi, p: (bi, 0, n_p + p)),
            pl.BlockSpec((None, t, PAIR), lambda bi, p: (bi, 0, 2 * n_p + p)),
            pl.BlockSpec((1, PAIR), c2),
            pl.BlockSpec((1, PAIR), c2),
            pl.BlockSpec((2, n_d, GRID_W, PAIR), lambda bi, p: (p, 0, 0, 0)),
            pl.BlockSpec((PAIR, PAIR), c2),
        ],
        out_specs=pl.BlockSpec((None, t, PAIR), lambda bi, p: (bi, 0, p)),
        out_shape=jax.ShapeDtypeStruct((bsz, t, d_b), ACT),
        scratch_shapes=[pltpu.VMEM((t, PAIR), BF16)] * 3,
        compiler_params=_params(("arbitrary", "arbitrary")),
        name="natten",
    )(pb3, pb3, pb3, gq, gk, table, ones)


def _merge_kernel(of_ref, ob_ref, bonus_ref, g_ref, yb_ref, gates_ref, x_ref,
                  lng_ref, lnb_ref, wa_ref, wb_ref, wo_ref, gffn_ref, wr_ref, br_ref,
                  ones_ref, tri_ref,
                  x1_ref, h2_ref, idx_ref, rank_ref, gw_ref, cnt_ref, carry_ref, *, d_model):
    first = jnp.logical_and(pl.program_id(0) == 0, pl.program_id(1) == 0)

    @pl.when(first)
    def _():
        carry_ref[...] = jnp.zeros_like(carry_ref)

    n_p = of_ref.shape[0]
    o = jnp.concatenate([of_ref[p].astype(F32) + ob_ref[p].astype(F32) for p in range(n_p)],
                        axis=1)
    ones = ones_ref[...]
    inv_d = 1.0 / HEAD_DIM
    mu = _mm_exact_rhs(o, ones) * inv_d
    dv = o - mu
    var = _mm_exact_rhs(dv * dv, ones) * inv_d
    y = dv * lax.rsqrt(var + GN_EPS) * lng_ref[...] + lnb_ref[...] + bonus_ref[...].astype(F32)
    ya = y * g_ref[...].astype(F32)

    gates = gates_ref[...].astype(F32)
    pa = _dot(ya.astype(BF16), wa_ref[...])
    pb = _dot(yb_ref[...].astype(BF16), wb_ref[...])
    merged = _sigmoid(gates[:, :d_model]) * pa + _sigmoid(gates[:, d_model:]) * pb
    x1 = x_ref[...] + _dot(merged.astype(BF16), wo_ref[...])
    x1_ref[...] = x1
    ms = jnp.mean(x1 * x1, axis=-1, keepdims=True)
    h2 = x1 * lax.rsqrt(ms + RMS_EPS) * gffn_ref[...]
    _store_rows_as_tiles(h2_ref, h2)

    logits = _mm3(h2, wr_ref) + br_ref[...]
    tm = logits.shape[0]
    lane = lax.broadcasted_iota(jnp.int32, (tm, LANES), 1)
    work = logits
    vals, idxs = [], []
    for _ in range(TOP_K):
        m = jnp.max(work, axis=-1, keepdims=True)
        ix = jnp.min(jnp.where(work == m, lane, LANES), axis=-1, keepdims=True)
        vals.append(m)
        idxs.append(ix)
        work = jnp.where(lane == ix, -jnp.inf, work)
    es = [jnp.exp(vk - vals[0]) for vk in vals]
    den = es[0] + es[1] + es[2] + es[3]
    member = jnp.zeros((tm, LANES), F32)
    for ix in idxs:
        member = member + (lane == ix).astype(F32)
    before = _dot(tri_ref[...], member.astype(BF16)) + carry_ref[...]
    idx_out = jnp.zeros((tm, LANES), jnp.int32)
    rank_out = jnp.zeros((tm, LANES), jnp.int32)
    gw_out = jnp.zeros((tm, LANES), F32)
    for kq in range(TOP_K):
        rk = jnp.sum(jnp.where(lane == idxs[kq], before, 0.0), axis=-1, keepdims=True)
        sel = lane == kq
        idx_out = jnp.where(sel, idxs[kq], idx_out)
        rank_out = jnp.where(sel, rk.astype(jnp.int32), rank_out)
        gw_out = jnp.where(sel, es[kq] / den, gw_out)
    idx_ref[...] = idx_out
    rank_ref[...] = rank_out
    gw_ref[...] = gw_out
    carry_ref[...] = carry_ref[...] + jnp.sum(member, axis=0, keepdims=True)
    cnt_ref[...] = carry_ref[...]


def _merge(o_f, o_b, bonus, g, yb, gates3, x3, lnx_g, lnx_b, w_a, w_b, w_o, g_ffn, wr_pad, br_pad, tm=512):
    bsz, n_p, t, _ = o_f.shape
    d_a = n_p * PAIR
    d_b = yb.shape[-1]
    d_model = x3.shape[-1]
    n_t = t // tm
    m = bsz * t
    ones = _block_ones(d_a, HEAD_DIM)
    tri = jnp.asarray(np.tril(np.ones((tm, tm)), -1), BF16)
    c2 = lambda bi, i: (0, 0)
    tok = lambda w: pl.BlockSpec((None, tm, w), lambda bi, i: (bi, i, 0))
    flat = lambda w: pl.BlockSpec((tm, w), lambda bi, i: (bi * n_t + i, 0))
    pair = pl.BlockSpec((None, n_p, tm, PAIR), lambda bi, i: (bi, 0, i, 0))
    return pl.pallas_call(
        functools.partial(_merge_kernel, d_model=d_model),
        grid=(bsz, n_t),
        in_specs=[
            pair, pair, tok(d_a), tok(d_a), tok(d_b), tok(2 * d_model), tok(d_model),
            pl.BlockSpec((1, d_a), c2), pl.BlockSpec((1, d_a), c2),
            pl.BlockSpec((d_a, d_model), c2), pl.BlockSpec((d_b, d_model), c2),
            pl.BlockSpec((d_model, d_model), c2), pl.BlockSpec((1, d_model), c2),
            pl.BlockSpec((2, d_model, LANES), lambda bi, i: (0, 0, 0)), pl.BlockSpec((1, LANES), c2),
            pl.BlockSpec((d_a, d_a), c2), pl.BlockSpec((tm, tm), c2),
        ],
        out_specs=[flat(d_model),
                   pl.BlockSpec((tm * SUBLANES, LANES), lambda bi, i: (bi * n_t + i, 0)),
                   flat(LANES), flat(LANES), flat(LANES),
                   pl.BlockSpec((1, LANES), c2)],
        out_shape=[
            jax.ShapeDtypeStruct((m, d_model), F32),
            jax.ShapeDtypeStruct((m * SUBLANES, LANES), F32),
            jax.ShapeDtypeStruct((m, LANES), jnp.int32),
            jax.ShapeDtypeStruct((m, LANES), jnp.int32),
            jax.ShapeDtypeStruct((m, LANES), F32),
            jax.ShapeDtypeStruct((1, LANES), F32),
        ],
        scratch_shapes=[pltpu.VMEM((1, LANES), F32)],
        compiler_params=_params(("arbitrary", "arbitrary")),
        name="merge_router",
    )(o_f, o_b, bonus, g, yb, gates3, x3, lnx_g, lnx_b, w_a, w_b, w_o, g_ffn, _hilo(wr_pad), br_pad, ones, tri)


def _dispatch_kernel(ps_ref, pn_ref, nu_ref, dest_ref, h_ref, xs_ref, zbuf, sem, zsem, csem, bsem):
    i = pl.program_id(0)
    tm = h_ref.shape[0]
    n_e = ps_ref.shape[0]
    nb = xs_ref.shape[0] // MOE_BLOCK

    def row_copy(t, kq):
        return pltpu.make_async_copy(
            h_ref.at[t], xs_ref.at[dest_ref[0, t * TOP_K + kq]], sem)

    def pad_copy(r):
        return pltpu.make_async_copy(zbuf.at[0], xs_ref.at[r], zsem)

    def oct_copy(o):
        return pltpu.make_async_copy(
            zbuf.at[pl.ds(0, SUBLANES)], xs_ref.at[pl.ds(o * SUBLANES, SUBLANES)], csem)

    def blk_copy(b):
        return pltpu.make_async_copy(zbuf, xs_ref.at[pl.ds(b * MOE_BLOCK, MOE_BLOCK)], bsem)

    def pads(fn):
        def per_expert(e, c):
            start = ps_ref[e]
            end = start + pn_ref[e]
            first_oct = (start + SUBLANES - 1) // SUBLANES

            def single(r, c2):
                fn(pad_copy(r))
                return c2

            def octet(o, c2):
                fn(oct_copy(o))
                return c2
            c = lax.fori_loop(start, jnp.minimum(first_oct * SUBLANES, end), single, c)
            return lax.fori_loop(first_oct, end // SUBLANES, octet, c)
        lax.fori_loop(0, n_e, per_expert, 0)

        def per_block(b, c):
            fn(blk_copy(b))
            return c
        lax.fori_loop(nu_ref[0], nb, per_block, 0)

    @pl.when(i == 0)
    def _():
        zbuf[...] = jnp.zeros_like(zbuf)
        pads(lambda cp: cp.start())

    def issue(t, c):
        for kq in range(TOP_K):
            row_copy(t, kq).start(priority=kq % 2)
        return c

    def drain(t, c):
        for kq in range(TOP_K):
            row_copy(t, kq).wait()
        return c

    lax.fori_loop(0, tm, issue, 0, unroll=8)
    lax.fori_loop(0, tm, drain, 0, unroll=8)

    @pl.when(i == 0)
    def _():
        pads(lambda cp: cp.wait())


def _dispatch(dest, pad_start, pad_n, n_used, h2, n_pad, tm=256):
    m, ds, dl = h2.shape
    nt = m // tm
    dest3 = dest.reshape(nt, 1, tm * TOP_K)
    grid_spec = pltpu.PrefetchScalarGridSpec(
        num_scalar_prefetch=3,
        grid=(nt,),
        in_specs=[
            pl.BlockSpec((None, 1, tm * TOP_K), lambda i, ps, pn, nu: (i, 0, 0), memory_space=pltpu.SMEM),
            pl.BlockSpec((tm, ds, dl), lambda i, ps, pn, nu: (i, 0, 0)),
        ],
        out_specs=pl.BlockSpec(memory_space=pl.ANY),
        scratch_shapes=[pltpu.VMEM((MOE_BLOCK, ds, dl), F32)] + [pltpu.SemaphoreType.DMA(())] * 4,
    )
    return pl.pallas_call(
        _dispatch_kernel,
        grid_spec=grid_spec,
        out_shape=jax.ShapeDtypeStruct((n_pad, ds, dl), F32),
        compiler_params=_params(("arbitrary",)),
        name="moe_dispatch",
    )(pad_start, pad_n, n_used, dest3, h2)


def _expert_kernel(blk_ref, be_ref, nu_ref, xs_ref, w1_ref, b1_ref, w2_ref, b2_ref, ys_ref,
                   w1b_ref, w2b_ref, *, d_e):
    del blk_ref
    i = pl.program_id(0)
    new_expert = jnp.logical_or(i == 0, be_ref[i] != be_ref[jnp.maximum(i - 1, 0)])

    @pl.when(new_expert)
    def _():
        w1b_ref[...] = w1_ref[...].astype(BF16)
        w2b_ref[...] = w2_ref[...].astype(BF16)

    @pl.when(i < nu_ref[0])
    def _():
        x = _load_tiles_as_rows(xs_ref, MOE_BLOCK).astype(BF16)
        u = _dot(x, w1b_ref[...]) + b1_ref[...]
        glu = jnp.minimum(u[:, :d_e], SWIGLU_LIMIT)
        lin = jnp.clip(u[:, d_e:], -SWIGLU_LIMIT, SWIGLU_LIMIT)
        act = glu * _sigmoid(SWIGLU_ALPHA * glu) * (lin + 1.0)
        _store_rows_as_tiles(ys_ref, _dot(act.astype(BF16), w2b_ref[...]) + b2_ref[...])

    @pl.when(i >= nu_ref[0])
    def _():
        ys_ref[...] = jnp.zeros_like(ys_ref)


def _experts(blk_idx, blk_e, n_used, xs, w1, b1, w2, b2):
    n_pad = xs.shape[0] // SUBLANES
    d = SUBLANES * LANES
    nb = n_pad // MOE_BLOCK
    n_e, _, d2 = w1.shape
    d_e = d2 // 2
    grid_spec = pltpu.PrefetchScalarGridSpec(
        num_scalar_prefetch=3,
        grid=(nb,),
        in_specs=[
            pl.BlockSpec((MOE_BLOCK * SUBLANES, LANES), lambda i, bi, be, nu: (bi[i], 0)),
            pl.BlockSpec((None, d, d2), lambda i, bi, be, nu: (be[i], 0, 0)),
            pl.BlockSpec((None, 1, d2), lambda i, bi, be, nu: (be[i], 0, 0)),
            pl.BlockSpec((None, d_e, d), lambda i, bi, be, nu: (be[i], 0, 0)),
            pl.BlockSpec((None, 1, d), lambda i, bi, be, nu: (be[i], 0, 0)),
        ],
        out_specs=pl.BlockSpec((MOE_BLOCK * SUBLANES, LANES), lambda i, bi, be, nu: (i, 0)),
        scratch_shapes=[pltpu.VMEM((d, d2), BF16), pltpu.VMEM((d_e, d), BF16)],
    )
    return pl.pallas_call(
        functools.partial(_expert_kernel, d_e=d_e),
        grid_spec=grid_spec,
        out_shape=jax.ShapeDtypeStruct((n_pad * SUBLANES, LANES), F32),
        compiler_params=_params(("arbitrary",)),
        name="moe_experts",
    )(blk_idx, blk_e, n_used, xs, w1, b1.reshape(n_e, 1, d2), w2, b2.reshape(n_e, 1, d))


def _combine_kernel(dest_ref, dnext_ref, ys_ref, x1_ref, gw_ref, o_ref, buf, sems):
    i = pl.program_id(0)
    nt = pl.num_programs(0)
    tm = x1_ref.shape[0]
    slot = i % 2

    def row_copy(d_ref, sl, t, kq):
        return pltpu.make_async_copy(
            ys_ref.at[d_ref[0, t * TOP_K + kq]],
            buf.at[sl, kq, pl.ds(pl.multiple_of(t * SUBLANES, SUBLANES), SUBLANES)], sems.at[sl])

    def issue_all(d_ref, sl):
        def body(t, c):
            for kq in range(TOP_K):
                row_copy(d_ref, sl, t, kq).start(priority=kq % 2)
            return c
        lax.fori_loop(0, tm, body, 0, unroll=8)

    @pl.when(i == 0)
    def _():
        issue_all(dest_ref, 0)

    @pl.when(i + 1 < nt)
    def _():
        issue_all(dnext_ref, 1 - slot)

    def drain(t, c):
        for kq in range(TOP_K):
            row_copy(dest_ref, slot, t, kq).wait()
        return c

    lax.fori_loop(0, tm, drain, 0, unroll=8)
    gw = gw_ref[...]
    for s in range(SUBLANES):
        acc = x1_ref[:, s * LANES:(s + 1) * LANES]
        for kq in range(TOP_K):
            acc = acc + gw[:, kq:kq + 1] * buf[slot, kq, pl.ds(s, tm, stride=SUBLANES), :]
        o_ref[:, s * LANES:(s + 1) * LANES] = acc


def _combine(dest, ys, x1, gw, tm=128):
    m, d = x1.shape
    nt = m // tm
    dest3 = dest.reshape(nt, 1, tm * TOP_K)
    return pl.pallas_call(
        _combine_kernel,
        grid=(nt,),
        in_specs=[
            pl.BlockSpec((None, 1, tm * TOP_K), lambda i: (i, 0, 0), memory_space=pltpu.SMEM),
            pl.BlockSpec((None, 1, tm * TOP_K), lambda i: (jnp.minimum(i + 1, nt - 1), 0, 0),
                         memory_space=pltpu.SMEM),
            pl.BlockSpec(memory_space=pl.ANY),
            pl.BlockSpec((tm, d), lambda i: (i, 0)),
            pl.BlockSpec((tm, LANES), lambda i: (i, 0)),
        ],
        out_specs=pl.BlockSpec((tm, d), lambda i: (i, 0)),
        out_shape=jax.ShapeDtypeStruct((m, d), F32),
        scratch_shapes=[pltpu.VMEM((2, TOP_K, tm * SUBLANES, LANES), F32), pltpu.SemaphoreType.DMA((2,))],
        compiler_params=_params(("arbitrary",)),
        name="moe_combine",
    )(dest3, dest3, ys, x1, gw)


def _blockdiag2(a, b):
    za = jnp.zeros((a.shape[0], b.shape[1]), a.dtype)
    zb = jnp.zeros((b.shape[0], a.shape[1]), a.dtype)
    return jnp.concatenate([jnp.concatenate([a, za], axis=1), jnp.concatenate([zb, b], axis=1)], axis=0)


def _layer(x, g_mix, w_in, mu_prev, mu_next, w0_f, w2_f, w0_b, w2_b, a0_f, a2_f, a0_b, a2_b,
           g2, k_k, k_a, r_k, lnx_g, lnx_b, q_norm_g, k_norm_g, rpb, w_a, w_b, w_o,
           g_ffn, w_router, b_router, w1, b1, w2, b2):
    bsz, t, d_model = x.shape
    m = bsz * t
    d_a = w_a.shape[0]
    d_b = w_b.shape[0]
    a_cols = mu_prev.shape[0]
    b_cols = 3 * d_b
    row = lambda a: a.reshape(1, -1).astype(F32)

    w_in_b = w_in.astype(BF16)
    pa, pb, gates = _inproj(x.reshape(m, d_model), row(g_mix), w_in_b[:, :a_cols],
                            w_in_b[:, a_cols:a_cols + b_cols], w_in_b[:, a_cols + b_cols:])

    prep = _prep(pa.reshape(bsz, t, a_cols), row(mu_prev), row(mu_next),
                 jnp.concatenate([row(w0_f), row(w0_b)], axis=1), _blockdiag2(w2_f, w2_b),
                 jnp.concatenate([row(a0_f), row(a0_b)], axis=1), _blockdiag2(a2_f, a2_b),
                 g2, row(k_k), row(k_a), row(r_k), d_a)
    r, v, kk, lw_f, lw_b, k_f, k_b, b_f, b_b, bonus, g = prep
    o_f, o_b = _scan(r, v, kk, lw_f, lw_b, k_f, k_b, b_f, b_b)

    yb = _na(pb.reshape(bsz, t, b_cols), q_norm_g, k_norm_g, _na_bias_table(rpb), d_b)

    n_e = w_router.shape[1]
    wr_pad = jnp.zeros((d_model, LANES), F32).at[:, :n_e].set(w_router)
    br_pad = jnp.full((1, LANES), NEG_BIG, F32).at[0, :n_e].set(b_router)
    x1, h2, idx, rank, gw, cnt = _merge(
        o_f, o_b, bonus, g, yb, gates.reshape(bsz, t, 2 * d_model), x,
        row(lnx_g), row(lnx_b), w_a.astype(BF16), w_b.astype(BF16), w_o.astype(BF16),
        row(g_ffn), wr_pad, br_pad)

    counts = cnt[0, :n_e].astype(jnp.int32)
    padded = ((counts + MOE_BLOCK - 1) // MOE_BLOCK) * MOE_BLOCK
    pend = jnp.cumsum(padded)
    pstart = pend - padded
    n_assign = m * TOP_K
    n_blocks = -(-n_assign // MOE_BLOCK) + n_e
    n_pad = n_blocks * MOE_BLOCK
    top_idx = idx[:, :TOP_K]
    dest = (pstart[top_idx] + rank[:, :TOP_K]).reshape(-1).astype(jnp.int32)
    n_used = (pend[-1] // MOE_BLOCK).astype(jnp.int32)
    blk_idx = jnp.minimum(jnp.arange(n_blocks, dtype=jnp.int32), n_used - 1)
    blk_e = jnp.sum((blk_idx[:, None] * MOE_BLOCK >= pend[None, :]).astype(jnp.int32), axis=1)
    blk_e = jnp.minimum(blk_e, n_e - 1)

    xs = _dispatch(dest, (pstart + counts).astype(jnp.int32), (padded - counts).astype(jnp.int32),
                   n_used.reshape(1), h2.reshape(m, SUBLANES, LANES), n_pad)
    ys = _experts(blk_idx, blk_e, n_used.reshape(1), xs.reshape(n_pad * SUBLANES, LANES), w1, b1, w2, b2)
    out = _combine(dest, ys.reshape(n_pad, SUBLANES, LANES), x1, gw)
    return out.reshape(bsz, t, d_model)


def kernel(x, g_mix, w_in, mu_prev, mu_next, w0_f, w2_f, w0_b, w2_b, a0_f, a2_f, a0_b, a2_b, g2, k_k, k_a, r_k, lnx_g, lnx_b, q_norm_g, k_norm_g, rpb, w_a, w_b, w_o, g_ffn, w_router, b_router, w1, b1, w2, b2):
    for l in range(g_mix.shape[0]):
        x = _layer(x, g_mix[l], w_in[l], mu_prev[l], mu_next[l], w0_f[l], w2_f[l], w0_b[l], w2_b[l],
                   a0_f[l], a2_f[l], a0_b[l], a2_b[l], g2[l], k_k[l], k_a[l], r_k[l], lnx_g[l], lnx_b[l],
                   q_norm_g[l], k_norm_g[l], rpb[l], w_a[l], w_b[l], w_o[l], g_ffn[l], w_router[l],
                   b_router[l], w1[l], b1[l], w2[l], b2[l])
    return x
```

```python
import functools
import math

import numpy as np
import jax
import jax.numpy as jnp
from jax import lax
from jax.experimental import pallas as pl
from jax.experimental.pallas import tpu as pltpu

F32 = jnp.float32
BF16 = jnp.bfloat16
ACT = jnp.bfloat16

LANES = 128
SUBLANES = 8
HEAD_DIM = 64
PAIR = 2 * HEAD_DIM
GRID_W = 64
NA_WR = 8
NA_WC = 16
W_LORA = 64
A_LORA = 64
G_LORA = 128
DECAY_SCALE = math.exp(-0.5)
GN_EPS = 64e-5
RMS_EPS = 1e-5
N_EXPERTS = 32
TOP_K = 4
MOE_BLOCK = 512
SWIGLU_LIMIT = 7.0
SWIGLU_ALPHA = 1.702
NEG_BIG = -1e30
CHUNK = 128
NA_ROWS_PER_STEP = 8
VMEM_LIMIT = 56 * 1024 * 1024


def _dot(a, b):
    return jnp.dot(a, b, preferred_element_type=F32)


def _dot_nt(a, b):
    return lax.dot_general(a, b, (((1,), (1,)), ((), ())), preferred_element_type=F32)


def _split(a):
    hi = a.astype(BF16)
    lo = (a - hi.astype(F32)).astype(BF16)
    return hi, lo


def _mm_exact_rhs(a, b_bf16):
    hi, lo = _split(a)
    return _dot(hi, b_bf16) + _dot(lo, b_bf16)


def _hilo(b):
    return jnp.stack(_split(b))


def _mm3(a, b_hilo):
    ah, al = _split(a)
    bh = b_hilo[0]
    return _dot(ah, bh) + _dot(al, bh) + _dot(ah, b_hilo[1])


def _store_rows_as_tiles(ref, val):
    n = val.shape[0]
    for s in range(SUBLANES):
        ref[pl.ds(s, n, stride=SUBLANES), :] = val[:, s * LANES:(s + 1) * LANES]


def _load_tiles_as_rows(ref, n):
    return jnp.concatenate([ref[pl.ds(s, n, stride=SUBLANES), :] for s in range(SUBLANES)], axis=1)


def _sigmoid(x):
    return 1.0 / (1.0 + jnp.exp(-x))


def _params(sem):
    return pltpu.CompilerParams(dimension_semantics=sem, vmem_limit_bytes=VMEM_LIMIT)


def _block_ones(n, blk):
    i = np.arange(n) // blk
    return jnp.asarray(i[:, None] == i[None, :], BF16)


def _inproj_kernel(x_ref, g_ref, wa_ref, wb_ref, wg_ref, pa_ref, pb_ref, pg_ref):
    x = x_ref[...]
    ms = jnp.mean(x * x, axis=-1, keepdims=True)
    h = (x * lax.rsqrt(ms + RMS_EPS) * g_ref[...]).astype(BF16)
    pa_ref[...] = _dot(h, wa_ref[...]).astype(pa_ref.dtype)
    pb_ref[...] = _dot(h, wb_ref[...]).astype(pb_ref.dtype)
    pg_ref[...] = _dot(h, wg_ref[...]).astype(pg_ref.dtype)


def _inproj(x2, g_mix, w_a, w_b, w_g, tm=256):
    m, d = x2.shape
    na, nb, ng = w_a.shape[1], w_b.shape[1], w_g.shape[1]
    full = lambda i: (0, 0)
    return pl.pallas_call(
        _inproj_kernel,
        grid=(m // tm,),
        in_specs=[
            pl.BlockSpec((tm, d), lambda i: (i, 0)),
            pl.BlockSpec((1, d), full),
            pl.BlockSpec((d, na), full),
            pl.BlockSpec((d, nb), full),
            pl.BlockSpec((d, ng), full),
        ],
        out_specs=[
            pl.BlockSpec((tm, na), lambda i: (i, 0)),
            pl.BlockSpec((tm, nb), lambda i: (i, 0)),
            pl.BlockSpec((tm, ng), lambda i: (i, 0)),
        ],
        out_shape=[
            jax.ShapeDtypeStruct((m, na), ACT),
            jax.ShapeDtypeStruct((m, nb), ACT),
            jax.ShapeDtypeStruct((m, ng), ACT),
        ],
        compiler_params=_params(("arbitrary",)),
        name="inproj",
    )(x2, g_mix, w_a, w_b, w_g)


def _prep_kernel(p_ref, prev_ref, next_ref, mup_ref, mun_ref, w0_ref, w2_ref, a0_ref, a2_ref,
                 g2_ref, kk_ref, ka_ref, rk_ref, ones_ref,
                 r_o, v_o, kk_o, lwf_o, lwb_o, kf_o, kb_o, bf_o, bb_o, bonus_o, g_o, *, d_a):
    i = pl.program_id(1)
    n_t = pl.num_programs(1)
    p = p_ref[...].astype(F32)
    tt = p.shape[0]
    halo = prev_ref.shape[0]
    prow = jnp.where(i > 0, prev_ref[halo - 1:halo, :].astype(F32), 0.0)
    nrow = jnp.where(i < n_t - 1, next_ref[0:1, :].astype(F32), 0.0)
    rid = lax.broadcasted_iota(jnp.int32, (tt, 1), 0)
    prev = jnp.where(rid == 0, prow, pltpu.roll(p, 1, axis=0))
    nxt = jnp.where(rid == tt - 1, nrow, pltpu.roll(p, tt - 1, axis=0))
    xa = p + mup_ref[...] * (prev - p) + mun_ref[...] * (nxt - p)

    r = xa[:, 0:d_a]
    k = xa[:, d_a:2 * d_a]
    v = xa[:, 2 * d_a:3 * d_a]
    o = 3 * d_a
    lw = xa[:, o:o + 2 * W_LORA]
    la = xa[:, o + 2 * W_LORA:o + 2 * W_LORA + 2 * A_LORA]
    lg = xa[:, o + 2 * W_LORA + 2 * A_LORA:]

    dpre = w0_ref[...] + _mm3(jnp.tanh(lw), w2_ref)
    apre = a0_ref[...] + _mm3(la, a2_ref)
    g = _mm3(_sigmoid(lg), g2_ref)
    logw = -DECAY_SCALE * _sigmoid(dpre)
    a = _sigmoid(apre)

    ones = ones_ref[...]
    kkr = k * kk_ref[...]
    ss = _mm_exact_rhs(kkr * kkr, ones)
    kk = kkr / jnp.maximum(jnp.sqrt(ss), 1e-12)

    ka = ka_ref[...]
    k_f = k * (1.0 + (a[:, :d_a] - 1.0) * ka)
    k_b = k * (1.0 + (a[:, d_a:] - 1.0) * ka)
    b_f = kk * a[:, :d_a]
    b_b = kk * a[:, d_a:]
    rk = rk_ref[...]
    bon = _mm_exact_rhs(r * (k_f + k_b) * rk, ones) * v

    bonus_o[...] = bon.astype(bonus_o.dtype)
    g_o[...] = g.astype(g_o.dtype)
    for pi in range(d_a // PAIR):
        sl = slice(pi * PAIR, (pi + 1) * PAIR)
        r_o[pi] = r[:, sl].astype(r_o.dtype)
        v_o[pi] = v[:, sl].astype(v_o.dtype)
        kk_o[pi] = kk[:, sl].astype(kk_o.dtype)
        lwf_o[pi] = logw[:, sl]
        lwb_o[pi] = logw[:, d_a + pi * PAIR:d_a + (pi + 1) * PAIR]
        kf_o[pi] = k_f[:, sl].astype(kf_o.dtype)
        kb_o[pi] = k_b[:, sl].astype(kb_o.dtype)
        bf_o[pi] = b_f[:, sl].astype(bf_o.dtype)
        bb_o[pi] = b_b[:, sl].astype(bb_o.dtype)


def _prep(pa3, mu_prev, mu_next, w0c, w2blk, a0c, a2blk, g2, k_k, k_a, r_k, d_a, tt=256):
    b, t, ac = pa3.shape
    n_t = t // tt
    n_p = d_a // PAIR
    ones = _block_ones(d_a, HEAD_DIM)
    c2 = lambda bi, i: (0, 0)
    c3 = lambda bi, i: (0, 0, 0)
    pair_spec = pl.BlockSpec((None, n_p, tt, PAIR), lambda bi, i: (bi, 0, i, 0))
    pair_act = jax.ShapeDtypeStruct((b, n_p, t, PAIR), ACT)
    pair_f32 = jax.ShapeDtypeStruct((b, n_p, t, PAIR), F32)
    flat_spec = pl.BlockSpec((None, tt, d_a), lambda bi, i: (bi, i, 0))
    flat_shape = jax.ShapeDtypeStruct((b, t, d_a), ACT)
    halo = 16
    r8 = tt // halo
    return pl.pallas_call(
        functools.partial(_prep_kernel, d_a=d_a),
        grid=(b, n_t),
        in_specs=[
            pl.BlockSpec((None, tt, ac), lambda bi, i: (bi, i, 0)),
            pl.BlockSpec((None, halo, ac), lambda bi, i: (bi, jnp.maximum(i * r8 - 1, 0), 0)),
            pl.BlockSpec((None, halo, ac), lambda bi, i: (bi, jnp.minimum((i + 1) * r8, t // halo - 1), 0)),
            pl.BlockSpec((1, ac), c2),
            pl.BlockSpec((1, ac), c2),
            pl.BlockSpec((1, 2 * d_a), c2),
            pl.BlockSpec((2, 2 * W_LORA, 2 * d_a), c3),
            pl.BlockSpec((1, 2 * d_a), c2),
            pl.BlockSpec((2, 2 * A_LORA, 2 * d_a), c3),
            pl.BlockSpec((2, G_LORA, d_a), c3),
            pl.BlockSpec((1, d_a), c2),
            pl.BlockSpec((1, d_a), c2),
            pl.BlockSpec((1, d_a), c2),
            pl.BlockSpec((d_a, d_a), c2),
        ],
        out_specs=[pair_spec] * 9 + [flat_spec] * 2,
        out_shape=[pair_act] * 3 + [pair_f32] * 2 + [pair_act] * 4 + [flat_shape] * 2,
        compiler_params=_params(("arbitrary", "arbitrary")),
        name="rwkv_prep",
    )(pa3, pa3, pa3, mu_prev, mu_next, w0c, _hilo(w2blk), a0c, _hilo(a2blk), _hilo(g2), k_k, k_a, r_k, ones)


def _mm_exact_rhs_left(tri_bf16, x):
    hi, lo = _split(x)
    return _dot(tri_bf16, hi) + _dot(tri_bf16, lo)


def _scan_stage(items, s_refs):
    c = items[0][0].shape[0]
    ri = lax.broadcasted_iota(jnp.int32, (c, c), 0)
    ci = lax.broadcasted_iota(jnp.int32, (c, c), 1)
    lane = lax.broadcasted_iota(jnp.int32, (1, PAIR), 1)
    m0 = (lane < HEAD_DIM).astype(F32)
    m1 = 1.0 - m0
    eye = (ri == ci).astype(F32)
    hi_ = lax.broadcasted_iota(jnp.int32, (PAIR, PAIR), 0) // HEAD_DIM
    hj_ = lax.broadcasted_iota(jnp.int32, (PAIR, PAIR), 1) // HEAD_DIM
    same_head = hi_ == hj_
    zero = jnp.zeros((), F32)
    n = len(items)

    def masks(reverse):
        if reverse:
            return ci >= ri, ci > ri, 0
        return ci <= ri, ci < ri, c - 1

    cums = [_mm_exact_rhs_left(masks(it[6])[0].astype(BF16), it[3]) for it in items]

    pre = []
    for (r, kk, v, lw, k, b, reverse), cum in zip(items, cums):
        end = masks(reverse)[2]
        cmid = cum[c // 2:c // 2 + 1, :]
        cend = cum[end:end + 1, :]
        r_abs = r * jnp.exp(cum)
        a_abs = -kk * jnp.exp(cum - lw)
        to_mid = jnp.exp(-cmid)
        from_mid = jnp.exp(cmid - cum)
        to_end = jnp.exp(cend - cum)
        pre.append(dict(
            r_abs=r_abs, a_abs=a_abs, r_rel=r_abs * to_mid, a_rel=a_abs * to_mid,
            k_rel=k * from_mid, b_rel=b * from_mid, k_end=k * to_end, b_end=b * to_end,
            d_tot=jnp.exp(cend), v=v))

    grams = []
    for p in pre:
        lhs = jnp.concatenate([p["r_rel"] * m0, p["r_rel"] * m1, p["a_rel"] * m0, p["a_rel"] * m1],
                              axis=0).astype(BF16)
        rhs = jnp.concatenate([p["k_rel"], p["b_rel"]], axis=0).astype(BF16)
        grams.append(_dot_nt(lhs, rhs))

    pws, ts = [], []
    for it, gram in zip(items, grams):
        strict = masks(it[6])[1]
        for e in range(2):
            a_ab = jnp.where(strict, gram[(2 + e) * c:(3 + e) * c, c:2 * c], zero)
            pws.append(a_ab)
            ts.append(eye + a_ab)
    for _ in range(int(round(math.log2(c))) - 1):
        pws = [_dot(pw.astype(BF16), pw.astype(BF16)) for pw in pws]
        ts = [t + _dot(t.astype(BF16), pw.astype(BF16)) for t, pw in zip(ts, pws)]

    v_blks = [jnp.concatenate([p["v"] * m0, p["v"] * m1], axis=0).astype(BF16) for p in pre]
    akvs, o_rks, rb_cats = [], [], []
    for it, gram, v_blk in zip(items, grams, v_blks):
        incl, strict, _ = masks(it[6])
        ak_cat = jnp.concatenate(
            [jnp.where(strict, gram[(2 + e) * c:(3 + e) * c, 0:c], zero) for e in range(2)], axis=1)
        rk_cat = jnp.concatenate(
            [jnp.where(incl, gram[e * c:(e + 1) * c, 0:c], zero) for e in range(2)], axis=1)
        rb_cats.append(jnp.concatenate(
            [jnp.where(incl, gram[e * c:(e + 1) * c, c:2 * c], zero) for e in range(2)], axis=1).astype(BF16))
        akvs.append(_dot(ak_cat.astype(BF16), v_blk))
        o_rks.append(_dot(rk_cat.astype(BF16), v_blk))

    xs = []
    for i, (p, akv) in enumerate(zip(pre, akvs)):
        t_cat = jnp.concatenate([ts[2 * i], ts[2 * i + 1]], axis=1).astype(BF16)
        y_blk = jnp.concatenate([
            jnp.concatenate([p["a_abs"] * m0, akv * m0], axis=1),
            jnp.concatenate([p["a_abs"] * m1, akv * m1], axis=1)], axis=0).astype(BF16)
        xs.append(_dot(t_cat, y_blk))

    s0s = [s_ref[...] for s_ref in s_refs]
    s0bs = [s0.astype(BF16) for s0 in s0s]
    us = [_dot(x[:, 0:PAIR].astype(BF16), s0b) + x[:, PAIR:2 * PAIR] for x, s0b in zip(xs, s0bs)]
    outs = []
    for i in range(n):
        p, u = pre[i], us[i]
        u_blk = jnp.concatenate([u * m0, u * m1], axis=0).astype(BF16)
        outs.append(_dot(p["r_abs"].astype(BF16), s0bs[i]) + _dot(rb_cats[i], u_blk) + o_rks[i])
        kb_t = jnp.concatenate([p["b_end"].T, p["k_end"].T], axis=1).astype(BF16)
        uv = jnp.concatenate([u, p["v"]], axis=0).astype(BF16)
        d_col = jnp.broadcast_to(p["d_tot"], (PAIR, PAIR)).T
        s_refs[i][...] = jnp.where(same_head, d_col * s0s[i] + _dot(kb_t, uv), zero)
    return outs


def _scan_kernel(rf, vf, kkf, lwf, kf, bf, rb, vb, kkb, lwb, kb, bb, of_ref, ob_ref, s_ref):
    @pl.when(pl.program_id(1) == 0)
    def _():
        s_ref[...] = jnp.zeros_like(s_ref)

    n_p = rf.shape[0]
    items, s_refs = [], []
    for p in range(n_p):
        ld = lambda ref: ref[p].astype(F32)
        items.append((ld(rf), ld(kkf), ld(vf), lwf[p], ld(kf), ld(bf), False))
        s_refs.append(s_ref.at[2 * p])
        items.append((ld(rb), ld(kkb), ld(vb), lwb[p], ld(kb), ld(bb), True))
        s_refs.append(s_ref.at[2 * p + 1])
    outs = _scan_stage(items, s_refs)
    for p in range(n_p):
        of_ref[p] = outs[2 * p].astype(of_ref.dtype)
        ob_ref[p] = outs[2 * p + 1].astype(ob_ref.dtype)


def _scan(r, v, kk, lw_f, lw_b, k_f, k_b, b_f, b_b):
    bsz, n_p, t, _ = r.shape
    nc = t // CHUNK
    fwd = pl.BlockSpec((None, n_p, CHUNK, PAIR), lambda bi, c: (bi, 0, c, 0))
    bwd = pl.BlockSpec((None, n_p, CHUNK, PAIR), lambda bi, c: (bi, 0, nc - 1 - c, 0))
    shape = jax.ShapeDtypeStruct((bsz, n_p, t, PAIR), ACT)
    return pl.pallas_call(
        _scan_kernel,
        grid=(bsz, nc),
        in_specs=[fwd] * 6 + [bwd] * 6,
        out_specs=[fwd, bwd],
        out_shape=[shape, shape],
        scratch_shapes=[pltpu.VMEM((2 * n_p, PAIR, PAIR), F32)],
        compiler_params=_params(("arbitrary", "arbitrary")),
        name="rwkv_scan",
    )(r, v, kk, lw_f, k_f, b_f, r, v, kk, lw_b, k_b, b_b)


def _na_kernel(q_ref, k_ref, v_ref, gq_ref, gk_ref, tab_ref, ones_ref, o_ref, qn_s, kn_s, vb_s, *, rows):
    ones = ones_ref[...]
    scale = HEAD_DIM ** -0.5
    q = q_ref[...].astype(F32)
    k = k_ref[...].astype(F32)
    inv_d = 1.0 / HEAD_DIM
    qn = q * lax.rsqrt(_mm_exact_rhs(q * q, ones) * inv_d + RMS_EPS) * (gq_ref[...] * scale)
    kn = k * lax.rsqrt(_mm_exact_rhs(k * k, ones) * inv_d + RMS_EPS) * gk_ref[...]
    qn_s[...] = qn.astype(BF16)
    kn_s[...] = kn.astype(BF16)
    vb_s[...] = v_ref[...].astype(BF16)
    lane = lax.broadcasted_iota(jnp.int32, (1, PAIR), 1)
    head0 = lane < HEAD_DIM
    win = NA_WR * GRID_W

    def row_group(gi, carry):
        rws = [gi * NA_ROWS_PER_STEP + j for j in range(NA_ROWS_PER_STEP)]
        rss = [jnp.clip(r - NA_WR // 2, 0, rows - NA_WR) for r in rws]
        q_rows = [qn_s[pl.ds(pl.multiple_of(r * GRID_W, GRID_W), GRID_W), :] for r in rws]
        k_wins = [kn_s[pl.ds(pl.multiple_of(rs * GRID_W, GRID_W), win), :] for rs in rss]
        v_wins = [vb_s[pl.ds(pl.multiple_of(rs * GRID_W, GRID_W), win), :] for rs in rss]
        ss = []
        for j in range(NA_ROWS_PER_STEP):
            d0 = rss[j] - rws[j] + NA_WR - 1
            for e in range(2):
                mask = head0 if e == 0 else jnp.logical_not(head0)
                qm = jnp.where(mask, q_rows[j], jnp.zeros_like(q_rows[j]))
                bias = jnp.concatenate(
                    [tab_ref[e, pl.ds(d0 + 2 * m, 1)][0] for m in range(NA_WR // 2)], axis=1)
                ss.append(_dot_nt(qm, k_wins[j]) + bias)
        mxs = [jnp.max(s, axis=-1, keepdims=True) for s in ss]
        ps = [jnp.exp(s - mx) for s, mx in zip(ss, mxs)]
        ls = [jnp.sum(p, axis=-1, keepdims=True) for p in ps]
        pvs = [_dot(p.astype(BF16), v_wins[i // 2]) for i, p in enumerate(ps)]
        for j in range(NA_ROWS_PER_STEP):
            o0 = pvs[2 * j] / ls[2 * j]
            o1 = pvs[2 * j + 1] / ls[2 * j + 1]
            o_ref[pl.ds(pl.multiple_of(rws[j] * GRID_W, GRID_W), GRID_W), :] = (
                jnp.where(head0, o0, o1).astype(o_ref.dtype))
        return carry

    lax.fori_loop(0, rows // NA_ROWS_PER_STEP, row_group, 0)


def _na_bias_table(rpb):
    qc = np.arange(GRID_W)
    kc = np.arange(GRID_W)
    cs = np.clip(qc - NA_WC // 2, 0, GRID_W - NA_WC)
    valid = (kc[None, :] >= cs[:, None]) & (kc[None, :] < cs[:, None] + NA_WC)
    dc = np.clip(kc[None, :] - qc[:, None] + NA_WC - 1, 0, 2 * NA_WC - 2)
    b = rpb.astype(F32)[:, :, dc]
    b = jnp.where(jnp.asarray(valid)[None, None], b, NEG_BIG)
    return jnp.concatenate([b[:, :-1], b[:, 1:]], axis=-1)


def _na(pb3, q_gain, k_gain, table, d_b):
    bsz, t, _ = pb3.shape
    rows = t // GRID_W
    n_p = d_b // PAIR
    ones = _block_ones(PAIR, HEAD_DIM)
    gq = jnp.tile(q_gain.reshape(1, HEAD_DIM), (1, 2))
    gk = jnp.tile(k_gain.reshape(1, HEAD_DIM), (1, 2))
    n_d = table.shape[1]
    c2 = lambda bi, p: (0, 0)
    return pl.pallas_call(
        functools.partial(_na_kernel, rows=rows),
        grid=(bsz, n_p),
        in_specs=[
            pl.BlockSpec((None, t, PAIR), lambda bi, p: (bi, 0, p)),
            pl.BlockSpec((None, t, PAIR), lambda bi, p: (bi, 0, n_p + p)),
            pl.BlockSpec((None, t, PAIR), lambda bi, p: (bi, 0, 2 * n_p + p)),
            pl.BlockSpec((1, PAIR), c2),
            pl.BlockSpec((1, PAIR), c2),
            pl.BlockSpec((2, n_d, GRID_W, PAIR), lambda bi, p: (p, 0, 0, 0)),
            pl.BlockSpec((PAIR, PAIR), c2),
        ],
        out_specs=pl.BlockSpec((None, t, PAIR), lambda bi, p: (bi, 0, p)),
        out_shape=jax.ShapeDtypeStruct((bsz, t, d_b), ACT),
        scratch_shapes=[pltpu.VMEM((t, PAIR), BF16)] * 3,
        compiler_params=_params(("arbitrary", "arbitrary")),
        name="natten",
    )(pb3, pb3, pb3, gq, gk, table, ones)


def _merge_kernel(of_ref, ob_ref, bonus_ref, g_ref, yb_ref, gates_ref, x_ref,
                  lng_ref, lnb_ref, wa_ref, wb_ref, wo_ref, gffn_ref, wr_ref, br_ref,
                  ones_ref, tri_ref,
                  x1_ref, h2_ref, idx_ref, rank_ref, gw_ref, cnt_ref, carry_ref, *, d_model):
    first = jnp.logical_and(pl.program_id(0) == 0, pl.program_id(1) == 0)

    @pl.when(first)
    def _():
        carry_ref[...] = jnp.zeros_like(carry_ref)

    n_p = of_ref.shape[0]
    o = jnp.concatenate([of_ref[p].astype(F32) + ob_ref[p].astype(F32) for p in range(n_p)],
                        axis=1)
    ones = ones_ref[...]
    inv_d = 1.0 / HEAD_DIM
    mu = _mm_exact_rhs(o, ones) * inv_d
    dv = o - mu
    var = _mm_exact_rhs(dv * dv, ones) * inv_d
    y = dv * lax.rsqrt(var + GN_EPS) * lng_ref[...] + lnb_ref[...] + bonus_ref[...].astype(F32)
    ya = y * g_ref[...].astype(F32)

    gates = gates_ref[...].astype(F32)
    pa = _dot(ya.astype(BF16), wa_ref[...])
    pb = _dot(yb_ref[...].astype(BF16), wb_ref[...])
    merged = _sigmoid(gates[:, :d_model]) * pa + _sigmoid(gates[:, d_model:]) * pb
    x1 = x_ref[...] + _dot(merged.astype(BF16), wo_ref[...])
    x1_ref[...] = x1
    ms = jnp.mean(x1 * x1, axis=-1, keepdims=True)
    h2 = x1 * lax.rsqrt(ms + RMS_EPS) * gffn_ref[...]
    _store_rows_as_tiles(h2_ref, h2)

    logits = _mm3(h2, wr_ref) + br_ref[...]
    tm = logits.shape[0]
    lane = lax.broadcasted_iota(jnp.int32, (tm, LANES), 1)
    work = logits
    vals, idxs = [], []
    for _ in range(TOP_K):
        m = jnp.max(work, axis=-1, keepdims=True)
        ix = jnp.min(jnp.where(work == m, lane, LANES), axis=-1, keepdims=True)
        vals.append(m)
        idxs.append(ix)
        work = jnp.where(lane == ix, -jnp.inf, work)
    es = [jnp.exp(vk - vals[0]) for vk in vals]
    den = es[0] + es[1] + es[2] + es[3]
    member = jnp.zeros((tm, LANES), F32)
    for ix in idxs:
        member = member + (lane == ix).astype(F32)
    before = _dot(tri_ref[...], member.astype(BF16)) + carry_ref[...]
    idx_out = jnp.zeros((tm, LANES), F32)
    rank_out = jnp.zeros((tm, LANES), F32)
    gw_out = jnp.zeros((tm, LANES), F32)
    for kq in range(TOP_K):
        rk = jnp.sum(jnp.where(lane == idxs[kq], before, 0.0), axis=-1, keepdims=True)
        sel = lane == kq
        idx_out = jnp.where(sel, idxs[kq].astype(F32), idx_out)
        rank_out = jnp.where(sel, rk, rank_out)
        gw_out = jnp.where(sel, es[kq] / den, gw_out)
    idx_ref[...] = idx_out.T[:SUBLANES].astype(jnp.int32)
    rank_ref[...] = rank_out.T[:SUBLANES].astype(jnp.int32)
    gw_ref[...] = gw_out
    carry_ref[...] = carry_ref[...] + jnp.sum(member, axis=0, keepdims=True)
    cnt_ref[...] = carry_ref[...]


def _merge(o_f, o_b, bonus, g, yb, gates3, x3, lnx_g, lnx_b, w_a, w_b, w_o, g_ffn, wr_pad, br_pad, tm=512):
    bsz, n_p, t, _ = o_f.shape
    d_a = n_p * PAIR
    d_b = yb.shape[-1]
    d_model = x3.shape[-1]
    n_t = t // tm
    m = bsz * t
    ones = _block_ones(d_a, HEAD_DIM)
    tri = jnp.asarray(np.tril(np.ones((tm, tm)), -1), BF16)
    c2 = lambda bi, i: (0, 0)
    tok = lambda w: pl.BlockSpec((None, tm, w), lambda bi, i: (bi, i, 0))
    flat = lambda w: pl.BlockSpec((tm, w), lambda bi, i: (bi * n_t + i, 0))
    pair = pl.BlockSpec((None, n_p, tm, PAIR), lambda bi, i: (bi, 0, i, 0))
    return pl.pallas_call(
        functools.partial(_merge_kernel, d_model=d_model),
        grid=(bsz, n_t),
        in_specs=[
            pair, pair, tok(d_a), tok(d_a), tok(d_b), tok(2 * d_model), tok(d_model),
            pl.BlockSpec((1, d_a), c2), pl.BlockSpec((1, d_a), c2),
            pl.BlockSpec((d_a, d_model), c2), pl.BlockSpec((d_b, d_model), c2),
            pl.BlockSpec((d_model, d_model), c2), pl.BlockSpec((1, d_model), c2),
            pl.BlockSpec((2, d_model, LANES), lambda bi, i: (0, 0, 0)), pl.BlockSpec((1, LANES), c2),
            pl.BlockSpec((d_a, d_a), c2), pl.BlockSpec((tm, tm), c2),
        ],
        out_specs=[flat(d_model),
                   pl.BlockSpec((tm * SUBLANES, LANES), lambda bi, i: (bi * n_t + i, 0)),
                   pl.BlockSpec((SUBLANES, tm), lambda bi, i: (0, bi * n_t + i)),
                   pl.BlockSpec((SUBLANES, tm), lambda bi, i: (0, bi * n_t + i)),
                   flat(LANES),
                   pl.BlockSpec((1, LANES), c2)],
        out_shape=[
            jax.ShapeDtypeStruct((m, d_model), F32),
            jax.ShapeDtypeStruct((m * SUBLANES, LANES), F32),
            jax.ShapeDtypeStruct((SUBLANES, m), jnp.int32),
            jax.ShapeDtypeStruct((SUBLANES, m), jnp.int32),
            jax.ShapeDtypeStruct((m, LANES), F32),
            jax.ShapeDtypeStruct((1, LANES), F32),
        ],
        scratch_shapes=[pltpu.VMEM((1, LANES), F32)],
        compiler_params=_params(("arbitrary", "arbitrary")),
        name="merge_router",
    )(o_f, o_b, bonus, g, yb, gates3, x3, lnx_g, lnx_b, w_a, w_b, w_o, g_ffn, _hilo(wr_pad), br_pad, ones, tri)


def _dispatch_kernel(ps_ref, pn_ref, nu_ref, dest_ref, h_ref, xs_ref, zbuf, sem, zsem, csem, bsem):
    i = pl.program_id(0)
    tm = h_ref.shape[0]
    n_e = ps_ref.shape[0]
    nb = xs_ref.shape[0] // MOE_BLOCK

    def row_copy(t, kq):
        return pltpu.make_async_copy(
            h_ref.at[t], xs_ref.at[dest_ref[kq, t]], sem)

    def pad_copy(r):
        return pltpu.make_async_copy(zbuf.at[0], xs_ref.at[r], zsem)

    def oct_copy(o):
        return pltpu.make_async_copy(
            zbuf.at[pl.ds(0, SUBLANES)], xs_ref.at[pl.ds(o * SUBLANES, SUBLANES)], csem)

    def blk_copy(b):
        return pltpu.make_async_copy(zbuf, xs_ref.at[pl.ds(b * MOE_BLOCK, MOE_BLOCK)], bsem)

    def pads(fn):
        def per_expert(e, c):
            start = ps_ref[e]
            end = start + pn_ref[e]
            first_oct = (start + SUBLANES - 1) // SUBLANES

            def single(r, c2):
                fn(pad_copy(r))
                return c2

            def octet(o, c2):
                fn(oct_copy(o))
                return c2
            c = lax.fori_loop(start, jnp.minimum(first_oct * SUBLANES, end), single, c)
            return lax.fori_loop(first_oct, end // SUBLANES, octet, c)
        lax.fori_loop(0, n_e, per_expert, 0)

        def per_block(b, c):
            fn(blk_copy(b))
            return c
        lax.fori_loop(nu_ref[0], nb, per_block, 0)

    @pl.when(i == 0)
    def _():
        zbuf[...] = jnp.zeros_like(zbuf)
        pads(lambda cp: cp.start())

    def issue(t, c):
        for kq in range(TOP_K):
            row_copy(t, kq).start(priority=kq % 2)
        return c

    def drain(t, c):
        for kq in range(TOP_K):
            row_copy(t, kq).wait()
        return c

    lax.fori_loop(0, tm, issue, 0, unroll=8)
    lax.fori_loop(0, tm, drain, 0, unroll=8)

    @pl.when(i == 0)
    def _():
        pads(lambda cp: cp.wait())


def _dispatch(dest, pad_start, pad_n, n_used, h2, n_pad, tm=256):
    m, ds, dl = h2.shape
    nt = m // tm
    grid_spec = pltpu.PrefetchScalarGridSpec(
        num_scalar_prefetch=3,
        grid=(nt,),
        in_specs=[
            pl.BlockSpec((TOP_K, tm), lambda i, ps, pn, nu: (0, i), memory_space=pltpu.SMEM),
            pl.BlockSpec((tm, ds, dl), lambda i, ps, pn, nu: (i, 0, 0)),
        ],
        out_specs=pl.BlockSpec(memory_space=pl.ANY),
        scratch_shapes=[pltpu.VMEM((MOE_BLOCK, ds, dl), F32)] + [pltpu.SemaphoreType.DMA(())] * 4,
    )
    return pl.pallas_call(
        _dispatch_kernel,
        grid_spec=grid_spec,
        out_shape=jax.ShapeDtypeStruct((n_pad, ds, dl), F32),
        compiler_params=_params(("arbitrary",)),
        name="moe_dispatch",
    )(pad_start, pad_n, n_used, dest, h2)


def _expert_kernel(blk_ref, be_ref, nu_ref, xs_ref, w1_ref, b1_ref, w2_ref, b2_ref, ys_ref,
                   w1b_ref, w2b_ref, *, d_e):
    del blk_ref
    i = pl.program_id(0)
    new_expert = jnp.logical_or(i == 0, be_ref[i] != be_ref[jnp.maximum(i - 1, 0)])

    @pl.when(new_expert)
    def _():
        w1b_ref[...] = w1_ref[...].astype(BF16)
        w2b_ref[...] = w2_ref[...].astype(BF16)

    @pl.when(i < nu_ref[0])
    def _():
        x = _load_tiles_as_rows(xs_ref, MOE_BLOCK).astype(BF16)
        u = _dot(x, w1b_ref[...]) + b1_ref[...]
        glu = jnp.minimum(u[:, :d_e], SWIGLU_LIMIT)
        lin = jnp.clip(u[:, d_e:], -SWIGLU_LIMIT, SWIGLU_LIMIT)
        act = glu * _sigmoid(SWIGLU_ALPHA * glu) * (lin + 1.0)
        _store_rows_as_tiles(ys_ref, _dot(act.astype(BF16), w2b_ref[...]) + b2_ref[...])

    @pl.when(i >= nu_ref[0])
    def _():
        ys_ref[...] = jnp.zeros_like(ys_ref)


def _experts(blk_idx, blk_e, n_used, xs, w1, b1, w2, b2):
    n_pad = xs.shape[0] // SUBLANES
    d = SUBLANES * LANES
    nb = n_pad // MOE_BLOCK
    n_e, _, d2 = w1.shape
    d_e = d2 // 2
    grid_spec = pltpu.PrefetchScalarGridSpec(
        num_scalar_prefetch=3,
        grid=(nb,),
        in_specs=[
            pl.BlockSpec((MOE_BLOCK * SUBLANES, LANES), lambda i, bi, be, nu: (bi[i], 0)),
            pl.BlockSpec((None, d, d2), lambda i, bi, be, nu: (be[i], 0, 0)),
            pl.BlockSpec((None, 1, d2), lambda i, bi, be, nu: (be[i], 0, 0)),
            pl.BlockSpec((None, d_e, d), lambda i, bi, be, nu: (be[i], 0, 0)),
            pl.BlockSpec((None, 1, d), lambda i, bi, be, nu: (be[i], 0, 0)),
        ],
        out_specs=pl.BlockSpec((MOE_BLOCK * SUBLANES, LANES), lambda i, bi, be, nu: (i, 0)),
        scratch_shapes=[pltpu.VMEM((d, d2), BF16), pltpu.VMEM((d_e, d), BF16)],
    )
    return pl.pallas_call(
        functools.partial(_expert_kernel, d_e=d_e),
        grid_spec=grid_spec,
        out_shape=jax.ShapeDtypeStruct((n_pad * SUBLANES, LANES), F32),
        compiler_params=_params(("arbitrary",)),
        name="moe_experts",
    )(blk_idx, blk_e, n_used, xs, w1, b1.reshape(n_e, 1, d2), w2, b2.reshape(n_e, 1, d))


def _combine_kernel(dest_ref, dnext_ref, ys_ref, x1_ref, gw_ref, o_ref, buf, sems):
    i = pl.program_id(0)
    nt = pl.num_programs(0)
    tm = x1_ref.shape[0]
    slot = i % 2

    def row_copy(d_ref, sl, t, kq):
        return pltpu.make_async_copy(
            ys_ref.at[d_ref[kq, t]],
            buf.at[sl, kq, pl.ds(pl.multiple_of(t * SUBLANES, SUBLANES), SUBLANES)], sems.at[sl])

    def issue_all(d_ref, sl):
        def body(t, c):
            for kq in range(TOP_K):
                row_copy(d_ref, sl, t, kq).start(priority=kq % 2)
            return c
        lax.fori_loop(0, tm, body, 0, unroll=8)

    @pl.when(i == 0)
    def _():
        issue_all(dest_ref, 0)

    @pl.when(i + 1 < nt)
    def _():
        issue_all(dnext_ref, 1 - slot)

    def drain(t, c):
        for kq in range(TOP_K):
            row_copy(dest_ref, slot, t, kq).wait()
        return c

    lax.fori_loop(0, tm, drain, 0, unroll=8)
    gw = gw_ref[...]
    for s in range(SUBLANES):
        acc = x1_ref[:, s * LANES:(s + 1) * LANES]
        for kq in range(TOP_K):
            acc = acc + gw[:, kq:kq + 1] * buf[slot, kq, pl.ds(s, tm, stride=SUBLANES), :]
        o_ref[:, s * LANES:(s + 1) * LANES] = acc


def _combine(dest, ys, x1, gw, tm=256):
    m, d = x1.shape
    nt = m // tm
    return pl.pallas_call(
        _combine_kernel,
        grid=(nt,),
        in_specs=[
            pl.BlockSpec((TOP_K, tm), lambda i: (0, i), memory_space=pltpu.SMEM),
            pl.BlockSpec((TOP_K, tm), lambda i: (0, jnp.minimum(i + 1, nt - 1)), memory_space=pltpu.SMEM),
            pl.BlockSpec(memory_space=pl.ANY),
            pl.BlockSpec((tm, d), lambda i: (i, 0)),
            pl.BlockSpec((tm, LANES), lambda i: (i, 0)),
        ],
        out_specs=pl.BlockSpec((tm, d), lambda i: (i, 0)),
        out_shape=jax.ShapeDtypeStruct((m, d), F32),
        scratch_shapes=[pltpu.VMEM((2, TOP_K, tm * SUBLANES, LANES), F32), pltpu.SemaphoreType.DMA((2,))],
        compiler_params=_params(("arbitrary",)),
        name="moe_combine",
    )(dest, dest, ys, x1, gw)


def _blockdiag2(a, b):
    za = jnp.zeros((a.shape[0], b.shape[1]), a.dtype)
    zb = jnp.zeros((b.shape[0], a.shape[1]), a.dtype)
    return jnp.concatenate([jnp.concatenate([a, za], axis=1), jnp.concatenate([zb, b], axis=1)], axis=0)


def _layer(x, g_mix, w_in, mu_prev, mu_next, w0_f, w2_f, w0_b, w2_b, a0_f, a2_f, a0_b, a2_b,
           g2, k_k, k_a, r_k, lnx_g, lnx_b, q_norm_g, k_norm_g, rpb, w_a, w_b, w_o,
           g_ffn, w_router, b_router, w1, b1, w2, b2):
    bsz, t, d_model = x.shape
    m = bsz * t
    d_a = w_a.shape[0]
    d_b = w_b.shape[0]
    a_cols = mu_prev.shape[0]
    b_cols = 3 * d_b
    row = lambda a: a.reshape(1, -1).astype(F32)

    w_in_b = w_in.astype(BF16)
    pa, pb, gates = _inproj(x.reshape(m, d_model), row(g_mix), w_in_b[:, :a_cols],
                            w_in_b[:, a_cols:a_cols + b_cols], w_in_b[:, a_cols + b_cols:])

    prep = _prep(pa.reshape(bsz, t, a_cols), row(mu_prev), row(mu_next),
                 jnp.concatenate([row(w0_f), row(w0_b)], axis=1), _blockdiag2(w2_f, w2_b),
                 jnp.concatenate([row(a0_f), row(a0_b)], axis=1), _blockdiag2(a2_f, a2_b),
                 g2, row(k_k), row(k_a), row(r_k), d_a)
    r, v, kk, lw_f, lw_b, k_f, k_b, b_f, b_b, bonus, g = prep
    o_f, o_b = _scan(r, v, kk, lw_f, lw_b, k_f, k_b, b_f, b_b)

    yb = _na(pb.reshape(bsz, t, b_cols), q_norm_g, k_norm_g, _na_bias_table(rpb), d_b)

    n_e = w_router.shape[1]
    wr_pad = jnp.zeros((d_model, LANES), F32).at[:, :n_e].set(w_router)
    br_pad = jnp.full((1, LANES), NEG_BIG, F32).at[0, :n_e].set(b_router)
    x1, h2, idx, rank, gw, cnt = _merge(
        o_f, o_b, bonus, g, yb, gates.reshape(bsz, t, 2 * d_model), x,
        row(lnx_g), row(lnx_b), w_a.astype(BF16), w_b.astype(BF16), w_o.astype(BF16),
        row(g_ffn), wr_pad, br_pad)

    counts = cnt[0, :n_e].astype(jnp.int32)
    padded = ((counts + MOE_BLOCK - 1) // MOE_BLOCK) * MOE_BLOCK
    pend = jnp.cumsum(padded)
    pstart = pend - padded
    n_assign = m * TOP_K
    n_blocks = -(-n_assign // MOE_BLOCK) + n_e
    n_pad = n_blocks * MOE_BLOCK
    dest = (pstart[idx[:TOP_K]] + rank[:TOP_K]).astype(jnp.int32)
    n_used = (pend[-1] // MOE_BLOCK).astype(jnp.int32)
    blk_idx = jnp.minimum(jnp.arange(n_blocks, dtype=jnp.int32), n_used - 1)
    blk_e = jnp.sum((blk_idx[:, None] * MOE_BLOCK >= pend[None, :]).astype(jnp.int32), axis=1)
    blk_e = jnp.minimum(blk_e, n_e - 1)

    xs = _dispatch(dest, (pstart + counts).astype(jnp.int32), (padded - counts).astype(jnp.int32),
                   n_used.reshape(1), h2.reshape(m, SUBLANES, LANES), n_pad)
    ys = _experts(blk_idx, blk_e, n_used.reshape(1), xs.reshape(n_pad * SUBLANES, LANES), w1, b1, w2, b2)
    out = _combine(dest, ys.reshape(n_pad, SUBLANES, LANES), x1, gw)
    return out.reshape(bsz, t, d_model)


def kernel(x, g_mix, w_in, mu_prev, mu_next, w0_f, w2_f, w0_b, w2_b, a0_f, a2_f, a0_b, a2_b, g2, k_k, k_a, r_k, lnx_g, lnx_b, q_norm_g, k_norm_g, rpb, w_a, w_b, w_o, g_ffn, w_router, b_router, w1, b1, w2, b2):
    for l in range(g_mix.shape[0]):
        x = _layer(x, g_mix[l], w_in[l], mu_prev[l], mu_next[l], w0_f[l], w2_f[l], w0_b[l], w2_b[l],
                   a0_f[l], a2_f[l], a0_b[l], a2_b[l], g2[l], k_k[l], k_a[l], r_k[l], lnx_g[l], lnx_b[l],
                   q_norm_g[l], k_norm_g[l], rpb[l], w_a[l], w_b[l], w_o[l], g_ffn[l], w_router[l],
                   b_router[l], w1[l], b1[l], w2[l], b2[l])
    return x
```

```python
import functools
import math

import numpy as np
import jax
import jax.numpy as jnp
from jax import lax
from jax.experimental import pallas as pl
from jax.experimental.pallas import tpu as pltpu

F32 = jnp.float32
BF16 = jnp.bfloat16
ACT = jnp.bfloat16

LANES = 128
SUBLANES = 8
HEAD_DIM = 64
PAIR = 2 * HEAD_DIM
GRID_W = 64
NA_WR = 8
NA_WC = 16
W_LORA = 64
A_LORA = 64
G_LORA = 128
DECAY_SCALE = math.exp(-0.5)
GN_EPS = 64e-5
RMS_EPS = 1e-5
N_EXPERTS = 32
TOP_K = 4
MOE_BLOCK = 512
SWIGLU_LIMIT = 7.0
SWIGLU_ALPHA = 1.702
NEG_BIG = -1e30
CHUNK = 128
NA_ROWS_PER_STEP = 8
VMEM_LIMIT = 56 * 1024 * 1024


def _dot(a, b):
    return jnp.dot(a, b, preferred_element_type=F32)


def _dot_nt(a, b):
    return lax.dot_general(a, b, (((1,), (1,)), ((), ())), preferred_element_type=F32)


def _split(a):
    hi = a.astype(BF16)
    lo = (a - hi.astype(F32)).astype(BF16)
    return hi, lo


def _mm_exact_rhs(a, b_bf16):
    hi, lo = _split(a)
    return _dot(hi, b_bf16) + _dot(lo, b_bf16)


def _hilo(b):
    return jnp.stack(_split(b))


def _mm3(a, b_hilo):
    ah, al = _split(a)
    bh = b_hilo[0]
    return _dot(ah, bh) + _dot(al, bh) + _dot(ah, b_hilo[1])


def _store_rows_as_tiles(ref, val):
    n = val.shape[0]
    for s in range(SUBLANES):
        ref[pl.ds(s, n, stride=SUBLANES), :] = val[:, s * LANES:(s + 1) * LANES]


def _load_tiles_as_rows(ref, n):
    return jnp.concatenate([ref[pl.ds(s, n, stride=SUBLANES), :] for s in range(SUBLANES)], axis=1)


def _sigmoid(x):
    return 1.0 / (1.0 + jnp.exp(-x))


def _params(sem):
    return pltpu.CompilerParams(dimension_semantics=sem, vmem_limit_bytes=VMEM_LIMIT)


def _block_ones(n, blk):
    i = np.arange(n) // blk
    return jnp.asarray(i[:, None] == i[None, :], BF16)


def _inproj_kernel(x_ref, g_ref, wa_ref, wb_ref, wg_ref, pa_ref, pb_ref, pg_ref):
    x = x_ref[...]
    ms = jnp.mean(x * x, axis=-1, keepdims=True)
    h = (x * lax.rsqrt(ms + RMS_EPS) * g_ref[...]).astype(BF16)
    pa_ref[...] = _dot(h, wa_ref[...]).astype(pa_ref.dtype)
    pb_ref[...] = _dot(h, wb_ref[...]).astype(pb_ref.dtype)
    pg_ref[...] = _dot(h, wg_ref[...]).astype(pg_ref.dtype)


def _inproj(x2, g_mix, w_a, w_b, w_g, tm=256):
    m, d = x2.shape
    na, nb, ng = w_a.shape[1], w_b.shape[1], w_g.shape[1]
    full = lambda i: (0, 0)
    return pl.pallas_call(
        _inproj_kernel,
        grid=(m // tm,),
        in_specs=[
            pl.BlockSpec((tm, d), lambda i: (i, 0)),
            pl.BlockSpec((1, d), full),
            pl.BlockSpec((d, na), full),
            pl.BlockSpec((d, nb), full),
            pl.BlockSpec((d, ng), full),
        ],
        out_specs=[
            pl.BlockSpec((tm, na), lambda i: (i, 0)),
            pl.BlockSpec((tm, nb), lambda i: (i, 0)),
            pl.BlockSpec((tm, ng), lambda i: (i, 0)),
        ],
        out_shape=[
            jax.ShapeDtypeStruct((m, na), ACT),
            jax.ShapeDtypeStruct((m, nb), ACT),
            jax.ShapeDtypeStruct((m, ng), ACT),
        ],
        compiler_params=_params(("arbitrary",)),
        name="inproj",
    )(x2, g_mix, w_a, w_b, w_g)


def _prep_kernel(p_ref, prev_ref, next_ref, mup_ref, mun_ref, w0_ref, w2_ref, a0_ref, a2_ref,
                 g2_ref, kk_ref, ka_ref, rk_ref, ones_ref,
                 r_o, v_o, kk_o, lwf_o, lwb_o, kf_o, kb_o, bf_o, bb_o, bonus_o, g_o, *, d_a):
    i = pl.program_id(1)
    n_t = pl.num_programs(1)
    p = p_ref[...].astype(F32)
    tt = p.shape[0]
    halo = prev_ref.shape[0]
    prow = jnp.where(i > 0, prev_ref[halo - 1:halo, :].astype(F32), 0.0)
    nrow = jnp.where(i < n_t - 1, next_ref[0:1, :].astype(F32), 0.0)
    rid = lax.broadcasted_iota(jnp.int32, (tt, 1), 0)
    prev = jnp.where(rid == 0, prow, pltpu.roll(p, 1, axis=0))
    nxt = jnp.where(rid == tt - 1, nrow, pltpu.roll(p, tt - 1, axis=0))
    xa = p + mup_ref[...] * (prev - p) + mun_ref[...] * (nxt - p)

    r = xa[:, 0:d_a]
    k = xa[:, d_a:2 * d_a]
    v = xa[:, 2 * d_a:3 * d_a]
    o = 3 * d_a
    lw = xa[:, o:o + 2 * W_LORA]
    la = xa[:, o + 2 * W_LORA:o + 2 * W_LORA + 2 * A_LORA]
    lg = xa[:, o + 2 * W_LORA + 2 * A_LORA:]

    dpre = w0_ref[...] + _mm3(jnp.tanh(lw), w2_ref)
    apre = a0_ref[...] + _mm3(la, a2_ref)
    g = _mm3(_sigmoid(lg), g2_ref)
    logw = -DECAY_SCALE * _sigmoid(dpre)
    a = _sigmoid(apre)

    ones = ones_ref[...]
    kkr = k * kk_ref[...]
    ss = _mm_exact_rhs(kkr * kkr, ones)
    kk = kkr / jnp.maximum(jnp.sqrt(ss), 1e-12)

    ka = ka_ref[...]
    k_f = k * (1.0 + (a[:, :d_a] - 1.0) * ka)
    k_b = k * (1.0 + (a[:, d_a:] - 1.0) * ka)
    b_f = kk * a[:, :d_a]
    b_b = kk * a[:, d_a:]
    rk = rk_ref[...]
    bon = _mm_exact_rhs(r * (k_f + k_b) * rk, ones) * v

    bonus_o[...] = bon.astype(bonus_o.dtype)
    g_o[...] = g.astype(g_o.dtype)
    for pi in range(d_a // PAIR):
        sl = slice(pi * PAIR, (pi + 1) * PAIR)
        r_o[pi] = r[:, sl].astype(r_o.dtype)
        v_o[pi] = v[:, sl].astype(v_o.dtype)
        kk_o[pi] = kk[:, sl].astype(kk_o.dtype)
        lwf_o[pi] = logw[:, sl]
        lwb_o[pi] = logw[:, d_a + pi * PAIR:d_a + (pi + 1) * PAIR]
        kf_o[pi] = k_f[:, sl].astype(kf_o.dtype)
        kb_o[pi] = k_b[:, sl].astype(kb_o.dtype)
        bf_o[pi] = b_f[:, sl].astype(bf_o.dtype)
        bb_o[pi] = b_b[:, sl].astype(bb_o.dtype)


def _prep(pa3, mu_prev, mu_next, w0c, w2blk, a0c, a2blk, g2, k_k, k_a, r_k, d_a, tt=256):
    b, t, ac = pa3.shape
    n_t = t // tt
    n_p = d_a // PAIR
    ones = _block_ones(d_a, HEAD_DIM)
    c2 = lambda bi, i: (0, 0)
    c3 = lambda bi, i: (0, 0, 0)
    pair_spec = pl.BlockSpec((None, n_p, tt, PAIR), lambda bi, i: (bi, 0, i, 0))
    pair_act = jax.ShapeDtypeStruct((b, n_p, t, PAIR), ACT)
    pair_f32 = jax.ShapeDtypeStruct((b, n_p, t, PAIR), F32)
    flat_spec = pl.BlockSpec((None, tt, d_a), lambda bi, i: (bi, i, 0))
    flat_shape = jax.ShapeDtypeStruct((b, t, d_a), ACT)
    halo = 16
    r8 = tt // halo
    return pl.pallas_call(
        functools.partial(_prep_kernel, d_a=d_a),
        grid=(b, n_t),
        in_specs=[
            pl.BlockSpec((None, tt, ac), lambda bi, i: (bi, i, 0)),
            pl.BlockSpec((None, halo, ac), lambda bi, i: (bi, jnp.maximum(i * r8 - 1, 0), 0)),
            pl.BlockSpec((None, halo, ac), lambda bi, i: (bi, jnp.minimum((i + 1) * r8, t // halo - 1), 0)),
            pl.BlockSpec((1, ac), c2),
            pl.BlockSpec((1, ac), c2),
            pl.BlockSpec((1, 2 * d_a), c2),
            pl.BlockSpec((2, 2 * W_LORA, 2 * d_a), c3),
            pl.BlockSpec((1, 2 * d_a), c2),
            pl.BlockSpec((2, 2 * A_LORA, 2 * d_a), c3),
            pl.BlockSpec((2, G_LORA, d_a), c3),
            pl.BlockSpec((1, d_a), c2),
            pl.BlockSpec((1, d_a), c2),
            pl.BlockSpec((1, d_a), c2),
            pl.BlockSpec((d_a, d_a), c2),
        ],
        out_specs=[pair_spec] * 9 + [flat_spec] * 2,
        out_shape=[pair_act] * 3 + [pair_f32] * 2 + [pair_act] * 4 + [flat_shape] * 2,
        compiler_params=_params(("arbitrary", "arbitrary")),
        name="rwkv_prep",
    )(pa3, pa3, pa3, mu_prev, mu_next, w0c, _hilo(w2blk), a0c, _hilo(a2blk), _hilo(g2), k_k, k_a, r_k, ones)


def _mm_exact_rhs_left(tri_bf16, x):
    hi, lo = _split(x)
    return _dot(tri_bf16, hi) + _dot(tri_bf16, lo)


def _scan_stage(items, s_refs):
    c = items[0][0].shape[0]
    ri = lax.broadcasted_iota(jnp.int32, (c, c), 0)
    ci = lax.broadcasted_iota(jnp.int32, (c, c), 1)
    lane = lax.broadcasted_iota(jnp.int32, (1, PAIR), 1)
    m0 = (lane < HEAD_DIM).astype(F32)
    m1 = 1.0 - m0
    eye = (ri == ci).astype(F32)
    hi_ = lax.broadcasted_iota(jnp.int32, (PAIR, PAIR), 0) // HEAD_DIM
    hj_ = lax.broadcasted_iota(jnp.int32, (PAIR, PAIR), 1) // HEAD_DIM
    same_head = hi_ == hj_
    zero = jnp.zeros((), F32)
    n = len(items)

    def masks(reverse):
        if reverse:
            return ci >= ri, ci > ri, 0
        return ci <= ri, ci < ri, c - 1

    cums = [_mm_exact_rhs_left(masks(it[6])[0].astype(BF16), it[3]) for it in items]

    pre = []
    for (r, kk, v, lw, k, b, reverse), cum in zip(items, cums):
        end = masks(reverse)[2]
        cmid = cum[c // 2:c // 2 + 1, :]
        cend = cum[end:end + 1, :]
        r_abs = r * jnp.exp(cum)
        a_abs = -kk * jnp.exp(cum - lw)
        to_mid = jnp.exp(-cmid)
        from_mid = jnp.exp(cmid - cum)
        to_end = jnp.exp(cend - cum)
        pre.append(dict(
            r_abs=r_abs, a_abs=a_abs, r_rel=r_abs * to_mid, a_rel=a_abs * to_mid,
            k_rel=k * from_mid, b_rel=b * from_mid, k_end=k * to_end, b_end=b * to_end,
            d_tot=jnp.exp(cend), v=v))

    grams = []
    for p in pre:
        lhs = jnp.concatenate([p["r_rel"] * m0, p["r_rel"] * m1, p["a_rel"] * m0, p["a_rel"] * m1],
                              axis=0).astype(BF16)
        rhs = jnp.concatenate([p["k_rel"], p["b_rel"]], axis=0).astype(BF16)
        grams.append(_dot_nt(lhs, rhs))

    pws, ts = [], []
    for it, gram in zip(items, grams):
        strict = masks(it[6])[1]
        for e in range(2):
            a_ab = jnp.where(strict, gram[(2 + e) * c:(3 + e) * c, c:2 * c], zero)
            pws.append(a_ab)
            ts.append(eye + a_ab)
    for _ in range(int(round(math.log2(c))) - 1):
        pws = [_dot(pw.astype(BF16), pw.astype(BF16)) for pw in pws]
        ts = [t + _dot(t.astype(BF16), pw.astype(BF16)) for t, pw in zip(ts, pws)]

    v_blks = [jnp.concatenate([p["v"] * m0, p["v"] * m1], axis=0).astype(BF16) for p in pre]
    akvs, o_rks, rb_cats = [], [], []
    for it, gram, v_blk in zip(items, grams, v_blks):
        incl, strict, _ = masks(it[6])
        ak_cat = jnp.concatenate(
            [jnp.where(strict, gram[(2 + e) * c:(3 + e) * c, 0:c], zero) for e in range(2)], axis=1)
        rk_cat = jnp.concatenate(
            [jnp.where(incl, gram[e * c:(e + 1) * c, 0:c], zero) for e in range(2)], axis=1)
        rb_cats.append(jnp.concatenate(
            [jnp.where(incl, gram[e * c:(e + 1) * c, c:2 * c], zero) for e in range(2)], axis=1).astype(BF16))
        akvs.append(_dot(ak_cat.astype(BF16), v_blk))
        o_rks.append(_dot(rk_cat.astype(BF16), v_blk))

    xs = []
    for i, (p, akv) in enumerate(zip(pre, akvs)):
        t_cat = jnp.concatenate([ts[2 * i], ts[2 * i + 1]], axis=1).astype(BF16)
        y_blk = jnp.concatenate([
            jnp.concatenate([p["a_abs"] * m0, akv * m0], axis=1),
            jnp.concatenate([p["a_abs"] * m1, akv * m1], axis=1)], axis=0).astype(BF16)
        xs.append(_dot(t_cat, y_blk))

    s0s = [s_ref[...] for s_ref in s_refs]
    s0bs = [s0.astype(BF16) for s0 in s0s]
    us = [_dot(x[:, 0:PAIR].astype(BF16), s0b) + x[:, PAIR:2 * PAIR] for x, s0b in zip(xs, s0bs)]
    outs = []
    for i in range(n):
        p, u = pre[i], us[i]
        u_blk = jnp.concatenate([u * m0, u * m1], axis=0).astype(BF16)
        outs.append(_dot(p["r_abs"].astype(BF16), s0bs[i]) + _dot(rb_cats[i], u_blk) + o_rks[i])
        kb_t = jnp.concatenate([p["b_end"].T, p["k_end"].T], axis=1).astype(BF16)
        uv = jnp.concatenate([u, p["v"]], axis=0).astype(BF16)
        d_col = jnp.broadcast_to(p["d_tot"], (PAIR, PAIR)).T
        s_refs[i][...] = jnp.where(same_head, d_col * s0s[i] + _dot(kb_t, uv), zero)
    return outs


def _scan_kernel(rf, vf, kkf, lwf, kf, bf, rb, vb, kkb, lwb, kb, bb, of_ref, ob_ref, s_ref):
    @pl.when(pl.program_id(1) == 0)
    def _():
        s_ref[...] = jnp.zeros_like(s_ref)

    n_p = rf.shape[0]
    items, s_refs = [], []
    for p in range(n_p):
        ld = lambda ref: ref[p].astype(F32)
        items.append((ld(rf), ld(kkf), ld(vf), lwf[p], ld(kf), ld(bf), False))
        s_refs.append(s_ref.at[2 * p])
        items.append((ld(rb), ld(kkb), ld(vb), lwb[p], ld(kb), ld(bb), True))
        s_refs.append(s_ref.at[2 * p + 1])
    outs = _scan_stage(items, s_refs)
    for p in range(n_p):
        of_ref[p] = outs[2 * p].astype(of_ref.dtype)
        ob_ref[p] = outs[2 * p + 1].astype(ob_ref.dtype)


def _scan(r, v, kk, lw_f, lw_b, k_f, k_b, b_f, b_b):
    bsz, n_p, t, _ = r.shape
    nc = t // CHUNK
    fwd = pl.BlockSpec((None, n_p, CHUNK, PAIR), lambda bi, c: (bi, 0, c, 0))
    bwd = pl.BlockSpec((None, n_p, CHUNK, PAIR), lambda bi, c: (bi, 0, nc - 1 - c, 0))
    shape = jax.ShapeDtypeStruct((bsz, n_p, t, PAIR), ACT)
    return pl.pallas_call(
        _scan_kernel,
        grid=(bsz, nc),
        in_specs=[fwd] * 6 + [bwd] * 6,
        out_specs=[fwd, bwd],
        out_shape=[shape, shape],
        scratch_shapes=[pltpu.VMEM((2 * n_p, PAIR, PAIR), F32)],
        compiler_params=_params(("arbitrary", "arbitrary")),
        name="rwkv_scan",
    )(r, v, kk, lw_f, k_f, b_f, r, v, kk, lw_b, k_b, b_b)


def _na_kernel(q_ref, k_ref, v_ref, gq_ref, gk_ref, tab_ref, ones_ref, o_ref, qn_s, kn_s, vb_s, *, rows):
    ones = ones_ref[...]
    scale = HEAD_DIM ** -0.5
    q = q_ref[...].astype(F32)
    k = k_ref[...].astype(F32)
    inv_d = 1.0 / HEAD_DIM
    qn = q * lax.rsqrt(_mm_exact_rhs(q * q, ones) * inv_d + RMS_EPS) * (gq_ref[...] * scale)
    kn = k * lax.rsqrt(_mm_exact_rhs(k * k, ones) * inv_d + RMS_EPS) * gk_ref[...]
    qn_s[...] = qn.astype(BF16)
    kn_s[...] = kn.astype(BF16)
    vb_s[...] = v_ref[...].astype(BF16)
    lane = lax.broadcasted_iota(jnp.int32, (1, PAIR), 1)
    head0 = lane < HEAD_DIM
    win = NA_WR * GRID_W

    def row_group(gi, carry):
        rws = [gi * NA_ROWS_PER_STEP + j for j in range(NA_ROWS_PER_STEP)]
        rss = [jnp.clip(r - NA_WR // 2, 0, rows - NA_WR) for r in rws]
        q_rows = [qn_s[pl.ds(pl.multiple_of(r * GRID_W, GRID_W), GRID_W), :] for r in rws]
        k_wins = [kn_s[pl.ds(pl.multiple_of(rs * GRID_W, GRID_W), win), :] for rs in rss]
        v_wins = [vb_s[pl.ds(pl.multiple_of(rs * GRID_W, GRID_W), win), :] for rs in rss]
        ss = []
        for j in range(NA_ROWS_PER_STEP):
            d0 = rss[j] - rws[j] + NA_WR - 1
            for e in range(2):
                mask = head0 if e == 0 else jnp.logical_not(head0)
                qm = jnp.where(mask, q_rows[j], jnp.zeros_like(q_rows[j]))
                bias = jnp.concatenate(
                    [tab_ref[e, pl.ds(d0 + 2 * m, 1)][0] for m in range(NA_WR // 2)], axis=1)
                ss.append(_dot_nt(qm, k_wins[j]) + bias)
        mxs = [jnp.max(s, axis=-1, keepdims=True) for s in ss]
        ps = [jnp.exp(s - mx) for s, mx in zip(ss, mxs)]
        ls = [jnp.sum(p, axis=-1, keepdims=True) for p in ps]
        pvs = [_dot(p.astype(BF16), v_wins[i // 2]) for i, p in enumerate(ps)]
        for j in range(NA_ROWS_PER_STEP):
            o0 = pvs[2 * j] / ls[2 * j]
            o1 = pvs[2 * j + 1] / ls[2 * j + 1]
            o_ref[pl.ds(pl.multiple_of(rws[j] * GRID_W, GRID_W), GRID_W), :] = (
                jnp.where(head0, o0, o1).astype(o_ref.dtype))
        return carry

    lax.fori_loop(0, rows // NA_ROWS_PER_STEP, row_group, 0)


def _na_bias_table(rpb):
    qc = np.arange(GRID_W)
    kc = np.arange(GRID_W)
    cs = np.clip(qc - NA_WC // 2, 0, GRID_W - NA_WC)
    valid = (kc[None, :] >= cs[:, None]) & (kc[None, :] < cs[:, None] + NA_WC)
    dc = np.clip(kc[None, :] - qc[:, None] + NA_WC - 1, 0, 2 * NA_WC - 2)
    n_dc = 2 * NA_WC - 1
    pick = jnp.asarray(dc.reshape(1, -1) == np.arange(n_dc).reshape(-1, 1), F32)
    b = jnp.dot(rpb.astype(F32).reshape(-1, n_dc), pick, precision=lax.Precision.HIGHEST)
    b = b.reshape(rpb.shape[0], rpb.shape[1], GRID_W, GRID_W)
    b = jnp.where(jnp.asarray(valid)[None, None], b, NEG_BIG)
    return jnp.concatenate([b[:, :-1], b[:, 1:]], axis=-1)


def _na(pb3, q_gain, k_gain, table, d_b):
    bsz, t, _ = pb3.shape
    rows = t // GRID_W
    n_p = d_b // PAIR
    ones = _block_ones(PAIR, HEAD_DIM)
    gq = jnp.tile(q_gain.reshape(1, HEAD_DIM), (1, 2))
    gk = jnp.tile(k_gain.reshape(1, HEAD_DIM), (1, 2))
    n_d = table.shape[1]
    c2 = lambda bi, p: (0, 0)
    return pl.pallas_call(
        functools.partial(_na_kernel, rows=rows),
        grid=(bsz, n_p),
        in_specs=[
            pl.BlockSpec((None, t, PAIR), lambda bi, p: (bi, 0, p)),
            pl.BlockSpec((None, t, PAIR), lambda bi, p: (bi, 0, n_p + p)),
            pl.BlockSpec((None, t, PAIR), lambda bi, p: (bi, 0, 2 * n_p + p)),
            pl.BlockSpec((1, PAIR), c2),
            pl.BlockSpec((1, PAIR), c2),
            pl.BlockSpec((2, n_d, GRID_W, PAIR), lambda bi, p: (p, 0, 0, 0)),
            pl.BlockSpec((PAIR, PAIR), c2),
        ],
        out_specs=pl.BlockSpec((None, t, PAIR), lambda bi, p: (bi, 0, p)),
        out_shape=jax.ShapeDtypeStruct((bsz, t, d_b), ACT),
        scratch_shapes=[pltpu.VMEM((t, PAIR), BF16)] * 3,
        compiler_params=_params(("arbitrary", "arbitrary")),
        name="natten",
    )(pb3, pb3, pb3, gq, gk, table, ones)


def _merge_kernel(of_ref, ob_ref, bonus_ref, g_ref, yb_ref, gates_ref, x_ref,
                  lng_ref, lnb_ref, wa_ref, wb_ref, wo_ref, gffn_ref, wr_ref, br_ref,
                  ones_ref, tri_ref,
                  x1_ref, h2_ref, idx_ref, rank_ref, gw_ref, cnt_ref, carry_ref, *, d_model):
    first = jnp.logical_and(pl.program_id(0) == 0, pl.program_id(1) == 0)

    @pl.when(first)
    def _():
        carry_ref[...] = jnp.zeros_like(carry_ref)

    n_p = of_ref.shape[0]
    o = jnp.concatenate([of_ref[p].astype(F32) + ob_ref[p].astype(F32) for p in range(n_p)],
                        axis=1)
    ones = ones_ref[...]
    inv_d = 1.0 / HEAD_DIM
    mu = _mm_exact_rhs(o, ones) * inv_d
    dv = o - mu
    var = _mm_exact_rhs(dv * dv, ones) * inv_d
    y = dv * lax.rsqrt(var + GN_EPS) * lng_ref[...] + lnb_ref[...] + bonus_ref[...].astype(F32)
    ya = y * g_ref[...].astype(F32)

    gates = gates_ref[...].astype(F32)
    pa = _dot(ya.astype(BF16), wa_ref[...])
    pb = _dot(yb_ref[...].astype(BF16), wb_ref[...])
    merged = _sigmoid(gates[:, :d_model]) * pa + _sigmoid(gates[:, d_model:]) * pb
    x1 = x_ref[...] + _dot(merged.astype(BF16), wo_ref[...])
    x1_ref[...] = x1
    ms = jnp.mean(x1 * x1, axis=-1, keepdims=True)
    h2 = x1 * lax.rsqrt(ms + RMS_EPS) * gffn_ref[...]
    _store_rows_as_tiles(h2_ref, h2)

    logits = _mm3(h2, wr_ref) + br_ref[...]
    tm = logits.shape[0]
    lane = lax.broadcasted_iota(jnp.int32, (tm, LANES), 1)
    work = logits
    vals, idxs = [], []
    for _ in range(TOP_K):
        m = jnp.max(work, axis=-1, keepdims=True)
        ix = jnp.min(jnp.where(work == m, lane, LANES), axis=-1, keepdims=True)
        vals.append(m)
        idxs.append(ix)
        work = jnp.where(lane == ix, -jnp.inf, work)
    es = [jnp.exp(vk - vals[0]) for vk in vals]
    den = es[0] + es[1] + es[2] + es[3]
    member = jnp.zeros((tm, LANES), F32)
    for ix in idxs:
        member = member + (lane == ix).astype(F32)
    before = _dot(tri_ref[...], member.astype(BF16)) + carry_ref[...]
    idx_out = jnp.zeros((tm, LANES), F32)
    rank_out = jnp.zeros((tm, LANES), F32)
    gw_out = jnp.zeros((tm, LANES), F32)
    for kq in range(TOP_K):
        rk = jnp.sum(jnp.where(lane == idxs[kq], before, 0.0), axis=-1, keepdims=True)
        sel = lane == kq
        idx_out = jnp.where(sel, idxs[kq].astype(F32), idx_out)
        rank_out = jnp.where(sel, rk, rank_out)
        gw_out = jnp.where(sel, es[kq] / den, gw_out)
    idx_ref[...] = idx_out.T[:SUBLANES].astype(jnp.int32)
    rank_ref[...] = rank_out.T[:SUBLANES].astype(jnp.int32)
    gw_ref[...] = gw_out
    carry_ref[...] = carry_ref[...] + jnp.sum(member, axis=0, keepdims=True)
    cnt_ref[...] = carry_ref[...]


def _merge(o_f, o_b, bonus, g, yb, gates3, x3, lnx_g, lnx_b, w_a, w_b, w_o, g_ffn, wr_pad, br_pad, tm=512):
    bsz, n_p, t, _ = o_f.shape
    d_a = n_p * PAIR
    d_b = yb.shape[-1]
    d_model = x3.shape[-1]
    n_t = t // tm
    m = bsz * t
    ones = _block_ones(d_a, HEAD_DIM)
    tri = jnp.asarray(np.tril(np.ones((tm, tm)), -1), BF16)
    c2 = lambda bi, i: (0, 0)
    tok = lambda w: pl.BlockSpec((None, tm, w), lambda bi, i: (bi, i, 0))
    flat = lambda w: pl.BlockSpec((tm, w), lambda bi, i: (bi * n_t + i, 0))
    pair = pl.BlockSpec((None, n_p, tm, PAIR), lambda bi, i: (bi, 0, i, 0))
    return pl.pallas_call(
        functools.partial(_merge_kernel, d_model=d_model),
        grid=(bsz, n_t),
        in_specs=[
            pair, pair, tok(d_a), tok(d_a), tok(d_b), tok(2 * d_model), tok(d_model),
            pl.BlockSpec((1, d_a), c2), pl.BlockSpec((1, d_a), c2),
            pl.BlockSpec((d_a, d_model), c2), pl.BlockSpec((d_b, d_model), c2),
            pl.BlockSpec((d_model, d_model), c2), pl.BlockSpec((1, d_model), c2),
            pl.BlockSpec((2, d_model, LANES), lambda bi, i: (0, 0, 0)), pl.BlockSpec((1, LANES), c2),
            pl.BlockSpec((d_a, d_a), c2), pl.BlockSpec((tm, tm), c2),
        ],
        out_specs=[flat(d_model),
                   pl.BlockSpec((tm * SUBLANES, LANES), lambda bi, i: (bi * n_t + i, 0)),
                   pl.BlockSpec((SUBLANES, tm), lambda bi, i: (0, bi * n_t + i)),
                   pl.BlockSpec((SUBLANES, tm), lambda bi, i: (0, bi * n_t + i)),
                   flat(LANES),
                   pl.BlockSpec((1, LANES), c2)],
        out_shape=[
            jax.ShapeDtypeStruct((m, d_model), F32),
            jax.ShapeDtypeStruct((m * SUBLANES, LANES), F32),
            jax.ShapeDtypeStruct((SUBLANES, m), jnp.int32),
            jax.ShapeDtypeStruct((SUBLANES, m), jnp.int32),
            jax.ShapeDtypeStruct((m, LANES), F32),
            jax.ShapeDtypeStruct((1, LANES), F32),
        ],
        scratch_shapes=[pltpu.VMEM((1, LANES), F32)],
        compiler_params=_params(("arbitrary", "arbitrary")),
        name="merge_router",
    )(o_f, o_b, bonus, g, yb, gates3, x3, lnx_g, lnx_b, w_a, w_b, w_o, g_ffn, _hilo(wr_pad), br_pad, ones, tri)


def _dispatch_kernel(ps_ref, pn_ref, nu_ref, dest_ref, h_ref, xs_ref, zbuf, sem, zsem, csem, bsem):
    i = pl.program_id(0)
    tm = h_ref.shape[0]
    n_e = ps_ref.shape[0]
    nb = xs_ref.shape[0] // MOE_BLOCK

    def row_copy(t, kq):
        return pltpu.make_async_copy(
            h_ref.at[t], xs_ref.at[dest_ref[kq, t]], sem)

    def pad_copy(r):
        return pltpu.make_async_copy(zbuf.at[0], xs_ref.at[r], zsem)

    def oct_copy(o):
        return pltpu.make_async_copy(
            zbuf.at[pl.ds(0, SUBLANES)], xs_ref.at[pl.ds(o * SUBLANES, SUBLANES)], csem)

    def blk_copy(b):
        return pltpu.make_async_copy(zbuf, xs_ref.at[pl.ds(b * MOE_BLOCK, MOE_BLOCK)], bsem)

    def pads(fn):
        def per_expert(e, c):
            start = ps_ref[e]
            end = start + pn_ref[e]
            first_oct = (start + SUBLANES - 1) // SUBLANES

            def single(r, c2):
                fn(pad_copy(r))
                return c2

            def octet(o, c2):
                fn(oct_copy(o))
                return c2
            c = lax.fori_loop(start, jnp.minimum(first_oct * SUBLANES, end), single, c)
            return lax.fori_loop(first_oct, end // SUBLANES, octet, c)
        lax.fori_loop(0, n_e, per_expert, 0)

        def per_block(b, c):
            fn(blk_copy(b))
            return c
        lax.fori_loop(nu_ref[0], nb, per_block, 0)

    @pl.when(i == 0)
    def _():
        zbuf[...] = jnp.zeros_like(zbuf)
        pads(lambda cp: cp.start())

    def issue(t, c):
        for kq in range(TOP_K):
            row_copy(t, kq).start(priority=kq % 2)
        return c

    def drain(t, c):
        for kq in range(TOP_K):
            row_copy(t, kq).wait()
        return c

    lax.fori_loop(0, tm, issue, 0, unroll=8)
    lax.fori_loop(0, tm, drain, 0, unroll=8)

    @pl.when(i == 0)
    def _():
        pads(lambda cp: cp.wait())


def _dispatch(dest, pad_start, pad_n, n_used, h2, n_pad, tm=256):
    m, ds, dl = h2.shape
    nt = m // tm
    grid_spec = pltpu.PrefetchScalarGridSpec(
        num_scalar_prefetch=3,
        grid=(nt,),
        in_specs=[
            pl.BlockSpec((TOP_K, tm), lambda i, ps, pn, nu: (0, i), memory_space=pltpu.SMEM),
            pl.BlockSpec((tm, ds, dl), lambda i, ps, pn, nu: (i, 0, 0)),
        ],
        out_specs=pl.BlockSpec(memory_space=pl.ANY),
        scratch_shapes=[pltpu.VMEM((MOE_BLOCK, ds, dl), F32)] + [pltpu.SemaphoreType.DMA(())] * 4,
    )
    return pl.pallas_call(
        _dispatch_kernel,
        grid_spec=grid_spec,
        out_shape=jax.ShapeDtypeStruct((n_pad, ds, dl), F32),
        compiler_params=_params(("arbitrary",)),
        name="moe_dispatch",
    )(pad_start, pad_n, n_used, dest, h2)


def _expert_kernel(blk_ref, be_ref, nu_ref, xs_ref, w1_ref, b1_ref, w2_ref, b2_ref, ys_ref,
                   w1b_ref, w2b_ref, *, d_e):
    del blk_ref
    i = pl.program_id(0)
    new_expert = jnp.logical_or(i == 0, be_ref[i] != be_ref[jnp.maximum(i - 1, 0)])

    @pl.when(new_expert)
    def _():
        w1b_ref[...] = w1_ref[...].astype(BF16)
        w2b_ref[...] = w2_ref[...].astype(BF16)

    @pl.when(i < nu_ref[0])
    def _():
        x = _load_tiles_as_rows(xs_ref, MOE_BLOCK).astype(BF16)
        u = _dot(x, w1b_ref[...]) + b1_ref[...]
        glu = jnp.minimum(u[:, :d_e], SWIGLU_LIMIT)
        lin = jnp.clip(u[:, d_e:], -SWIGLU_LIMIT, SWIGLU_LIMIT)
        act = glu * _sigmoid(SWIGLU_ALPHA * glu) * (lin + 1.0)
        _store_rows_as_tiles(ys_ref, _dot(act.astype(BF16), w2b_ref[...]) + b2_ref[...])

    @pl.when(i >= nu_ref[0])
    def _():
        ys_ref[...] = jnp.zeros_like(ys_ref)


def _experts(blk_idx, blk_e, n_used, xs, w1, b1, w2, b2):
    n_pad = xs.shape[0] // SUBLANES
    d = SUBLANES * LANES
    nb = n_pad // MOE_BLOCK
    n_e, _, d2 = w1.shape
    d_e = d2 // 2
    grid_spec = pltpu.PrefetchScalarGridSpec(
        num_scalar_prefetch=3,
        grid=(nb,),
        in_specs=[
            pl.BlockSpec((MOE_BLOCK * SUBLANES, LANES), lambda i, bi, be, nu: (bi[i], 0)),
            pl.BlockSpec((None, d, d2), lambda i, bi, be, nu: (be[i], 0, 0)),
            pl.BlockSpec((None, 1, d2), lambda i, bi, be, nu: (be[i], 0, 0)),
            pl.BlockSpec((None, d_e, d), lambda i, bi, be, nu: (be[i], 0, 0)),
            pl.BlockSpec((None, 1, d), lambda i, bi, be, nu: (be[i], 0, 0)),
        ],
        out_specs=pl.BlockSpec((MOE_BLOCK * SUBLANES, LANES), lambda i, bi, be, nu: (i, 0)),
        scratch_shapes=[pltpu.VMEM((d, d2), BF16), pltpu.VMEM((d_e, d), BF16)],
    )
    return pl.pallas_call(
        functools.partial(_expert_kernel, d_e=d_e),
        grid_spec=grid_spec,
        out_shape=jax.ShapeDtypeStruct((n_pad * SUBLANES, LANES), F32),
        compiler_params=_params(("arbitrary",)),
        name="moe_experts",
    )(blk_idx, blk_e, n_used, xs, w1, b1.reshape(n_e, 1, d2), w2, b2.reshape(n_e, 1, d))


def _combine_kernel(dest_ref, dnext_ref, ys_ref, x1_ref, gw_ref, o_ref, buf, sems):
    i = pl.program_id(0)
    nt = pl.num_programs(0)
    tm = x1_ref.shape[0]
    slot = i % 2

    def row_copy(d_ref, sl, t, kq):
        return pltpu.make_async_copy(
            ys_ref.at[d_ref[kq, t]],
            buf.at[sl, kq, pl.ds(pl.multiple_of(t * SUBLANES, SUBLANES), SUBLANES)], sems.at[sl])

    def issue_all(d_ref, sl):
        def body(t, c):
            for kq in range(TOP_K):
                row_copy(d_ref, sl, t, kq).start(priority=kq % 2)
            return c
        lax.fori_loop(0, tm, body, 0, unroll=8)

    @pl.when(i == 0)
    def _():
        issue_all(dest_ref, 0)

    @pl.when(i + 1 < nt)
    def _():
        issue_all(dnext_ref, 1 - slot)

    def drain(t, c):
        for kq in range(TOP_K):
            row_copy(dest_ref, slot, t, kq).wait()
        return c

    lax.fori_loop(0, tm, drain, 0, unroll=8)
    gw = gw_ref[...]
    for s in range(SUBLANES):
        acc = x1_ref[:, s * LANES:(s + 1) * LANES]
        for kq in range(TOP_K):
            acc = acc + gw[:, kq:kq + 1] * buf[slot, kq, pl.ds(s, tm, stride=SUBLANES), :]
        o_ref[:, s * LANES:(s + 1) * LANES] = acc


def _combine(dest, ys, x1, gw, tm=128):
    m, d = x1.shape
    nt = m // tm
    return pl.pallas_call(
        _combine_kernel,
        grid=(nt,),
        in_specs=[
            pl.BlockSpec((TOP_K, tm), lambda i: (0, i), memory_space=pltpu.SMEM),
            pl.BlockSpec((TOP_K, tm), lambda i: (0, jnp.minimum(i + 1, nt - 1)), memory_space=pltpu.SMEM),
            pl.BlockSpec(memory_space=pl.ANY),
            pl.BlockSpec((tm, d), lambda i: (i, 0)),
            pl.BlockSpec((tm, LANES), lambda i: (i, 0)),
        ],
        out_specs=pl.BlockSpec((tm, d), lambda i: (i, 0)),
        out_shape=jax.ShapeDtypeStruct((m, d), F32),
        scratch_shapes=[pltpu.VMEM((2, TOP_K, tm * SUBLANES, LANES), F32), pltpu.SemaphoreType.DMA((2,))],
        compiler_params=_params(("arbitrary",)),
        name="moe_combine",
    )(dest, dest, ys, x1, gw)


def _blockdiag2(a, b):
    za = jnp.zeros((a.shape[0], b.shape[1]), a.dtype)
    zb = jnp.zeros((b.shape[0], a.shape[1]), a.dtype)
    return jnp.concatenate([jnp.concatenate([a, za], axis=1), jnp.concatenate([zb, b], axis=1)], axis=0)


def _layer(x, g_mix, w_in, mu_prev, mu_next, w0_f, w2_f, w0_b, w2_b, a0_f, a2_f, a0_b, a2_b,
           g2, k_k, k_a, r_k, lnx_g, lnx_b, q_norm_g, k_norm_g, rpb, w_a, w_b, w_o,
           g_ffn, w_router, b_router, w1, b1, w2, b2):
    bsz, t, d_model = x.shape
    m = bsz * t
    d_a = w_a.shape[0]
    d_b = w_b.shape[0]
    a_cols = mu_prev.shape[0]
    b_cols = 3 * d_b
    row = lambda a: a.reshape(1, -1).astype(F32)

    w_in_b = w_in.astype(BF16)
    pa, pb, gates = _inproj(x.reshape(m, d_model), row(g_mix), w_in_b[:, :a_cols],
                            w_in_b[:, a_cols:a_cols + b_cols], w_in_b[:, a_cols + b_cols:])

    prep = _prep(pa.reshape(bsz, t, a_cols), row(mu_prev), row(mu_next),
                 jnp.concatenate([row(w0_f), row(w0_b)], axis=1), _blockdiag2(w2_f, w2_b),
                 jnp.concatenate([row(a0_f), row(a0_b)], axis=1), _blockdiag2(a2_f, a2_b),
                 g2, row(k_k), row(k_a), row(r_k), d_a)
    r, v, kk, lw_f, lw_b, k_f, k_b, b_f, b_b, bonus, g = prep
    o_f, o_b = _scan(r, v, kk, lw_f, lw_b, k_f, k_b, b_f, b_b)

    yb = _na(pb.reshape(bsz, t, b_cols), q_norm_g, k_norm_g, _na_bias_table(rpb), d_b)

    n_e = w_router.shape[1]
    wr_pad = jnp.zeros((d_model, LANES), F32).at[:, :n_e].set(w_router)
    br_pad = jnp.full((1, LANES), NEG_BIG, F32).at[0, :n_e].set(b_router)
    x1, h2, idx, rank, gw, cnt = _merge(
        o_f, o_b, bonus, g, yb, gates.reshape(bsz, t, 2 * d_model), x,
        row(lnx_g), row(lnx_b), w_a.astype(BF16), w_b.astype(BF16), w_o.astype(BF16),
        row(g_ffn), wr_pad, br_pad)

    counts = cnt[0, :n_e].astype(jnp.int32)
    padded = ((counts + MOE_BLOCK - 1) // MOE_BLOCK) * MOE_BLOCK
    pend = jnp.cumsum(padded)
    pstart = pend - padded
    n_assign = m * TOP_K
    n_blocks = -(-n_assign // MOE_BLOCK) + n_e
    n_pad = n_blocks * MOE_BLOCK
    dest = rank[:TOP_K]
    for e in range(n_e):
        dest = dest + jnp.where(idx[:TOP_K] == e, pstart[e], 0)
    dest = dest.astype(jnp.int32)
    n_used = (pend[-1] // MOE_BLOCK).astype(jnp.int32)
    blk_idx = jnp.minimum(jnp.arange(n_blocks, dtype=jnp.int32), n_used - 1)
    blk_e = jnp.sum((blk_idx[:, None] * MOE_BLOCK >= pend[None, :]).astype(jnp.int32), axis=1)
    blk_e = jnp.minimum(blk_e, n_e - 1)

    xs = _dispatch(dest, (pstart + counts).astype(jnp.int32), (padded - counts).astype(jnp.int32),
                   n_used.reshape(1), h2.reshape(m, SUBLANES, LANES), n_pad)
    ys = _experts(blk_idx, blk_e, n_used.reshape(1), xs.reshape(n_pad * SUBLANES, LANES), w1, b1, w2, b2)
    out = _combine(dest, ys.reshape(n_pad, SUBLANES, LANES), x1, gw)
    return out.reshape(bsz, t, d_model)


def kernel(x, g_mix, w_in, mu_prev, mu_next, w0_f, w2_f, w0_b, w2_b, a0_f, a2_f, a0_b, a2_b, g2, k_k, k_a, r_k, lnx_g, lnx_b, q_norm_g, k_norm_g, rpb, w_a, w_b, w_o, g_ffn, w_router, b_router, w1, b1, w2, b2):
    for l in range(g_mix.shape[0]):
        x = _layer(x, g_mix[l], w_in[l], mu_prev[l], mu_next[l], w0_f[l], w2_f[l], w0_b[l], w2_b[l],
                   a0_f[l], a2_f[l], a0_b[l], a2_b[l], g2[l], k_k[l], k_a[l], r_k[l], lnx_g[l], lnx_b[l],
                   q_norm_g[l], k_norm_g[l], rpb[l], w_a[l], w_b[l], w_o[l], g_ffn[l], w_router[l],
                   b_router[l], w1[l], b1[l], w2[l], b2[l])
    return x
```

```python
import functools
import math

import numpy as np
import jax
import jax.numpy as jnp
from jax import lax
from jax.experimental import pallas as pl
from jax.experimental.pallas import tpu as pltpu

F32 = jnp.float32
BF16 = jnp.bfloat16
ACT = jnp.bfloat16

LANES = 128
SUBLANES = 8
HEAD_DIM = 64
PAIR = 2 * HEAD_DIM
GRID_W = 64
NA_WR = 8
NA_WC = 16
W_LORA = 64
A_LORA = 64
G_LORA = 128
DECAY_SCALE = math.exp(-0.5)
GN_EPS = 64e-5
RMS_EPS = 1e-5
N_EXPERTS = 32
TOP_K = 4
MOE_BLOCK = 512
SWIGLU_LIMIT = 7.0
SWIGLU_ALPHA = 1.702
NEG_BIG = -1e30
CHUNK = 128
NA_ROWS_PER_STEP = 16
VMEM_LIMIT = 56 * 1024 * 1024


def _dot(a, b):
    return jnp.dot(a, b, preferred_element_type=F32)


def _dot_nt(a, b):
    return lax.dot_general(a, b, (((1,), (1,)), ((), ())), preferred_element_type=F32)


def _split(a):
    hi = a.astype(BF16)
    lo = (a - hi.astype(F32)).astype(BF16)
    return hi, lo


def _mm_exact_rhs(a, b_bf16):
    hi, lo = _split(a)
    return _dot(hi, b_bf16) + _dot(lo, b_bf16)


def _hilo(b):
    return jnp.stack(_split(b))


def _mm3(a, b_hilo):
    ah, al = _split(a)
    bh = b_hilo[0]
    return _dot(ah, bh) + _dot(al, bh) + _dot(ah, b_hilo[1])


def _store_rows_as_tiles(ref, val):
    n = val.shape[0]
    for s in range(SUBLANES):
        ref[pl.ds(s, n, stride=SUBLANES), :] = val[:, s * LANES:(s + 1) * LANES]


def _load_tiles_as_rows(ref, n):
    return jnp.concatenate([ref[pl.ds(s, n, stride=SUBLANES), :] for s in range(SUBLANES)], axis=1)


def _sigmoid(x):
    return 1.0 / (1.0 + jnp.exp(-x))


def _params(sem):
    return pltpu.CompilerParams(dimension_semantics=sem, vmem_limit_bytes=VMEM_LIMIT)


def _block_ones(n, blk):
    i = np.arange(n) // blk
    return jnp.asarray(i[:, None] == i[None, :], BF16)


def _inproj_kernel(x_ref, g_ref, wa_ref, wb_ref, wg_ref, pa_ref, pb_ref, pg_ref):
    x = x_ref[...]
    ms = jnp.mean(x * x, axis=-1, keepdims=True)
    h = (x * lax.rsqrt(ms + RMS_EPS) * g_ref[...]).astype(BF16)
    pa_ref[...] = _dot(h, wa_ref[...]).astype(pa_ref.dtype)
    pb_ref[...] = _dot(h, wb_ref[...]).astype(pb_ref.dtype)
    pg_ref[...] = _dot(h, wg_ref[...]).astype(pg_ref.dtype)


def _inproj(x2, g_mix, w_a, w_b, w_g, tm=512):
    m, d = x2.shape
    na, nb, ng = w_a.shape[1], w_b.shape[1], w_g.shape[1]
    full = lambda i: (0, 0)
    return pl.pallas_call(
        _inproj_kernel,
        grid=(m // tm,),
        in_specs=[
            pl.BlockSpec((tm, d), lambda i: (i, 0)),
            pl.BlockSpec((1, d), full),
            pl.BlockSpec((d, na), full),
            pl.BlockSpec((d, nb), full),
            pl.BlockSpec((d, ng), full),
        ],
        out_specs=[
            pl.BlockSpec((tm, na), lambda i: (i, 0)),
            pl.BlockSpec((tm, nb), lambda i: (i, 0)),
            pl.BlockSpec((tm, ng), lambda i: (i, 0)),
        ],
        out_shape=[
            jax.ShapeDtypeStruct((m, na), ACT),
            jax.ShapeDtypeStruct((m, nb), ACT),
            jax.ShapeDtypeStruct((m, ng), ACT),
        ],
        compiler_params=_params(("arbitrary",)),
        name="inproj",
    )(x2, g_mix, w_a, w_b, w_g)


def _prep_kernel(p_ref, prev_ref, next_ref, mup_ref, mun_ref, w0_ref, w2_ref, a0_ref, a2_ref,
                 g2_ref, kk_ref, ka_ref, rk_ref, ones_ref,
                 r_o, v_o, kk_o, lwf_o, lwb_o, kf_o, kb_o, bf_o, bb_o, bonus_o, g_o, *, d_a):
    i = pl.program_id(1)
    n_t = pl.num_programs(1)
    p = p_ref[...].astype(F32)
    tt = p.shape[0]
    halo = prev_ref.shape[0]
    prow = jnp.where(i > 0, prev_ref[halo - 1:halo, :].astype(F32), 0.0)
    nrow = jnp.where(i < n_t - 1, next_ref[0:1, :].astype(F32), 0.0)
    rid = lax.broadcasted_iota(jnp.int32, (tt, 1), 0)
    prev = jnp.where(rid == 0, prow, pltpu.roll(p, 1, axis=0))
    nxt = jnp.where(rid == tt - 1, nrow, pltpu.roll(p, tt - 1, axis=0))
    xa = p + mup_ref[...] * (prev - p) + mun_ref[...] * (nxt - p)

    r = xa[:, 0:d_a]
    k = xa[:, d_a:2 * d_a]
    v = xa[:, 2 * d_a:3 * d_a]
    o = 3 * d_a
    lw = xa[:, o:o + 2 * W_LORA]
    la = xa[:, o + 2 * W_LORA:o + 2 * W_LORA + 2 * A_LORA]
    lg = xa[:, o + 2 * W_LORA + 2 * A_LORA:]

    dpre = w0_ref[...] + _mm3(jnp.tanh(lw), w2_ref)
    apre = a0_ref[...] + _mm3(la, a2_ref)
    g = _mm3(_sigmoid(lg), g2_ref)
    logw = -DECAY_SCALE * _sigmoid(dpre)
    a = _sigmoid(apre)

    ones = ones_ref[...]
    kkr = k * kk_ref[...]
    ss = _mm_exact_rhs(kkr * kkr, ones)
    kk = kkr / jnp.maximum(jnp.sqrt(ss), 1e-12)

    ka = ka_ref[...]
    k_f = k * (1.0 + (a[:, :d_a] - 1.0) * ka)
    k_b = k * (1.0 + (a[:, d_a:] - 1.0) * ka)
    b_f = kk * a[:, :d_a]
    b_b = kk * a[:, d_a:]
    rk = rk_ref[...]
    bon = _mm_exact_rhs(r * (k_f + k_b) * rk, ones) * v

    bonus_o[...] = bon.astype(bonus_o.dtype)
    g_o[...] = g.astype(g_o.dtype)
    for pi in range(d_a // PAIR):
        sl = slice(pi * PAIR, (pi + 1) * PAIR)
        r_o[pi] = r[:, sl].astype(r_o.dtype)
        v_o[pi] = v[:, sl].astype(v_o.dtype)
        kk_o[pi] = kk[:, sl].astype(kk_o.dtype)
        lwf_o[pi] = logw[:, sl]
        lwb_o[pi] = logw[:, d_a + pi * PAIR:d_a + (pi + 1) * PAIR]
        kf_o[pi] = k_f[:, sl].astype(kf_o.dtype)
        kb_o[pi] = k_b[:, sl].astype(kb_o.dtype)
        bf_o[pi] = b_f[:, sl].astype(bf_o.dtype)
        bb_o[pi] = b_b[:, sl].astype(bb_o.dtype)


def _prep(pa3, mu_prev, mu_next, w0c, w2blk, a0c, a2blk, g2, k_k, k_a, r_k, d_a, tt=256):
    b, t, ac = pa3.shape
    n_t = t // tt
    n_p = d_a // PAIR
    ones = _block_ones(d_a, HEAD_DIM)
    c2 = lambda bi, i: (0, 0)
    c3 = lambda bi, i: (0, 0, 0)
    pair_spec = pl.BlockSpec((None, n_p, tt, PAIR), lambda bi, i: (bi, 0, i, 0))
    pair_act = jax.ShapeDtypeStruct((b, n_p, t, PAIR), ACT)
    pair_f32 = jax.ShapeDtypeStruct((b, n_p, t, PAIR), F32)
    flat_spec = pl.BlockSpec((None, tt, d_a), lambda bi, i: (bi, i, 0))
    flat_shape = jax.ShapeDtypeStruct((b, t, d_a), ACT)
    halo = 16
    r8 = tt // halo
    return pl.pallas_call(
        functools.partial(_prep_kernel, d_a=d_a),
        grid=(b, n_t),
        in_specs=[
            pl.BlockSpec((None, tt, ac), lambda bi, i: (bi, i, 0)),
            pl.BlockSpec((None, halo, ac), lambda bi, i: (bi, jnp.maximum(i * r8 - 1, 0), 0)),
            pl.BlockSpec((None, halo, ac), lambda bi, i: (bi, jnp.minimum((i + 1) * r8, t // halo - 1), 0)),
            pl.BlockSpec((1, ac), c2),
            pl.BlockSpec((1, ac), c2),
            pl.BlockSpec((1, 2 * d_a), c2),
            pl.BlockSpec((2, 2 * W_LORA, 2 * d_a), c3),
            pl.BlockSpec((1, 2 * d_a), c2),
            pl.BlockSpec((2, 2 * A_LORA, 2 * d_a), c3),
            pl.BlockSpec((2, G_LORA, d_a), c3),
            pl.BlockSpec((1, d_a), c2),
            pl.BlockSpec((1, d_a), c2),
            pl.BlockSpec((1, d_a), c2),
            pl.BlockSpec((d_a, d_a), c2),
        ],
        out_specs=[pair_spec] * 9 + [flat_spec] * 2,
        out_shape=[pair_act] * 3 + [pair_f32] * 2 + [pair_act] * 4 + [flat_shape] * 2,
        compiler_params=_params(("arbitrary", "arbitrary")),
        name="rwkv_prep",
    )(pa3, pa3, pa3, mu_prev, mu_next, w0c, _hilo(w2blk), a0c, _hilo(a2blk), _hilo(g2), k_k, k_a, r_k, ones)


def _mm_exact_rhs_left(tri_bf16, x):
    hi, lo = _split(x)
    return _dot(tri_bf16, hi) + _dot(tri_bf16, lo)


def _scan_stage(items, s_refs):
    c = items[0][0].shape[0]
    ri = lax.broadcasted_iota(jnp.int32, (c, c), 0)
    ci = lax.broadcasted_iota(jnp.int32, (c, c), 1)
    lane = lax.broadcasted_iota(jnp.int32, (1, PAIR), 1)
    m0 = (lane < HEAD_DIM).astype(F32)
    m1 = 1.0 - m0
    eye = (ri == ci).astype(F32)
    hi_ = lax.broadcasted_iota(jnp.int32, (PAIR, PAIR), 0) // HEAD_DIM
    hj_ = lax.broadcasted_iota(jnp.int32, (PAIR, PAIR), 1) // HEAD_DIM
    same_head = hi_ == hj_
    zero = jnp.zeros((), F32)
    n = len(items)

    def masks(reverse):
        if reverse:
            return ci >= ri, ci > ri, 0
        return ci <= ri, ci < ri, c - 1

    cums = [_mm_exact_rhs_left(masks(it[6])[0].astype(BF16), it[3]) for it in items]

    pre = []
    for (r, kk, v, lw, k, b, reverse), cum in zip(items, cums):
        end = masks(reverse)[2]
        cmid = cum[c // 2:c // 2 + 1, :]
        cend = cum[end:end + 1, :]
        r_abs = r * jnp.exp(cum)
        a_abs = -kk * jnp.exp(cum - lw)
        to_mid = jnp.exp(-cmid)
        from_mid = jnp.exp(cmid - cum)
        to_end = jnp.exp(cend - cum)
        pre.append(dict(
            r_abs=r_abs, a_abs=a_abs, r_rel=r_abs * to_mid, a_rel=a_abs * to_mid,
            k_rel=k * from_mid, b_rel=b * from_mid, k_end=k * to_end, b_end=b * to_end,
            d_tot=jnp.exp(cend), v=v))

    grams = []
    for p in pre:
        lhs = jnp.concatenate([p["r_rel"] * m0, p["r_rel"] * m1, p["a_rel"] * m0, p["a_rel"] * m1],
                              axis=0).astype(BF16)
        rhs = jnp.concatenate([p["k_rel"], p["b_rel"]], axis=0).astype(BF16)
        grams.append(_dot_nt(lhs, rhs))

    pws, ts = [], []
    for it, gram in zip(items, grams):
        strict = masks(it[6])[1]
        for e in range(2):
            a_ab = jnp.where(strict, gram[(2 + e) * c:(3 + e) * c, c:2 * c], zero)
            pws.append(a_ab)
            ts.append(eye + a_ab)
    for _ in range(int(round(math.log2(c))) - 1):
        pws = [_dot(pw.astype(BF16), pw.astype(BF16)) for pw in pws]
        ts = [t + _dot(t.astype(BF16), pw.astype(BF16)) for t, pw in zip(ts, pws)]

    v_blks = [jnp.concatenate([p["v"] * m0, p["v"] * m1], axis=0).astype(BF16) for p in pre]
    akvs, o_rks, rb_cats = [], [], []
    for it, gram, v_blk in zip(items, grams, v_blks):
        incl, strict, _ = masks(it[6])
        ak_cat = jnp.concatenate(
            [jnp.where(strict, gram[(2 + e) * c:(3 + e) * c, 0:c], zero) for e in range(2)], axis=1)
        rk_cat = jnp.concatenate(
            [jnp.where(incl, gram[e * c:(e + 1) * c, 0:c], zero) for e in range(2)], axis=1)
        rb_cats.append(jnp.concatenate(
            [jnp.where(incl, gram[e * c:(e + 1) * c, c:2 * c], zero) for e in range(2)], axis=1).astype(BF16))
        akvs.append(_dot(ak_cat.astype(BF16), v_blk))
        o_rks.append(_dot(rk_cat.astype(BF16), v_blk))

    xs = []
    for i, (p, akv) in enumerate(zip(pre, akvs)):
        t_cat = jnp.concatenate([ts[2 * i], ts[2 * i + 1]], axis=1).astype(BF16)
        y_blk = jnp.concatenate([
            jnp.concatenate([p["a_abs"] * m0, akv * m0], axis=1),
            jnp.concatenate([p["a_abs"] * m1, akv * m1], axis=1)], axis=0).astype(BF16)
        xs.append(_dot(t_cat, y_blk))

    s0s = [s_ref[...] for s_ref in s_refs]
    s0bs = [s0.astype(BF16) for s0 in s0s]
    us = [_dot(x[:, 0:PAIR].astype(BF16), s0b) + x[:, PAIR:2 * PAIR] for x, s0b in zip(xs, s0bs)]
    outs = []
    for i in range(n):
        p, u = pre[i], us[i]
        u_blk = jnp.concatenate([u * m0, u * m1], axis=0).astype(BF16)
        outs.append(_dot(p["r_abs"].astype(BF16), s0bs[i]) + _dot(rb_cats[i], u_blk) + o_rks[i])
        kb_t = jnp.concatenate([p["b_end"].T, p["k_end"].T], axis=1).astype(BF16)
        uv = jnp.concatenate([u, p["v"]], axis=0).astype(BF16)
        d_col = jnp.broadcast_to(p["d_tot"], (PAIR, PAIR)).T
        s_refs[i][...] = jnp.where(same_head, d_col * s0s[i] + _dot(kb_t, uv), zero)
    return outs


def _scan_kernel(rf, vf, kkf, lwf, kf, bf, rb, vb, kkb, lwb, kb, bb, of_ref, ob_ref, s_ref):
    @pl.when(pl.program_id(1) == 0)
    def _():
        s_ref[...] = jnp.zeros_like(s_ref)

    n_p = rf.shape[0]
    items, s_refs = [], []
    for p in range(n_p):
        ld = lambda ref: ref[p].astype(F32)
        items.append((ld(rf), ld(kkf), ld(vf), lwf[p], ld(kf), ld(bf), False))
        s_refs.append(s_ref.at[2 * p])
        items.append((ld(rb), ld(kkb), ld(vb), lwb[p], ld(kb), ld(bb), True))
        s_refs.append(s_ref.at[2 * p + 1])
    outs = _scan_stage(items, s_refs)
    for p in range(n_p):
        of_ref[p] = outs[2 * p].astype(of_ref.dtype)
        ob_ref[p] = outs[2 * p + 1].astype(ob_ref.dtype)


def _scan(r, v, kk, lw_f, lw_b, k_f, k_b, b_f, b_b):
    bsz, n_p, t, _ = r.shape
    nc = t // CHUNK
    fwd = pl.BlockSpec((None, n_p, CHUNK, PAIR), lambda bi, c: (bi, 0, c, 0))
    bwd = pl.BlockSpec((None, n_p, CHUNK, PAIR), lambda bi, c: (bi, 0, nc - 1 - c, 0))
    shape = jax.ShapeDtypeStruct((bsz, n_p, t, PAIR), ACT)
    return pl.pallas_call(
        _scan_kernel,
        grid=(bsz, nc),
        in_specs=[fwd] * 6 + [bwd] * 6,
        out_specs=[fwd, bwd],
        out_shape=[shape, shape],
        scratch_shapes=[pltpu.VMEM((2 * n_p, PAIR, PAIR), F32)],
        compiler_params=_params(("arbitrary", "arbitrary")),
        name="rwkv_scan",
    )(r, v, kk, lw_f, k_f, b_f, r, v, kk, lw_b, k_b, b_b)


def _na_kernel(q_ref, k_ref, v_ref, gq_ref, gk_ref, tab_ref, ones_ref, o_ref, qn_s, kn_s, vb_s, *, rows):
    ones = ones_ref[...]
    scale = HEAD_DIM ** -0.5
    q = q_ref[...].astype(F32)
    k = k_ref[...].astype(F32)
    inv_d = 1.0 / HEAD_DIM
    qn = q * lax.rsqrt(_mm_exact_rhs(q * q, ones) * inv_d + RMS_EPS) * (gq_ref[...] * scale)
    kn = k * lax.rsqrt(_mm_exact_rhs(k * k, ones) * inv_d + RMS_EPS) * gk_ref[...]
    qn_s[...] = qn.astype(BF16)
    kn_s[...] = kn.astype(BF16)
    vb_s[...] = v_ref[...].astype(BF16)
    lane = lax.broadcasted_iota(jnp.int32, (1, PAIR), 1)
    head0 = lane < HEAD_DIM
    win = NA_WR * GRID_W

    def row_group(gi, carry):
        rws = [gi * NA_ROWS_PER_STEP + j for j in range(NA_ROWS_PER_STEP)]
        rss = [jnp.clip(r - NA_WR // 2, 0, rows - NA_WR) for r in rws]
        q_rows = [qn_s[pl.ds(pl.multiple_of(r * GRID_W, GRID_W), GRID_W), :] for r in rws]
        k_wins = [kn_s[pl.ds(pl.multiple_of(rs * GRID_W, GRID_W), win), :] for rs in rss]
        v_wins = [vb_s[pl.ds(pl.multiple_of(rs * GRID_W, GRID_W), win), :] for rs in rss]
        ss = []
        for j in range(NA_ROWS_PER_STEP):
            d0 = rss[j] - rws[j] + NA_WR - 1
            for e in range(2):
                mask = head0 if e == 0 else jnp.logical_not(head0)
                qm = jnp.where(mask, q_rows[j], jnp.zeros_like(q_rows[j]))
                bias = jnp.concatenate(
                    [tab_ref[e, pl.ds(d0 + 2 * m, 1)][0] for m in range(NA_WR // 2)], axis=1)
                ss.append(_dot_nt(qm, k_wins[j]) + bias)
        mxs = [jnp.max(s, axis=-1, keepdims=True) for s in ss]
        ps = [jnp.exp(s - mx) for s, mx in zip(ss, mxs)]
        ls = [jnp.sum(p, axis=-1, keepdims=True) for p in ps]
        pvs = [_dot(p.astype(BF16), v_wins[i // 2]) for i, p in enumerate(ps)]
        for j in range(NA_ROWS_PER_STEP):
            o0 = pvs[2 * j] / ls[2 * j]
            o1 = pvs[2 * j + 1] / ls[2 * j + 1]
            o_ref[pl.ds(pl.multiple_of(rws[j] * GRID_W, GRID_W), GRID_W), :] = (
                jnp.where(head0, o0, o1).astype(o_ref.dtype))
        return carry

    lax.fori_loop(0, rows // NA_ROWS_PER_STEP, row_group, 0)


def _na_bias_table(rpb):
    qc = np.arange(GRID_W)
    kc = np.arange(GRID_W)
    cs = np.clip(qc - NA_WC // 2, 0, GRID_W - NA_WC)
    valid = (kc[None, :] >= cs[:, None]) & (kc[None, :] < cs[:, None] + NA_WC)
    dc = np.clip(kc[None, :] - qc[:, None] + NA_WC - 1, 0, 2 * NA_WC - 2)
    n_dc = 2 * NA_WC - 1
    pick = jnp.asarray(dc.reshape(1, -1) == np.arange(n_dc).reshape(-1, 1), F32)
    b = jnp.dot(rpb.astype(F32).reshape(-1, n_dc), pick, precision=lax.Precision.HIGHEST)
    b = b.reshape(rpb.shape[0], rpb.shape[1], GRID_W, GRID_W)
    b = jnp.where(jnp.asarray(valid)[None, None], b, NEG_BIG)
    return jnp.concatenate([b[:, :-1], b[:, 1:]], axis=-1)


def _na(pb3, q_gain, k_gain, table, d_b):
    bsz, t, _ = pb3.shape
    rows = t // GRID_W
    n_p = d_b // PAIR
    ones = _block_ones(PAIR, HEAD_DIM)
    gq = jnp.tile(q_gain.reshape(1, HEAD_DIM), (1, 2))
    gk = jnp.tile(k_gain.reshape(1, HEAD_DIM), (1, 2))
    n_d = table.shape[1]
    c2 = lambda bi, p: (0, 0)
    return pl.pallas_call(
        functools.partial(_na_kernel, rows=rows),
        grid=(bsz, n_p),
        in_specs=[
            pl.BlockSpec((None, t, PAIR), lambda bi, p: (bi, 0, p)),
            pl.BlockSpec((None, t, PAIR), lambda bi, p: (bi, 0, n_p + p)),
            pl.BlockSpec((None, t, PAIR), lambda bi, p: (bi, 0, 2 * n_p + p)),
            pl.BlockSpec((1, PAIR), c2),
            pl.BlockSpec((1, PAIR), c2),
            pl.BlockSpec((2, n_d, GRID_W, PAIR), lambda bi, p: (p, 0, 0, 0)),
            pl.BlockSpec((PAIR, PAIR), c2),
        ],
        out_specs=pl.BlockSpec((None, t, PAIR), lambda bi, p: (bi, 0, p)),
        out_shape=jax.ShapeDtypeStruct((bsz, t, d_b), ACT),
        scratch_shapes=[pltpu.VMEM((t, PAIR), BF16)] * 3,
        compiler_params=_params(("arbitrary", "arbitrary")),
        name="natten",
    )(pb3, pb3, pb3, gq, gk, table, ones)


def _merge_kernel(of_ref, ob_ref, bonus_ref, g_ref, yb_ref, gates_ref, x_ref,
                  lng_ref, lnb_ref, wa_ref, wb_ref, wo_ref, gffn_ref, wr_ref, br_ref,
                  ones_ref, tri_ref,
                  x1_ref, h2_ref, idx_ref, rank_ref, gw_ref, cnt_ref, carry_ref, *, d_model):
    first = jnp.logical_and(pl.program_id(0) == 0, pl.program_id(1) == 0)

    @pl.when(first)
    def _():
        carry_ref[...] = jnp.zeros_like(carry_ref)

    n_p = of_ref.shape[0]
    o = jnp.concatenate([of_ref[p].astype(F32) + ob_ref[p].astype(F32) for p in range(n_p)],
                        axis=1)
    ones = ones_ref[...]
    inv_d = 1.0 / HEAD_DIM
    mu = _mm_exact_rhs(o, ones) * inv_d
    dv = o - mu
    var = _mm_exact_rhs(dv * dv, ones) * inv_d
    y = dv * lax.rsqrt(var + GN_EPS) * lng_ref[...] + lnb_ref[...] + bonus_ref[...].astype(F32)
    ya = y * g_ref[...].astype(F32)

    gates = gates_ref[...].astype(F32)
    pa = _dot(ya.astype(BF16), wa_ref[...])
    pb = _dot(yb_ref[...].astype(BF16), wb_ref[...])
    merged = _sigmoid(gates[:, :d_model]) * pa + _sigmoid(gates[:, d_model:]) * pb
    x1 = x_ref[...] + _dot(merged.astype(BF16), wo_ref[...])
    x1_ref[...] = x1
    ms = jnp.mean(x1 * x1, axis=-1, keepdims=True)
    h2 = x1 * lax.rsqrt(ms + RMS_EPS) * gffn_ref[...]
    _store_rows_as_tiles(h2_ref, h2)

    logits = _mm3(h2, wr_ref) + br_ref[...]
    tm = logits.shape[0]
    lane = lax.broadcasted_iota(jnp.int32, (tm, LANES), 1)
    work = logits
    vals, idxs = [], []
    for _ in range(TOP_K):
        m = jnp.max(work, axis=-1, keepdims=True)
        ix = jnp.min(jnp.where(work == m, lane, LANES), axis=-1, keepdims=True)
        vals.append(m)
        idxs.append(ix)
        work = jnp.where(lane == ix, -jnp.inf, work)
    es = [jnp.exp(vk - vals[0]) for vk in vals]
    den = es[0] + es[1] + es[2] + es[3]
    member = jnp.zeros((tm, LANES), F32)
    for ix in idxs:
        member = member + (lane == ix).astype(F32)
    before = _dot(tri_ref[...], member.astype(BF16)) + carry_ref[...]
    idx_out = jnp.zeros((tm, LANES), F32)
    rank_out = jnp.zeros((tm, LANES), F32)
    gw_out = jnp.zeros((tm, LANES), F32)
    for kq in range(TOP_K):
        rk = jnp.sum(jnp.where(lane == idxs[kq], before, 0.0), axis=-1, keepdims=True)
        sel = lane == kq
        idx_out = jnp.where(sel, idxs[kq].astype(F32), idx_out)
        rank_out = jnp.where(sel, rk, rank_out)
        gw_out = jnp.where(sel, es[kq] / den, gw_out)
    idx_ref[...] = idx_out.T[:SUBLANES].astype(jnp.int32)
    rank_ref[...] = rank_out.T[:SUBLANES].astype(jnp.int32)
    gw_ref[...] = gw_out
    carry_ref[...] = carry_ref[...] + jnp.sum(member, axis=0, keepdims=True)
    cnt_ref[...] = carry_ref[...]


def _merge(o_f, o_b, bonus, g, yb, gates3, x3, lnx_g, lnx_b, w_a, w_b, w_o, g_ffn, wr_pad, br_pad, tm=512):
    bsz, n_p, t, _ = o_f.shape
    d_a = n_p * PAIR
    d_b = yb.shape[-1]
    d_model = x3.shape[-1]
    n_t = t // tm
    m = bsz * t
    ones = _block_ones(d_a, HEAD_DIM)
    tri = jnp.asarray(np.tril(np.ones((tm, tm)), -1), BF16)
    c2 = lambda bi, i: (0, 0)
    tok = lambda w: pl.BlockSpec((None, tm, w), lambda bi, i: (bi, i, 0))
    flat = lambda w: pl.BlockSpec((tm, w), lambda bi, i: (bi * n_t + i, 0))
    pair = pl.BlockSpec((None, n_p, tm, PAIR), lambda bi, i: (bi, 0, i, 0))
    return pl.pallas_call(
        functools.partial(_merge_kernel, d_model=d_model),
        grid=(bsz, n_t),
        in_specs=[
            pair, pair, tok(d_a), tok(d_a), tok(d_b), tok(2 * d_model), tok(d_model),
            pl.BlockSpec((1, d_a), c2), pl.BlockSpec((1, d_a), c2),
            pl.BlockSpec((d_a, d_model), c2), pl.BlockSpec((d_b, d_model), c2),
            pl.BlockSpec((d_model, d_model), c2), pl.BlockSpec((1, d_model), c2),
            pl.BlockSpec((2, d_model, LANES), lambda bi, i: (0, 0, 0)), pl.BlockSpec((1, LANES), c2),
            pl.BlockSpec((d_a, d_a), c2), pl.BlockSpec((tm, tm), c2),
        ],
        out_specs=[flat(d_model),
                   pl.BlockSpec((tm * SUBLANES, LANES), lambda bi, i: (bi * n_t + i, 0)),
                   pl.BlockSpec((SUBLANES, tm), lambda bi, i: (0, bi * n_t + i)),
                   pl.BlockSpec((SUBLANES, tm), lambda bi, i: (0, bi * n_t + i)),
                   flat(LANES),
                   pl.BlockSpec((1, LANES), c2)],
        out_shape=[
            jax.ShapeDtypeStruct((m, d_model), F32),
            jax.ShapeDtypeStruct((m * SUBLANES, LANES), F32),
            jax.ShapeDtypeStruct((SUBLANES, m), jnp.int32),
            jax.ShapeDtypeStruct((SUBLANES, m), jnp.int32),
            jax.ShapeDtypeStruct((m, LANES), F32),
            jax.ShapeDtypeStruct((1, LANES), F32),
        ],
        scratch_shapes=[pltpu.VMEM((1, LANES), F32)],
        compiler_params=_params(("arbitrary", "arbitrary")),
        name="merge_router",
    )(o_f, o_b, bonus, g, yb, gates3, x3, lnx_g, lnx_b, w_a, w_b, w_o, g_ffn, _hilo(wr_pad), br_pad, ones, tri)


def _tile_major(dest, tm):
    k, m = dest.shape
    return dest.reshape(k, m // tm, tm).transpose(1, 0, 2).reshape(m // tm, 1, k * tm)


def _dispatch_kernel(ps_ref, pn_ref, nu_ref, dest_ref, h_ref, xs_ref, zbuf, sem, zsem, csem, bsem):
    i = pl.program_id(0)
    tm = h_ref.shape[0]
    n_e = ps_ref.shape[0]
    nb = xs_ref.shape[0] // MOE_BLOCK

    def row_copy(t, kq):
        return pltpu.make_async_copy(
            h_ref.at[t], xs_ref.at[dest_ref[0, kq * tm + t]], sem)

    def pad_copy(r):
        return pltpu.make_async_copy(zbuf.at[0], xs_ref.at[r], zsem)

    def oct_copy(o):
        return pltpu.make_async_copy(
            zbuf.at[pl.ds(0, SUBLANES)], xs_ref.at[pl.ds(o * SUBLANES, SUBLANES)], csem)

    def blk_copy(b):
        return pltpu.make_async_copy(zbuf, xs_ref.at[pl.ds(b * MOE_BLOCK, MOE_BLOCK)], bsem)

    def pads(fn):
        def per_expert(e, c):
            start = ps_ref[e]
            end = start + pn_ref[e]
            first_oct = (start + SUBLANES - 1) // SUBLANES

            def single(r, c2):
                fn(pad_copy(r))
                return c2

            def octet(o, c2):
                fn(oct_copy(o))
                return c2
            c = lax.fori_loop(start, jnp.minimum(first_oct * SUBLANES, end), single, c)
            return lax.fori_loop(first_oct, end // SUBLANES, octet, c)
        lax.fori_loop(0, n_e, per_expert, 0)

        def per_block(b, c):
            fn(blk_copy(b))
            return c
        lax.fori_loop(nu_ref[0], nb, per_block, 0)

    @pl.when(i == 0)
    def _():
        zbuf[...] = jnp.zeros_like(zbuf)
        pads(lambda cp: cp.start())

    def issue(t, c):
        for kq in range(TOP_K):
            row_copy(t, kq).start(priority=kq % 2)
        return c

    def drain(t, c):
        for kq in range(TOP_K):
            row_copy(t, kq).wait()
        return c

    lax.fori_loop(0, tm, issue, 0, unroll=8)
    lax.fori_loop(0, tm, drain, 0, unroll=8)

    @pl.when(i == 0)
    def _():
        pads(lambda cp: cp.wait())


def _dispatch(dest, pad_start, pad_n, n_used, h2, n_pad, tm=256):
    m, ds, dl = h2.shape
    nt = m // tm
    grid_spec = pltpu.PrefetchScalarGridSpec(
        num_scalar_prefetch=3,
        grid=(nt,),
        in_specs=[
            pl.BlockSpec((None, 1, TOP_K * tm), lambda i, ps, pn, nu: (i, 0, 0), memory_space=pltpu.SMEM),
            pl.BlockSpec((tm, ds, dl), lambda i, ps, pn, nu: (i, 0, 0)),
        ],
        out_specs=pl.BlockSpec(memory_space=pl.ANY),
        scratch_shapes=[pltpu.VMEM((MOE_BLOCK, ds, dl), F32)] + [pltpu.SemaphoreType.DMA(())] * 4,
    )
    return pl.pallas_call(
        _dispatch_kernel,
        grid_spec=grid_spec,
        out_shape=jax.ShapeDtypeStruct((n_pad, ds, dl), F32),
        compiler_params=_params(("arbitrary",)),
        name="moe_dispatch",
    )(pad_start, pad_n, n_used, _tile_major(dest, tm), h2)


def _expert_kernel(blk_ref, be_ref, nu_ref, xs_ref, w1_ref, b1_ref, w2_ref, b2_ref, ys_ref,
                   w1b_ref, w2b_ref, *, d_e):
    del blk_ref
    i = pl.program_id(0)
    new_expert = jnp.logical_or(i == 0, be_ref[i] != be_ref[jnp.maximum(i - 1, 0)])

    @pl.when(new_expert)
    def _():
        w1b_ref[...] = w1_ref[...].astype(BF16)
        w2b_ref[...] = w2_ref[...].astype(BF16)

    @pl.when(i < nu_ref[0])
    def _():
        x = _load_tiles_as_rows(xs_ref, MOE_BLOCK).astype(BF16)
        u = _dot(x, w1b_ref[...]) + b1_ref[...]
        glu = jnp.minimum(u[:, :d_e], SWIGLU_LIMIT)
        lin = jnp.clip(u[:, d_e:], -SWIGLU_LIMIT, SWIGLU_LIMIT)
        act = glu * _sigmoid(SWIGLU_ALPHA * glu) * (lin + 1.0)
        _store_rows_as_tiles(ys_ref, _dot(act.astype(BF16), w2b_ref[...]) + b2_ref[...])

    @pl.when(i >= nu_ref[0])
    def _():
        ys_ref[...] = jnp.zeros_like(ys_ref)


def _experts(blk_idx, blk_e, n_used, xs, w1, b1, w2, b2):
    n_pad = xs.shape[0] // SUBLANES
    d = SUBLANES * LANES
    nb = n_pad // MOE_BLOCK
    n_e, _, d2 = w1.shape
    d_e = d2 // 2
    grid_spec = pltpu.PrefetchScalarGridSpec(
        num_scalar_prefetch=3,
        grid=(nb,),
        in_specs=[
            pl.BlockSpec((MOE_BLOCK * SUBLANES, LANES), lambda i, bi, be, nu: (bi[i], 0)),
            pl.BlockSpec((None, d, d2), lambda i, bi, be, nu: (be[i], 0, 0)),
            pl.BlockSpec((None, 1, d2), lambda i, bi, be, nu: (be[i], 0, 0)),
            pl.BlockSpec((None, d_e, d), lambda i, bi, be, nu: (be[i], 0, 0)),
            pl.BlockSpec((None, 1, d), lambda i, bi, be, nu: (be[i], 0, 0)),
        ],
        out_specs=pl.BlockSpec((MOE_BLOCK * SUBLANES, LANES), lambda i, bi, be, nu: (i, 0)),
        scratch_shapes=[pltpu.VMEM((d, d2), BF16), pltpu.VMEM((d_e, d), BF16)],
    )
    return pl.pallas_call(
        functools.partial(_expert_kernel, d_e=d_e),
        grid_spec=grid_spec,
        out_shape=jax.ShapeDtypeStruct((n_pad * SUBLANES, LANES), F32),
        compiler_params=_params(("arbitrary",)),
        name="moe_experts",
    )(blk_idx, blk_e, n_used, xs, w1, b1.reshape(n_e, 1, d2), w2, b2.reshape(n_e, 1, d))


def _combine_kernel(dest_ref, dnext_ref, ys_ref, x1_ref, gw_ref, o_ref, buf, sems):
    i = pl.program_id(0)
    nt = pl.num_programs(0)
    tm = x1_ref.shape[0]
    slot = i % 2

    def row_copy(d_ref, sl, t, kq):
        return pltpu.make_async_copy(
            ys_ref.at[d_ref[0, kq * tm + t]],
            buf.at[sl, kq, pl.ds(pl.multiple_of(t * SUBLANES, SUBLANES), SUBLANES)], sems.at[sl])

    def issue_all(d_ref, sl):
        def body(t, c):
            for kq in range(TOP_K):
                row_copy(d_ref, sl, t, kq).start(priority=kq % 2)
            return c
        lax.fori_loop(0, tm, body, 0, unroll=8)

    @pl.when(i == 0)
    def _():
        issue_all(dest_ref, 0)

    @pl.when(i + 1 < nt)
    def _():
        issue_all(dnext_ref, 1 - slot)

    def drain(t, c):
        for kq in range(TOP_K):
            row_copy(dest_ref, slot, t, kq).wait()
        return c

    lax.fori_loop(0, tm, drain, 0, unroll=8)
    gw = gw_ref[...]
    for s in range(SUBLANES):
        acc = x1_ref[:, s * LANES:(s + 1) * LANES]
        for kq in range(TOP_K):
            acc = acc + gw[:, kq:kq + 1] * buf[slot, kq, pl.ds(s, tm, stride=SUBLANES), :]
        o_ref[:, s * LANES:(s + 1) * LANES] = acc


def _combine(dest, ys, x1, gw, tm=128):
    m, d = x1.shape
    nt = m // tm
    return pl.pallas_call(
        _combine_kernel,
        grid=(nt,),
        in_specs=[
            pl.BlockSpec((None, 1, TOP_K * tm), lambda i: (i, 0, 0), memory_space=pltpu.SMEM),
            pl.BlockSpec((None, 1, TOP_K * tm), lambda i: (jnp.minimum(i + 1, nt - 1), 0, 0),
                         memory_space=pltpu.SMEM),
            pl.BlockSpec(memory_space=pl.ANY),
            pl.BlockSpec((tm, d), lambda i: (i, 0)),
            pl.BlockSpec((tm, LANES), lambda i: (i, 0)),
        ],
        out_specs=pl.BlockSpec((tm, d), lambda i: (i, 0)),
        out_shape=jax.ShapeDtypeStruct((m, d), F32),
        scratch_shapes=[pltpu.VMEM((2, TOP_K, tm * SUBLANES, LANES), F32), pltpu.SemaphoreType.DMA((2,))],
        compiler_params=_params(("arbitrary",)),
        name="moe_combine",
    )(_tile_major(dest, tm), _tile_major(dest, tm), ys, x1, gw)


def _blockdiag2(a, b):
    za = jnp.zeros((a.shape[0], b.shape[1]), a.dtype)
    zb = jnp.zeros((b.shape[0], a.shape[1]), a.dtype)
    return jnp.concatenate([jnp.concatenate([a, za], axis=1), jnp.concatenate([zb, b], axis=1)], axis=0)


def _layer(x, g_mix, w_in, mu_prev, mu_next, w0_f, w2_f, w0_b, w2_b, a0_f, a2_f, a0_b, a2_b,
           g2, k_k, k_a, r_k, lnx_g, lnx_b, q_norm_g, k_norm_g, rpb, w_a, w_b, w_o,
           g_ffn, w_router, b_router, w1, b1, w2, b2):
    bsz, t, d_model = x.shape
    m = bsz * t
    d_a = w_a.shape[0]
    d_b = w_b.shape[0]
    a_cols = mu_prev.shape[0]
    b_cols = 3 * d_b
    row = lambda a: a.reshape(1, -1).astype(F32)

    w_in_b = w_in.astype(BF16)
    pa, pb, gates = _inproj(x.reshape(m, d_model), row(g_mix), w_in_b[:, :a_cols],
                            w_in_b[:, a_cols:a_cols + b_cols], w_in_b[:, a_cols + b_cols:])

    prep = _prep(pa.reshape(bsz, t, a_cols), row(mu_prev), row(mu_next),
                 jnp.concatenate([row(w0_f), row(w0_b)], axis=1), _blockdiag2(w2_f, w2_b),
                 jnp.concatenate([row(a0_f), row(a0_b)], axis=1), _blockdiag2(a2_f, a2_b),
                 g2, row(k_k), row(k_a), row(r_k), d_a)
    r, v, kk, lw_f, lw_b, k_f, k_b, b_f, b_b, bonus, g = prep
    o_f, o_b = _scan(r, v, kk, lw_f, lw_b, k_f, k_b, b_f, b_b)

    yb = _na(pb.reshape(bsz, t, b_cols), q_norm_g, k_norm_g, _na_bias_table(rpb), d_b)

    n_e = w_router.shape[1]
    wr_pad = jnp.zeros((d_model, LANES), F32).at[:, :n_e].set(w_router)
    br_pad = jnp.full((1, LANES), NEG_BIG, F32).at[0, :n_e].set(b_router)
    x1, h2, idx, rank, gw, cnt = _merge(
        o_f, o_b, bonus, g, yb, gates.reshape(bsz, t, 2 * d_model), x,
        row(lnx_g), row(lnx_b), w_a.astype(BF16), w_b.astype(BF16), w_o.astype(BF16),
        row(g_ffn), wr_pad, br_pad)

    counts = cnt[0, :n_e].astype(jnp.int32)
    padded = ((counts + MOE_BLOCK - 1) // MOE_BLOCK) * MOE_BLOCK
    pend = jnp.cumsum(padded)
    pstart = pend - padded
    n_assign = m * TOP_K
    n_blocks = -(-n_assign // MOE_BLOCK) + n_e
    n_pad = n_blocks * MOE_BLOCK
    dest = rank[:TOP_K]
    for e in range(n_e):
        dest = dest + jnp.where(idx[:TOP_K] == e, pstart[e], 0)
    dest = dest.astype(jnp.int32)
    n_used = (pend[-1] // MOE_BLOCK).astype(jnp.int32)
    blk_idx = jnp.minimum(jnp.arange(n_blocks, dtype=jnp.int32), n_used - 1)
    blk_e = jnp.sum((blk_idx[:, None] * MOE_BLOCK >= pend[None, :]).astype(jnp.int32), axis=1)
    blk_e = jnp.minimum(blk_e, n_e - 1)

    xs = _dispatch(dest, (pstart + counts).astype(jnp.int32), (padded - counts).astype(jnp.int32),
                   n_used.reshape(1), h2.reshape(m, SUBLANES, LANES), n_pad)
    ys = _experts(blk_idx, blk_e, n_used.reshape(1), xs.reshape(n_pad * SUBLANES, LANES), w1, b1, w2, b2)
    out = _combine(dest, ys.reshape(n_pad, SUBLANES, LANES), x1, gw)
    return out.reshape(bsz, t, d_model)


def kernel(x, g_mix, w_in, mu_prev, mu_next, w0_f, w2_f, w0_b, w2_b, a0_f, a2_f, a0_b, a2_b, g2, k_k, k_a, r_k, lnx_g, lnx_b, q_norm_g, k_norm_g, rpb, w_a, w_b, w_o, g_ffn, w_router, b_router, w1, b1, w2, b2):
    for l in range(g_mix.shape[0]):
        x = _layer(x, g_mix[l], w_in[l], mu_prev[l], mu_next[l], w0_f[l], w2_f[l], w0_b[l], w2_b[l],
                   a0_f[l], a2_f[l], a0_b[l], a2_b[l], g2[l], k_k[l], k_a[l], r_k[l], lnx_g[l], lnx_b[l],
                   q_norm_g[l], k_norm_g[l], rpb[l], w_a[l], w_b[l], w_o[l], g_ffn[l], w_router[l],
                   b_router[l], w1[l], b1[l], w2[l], b2[l])
    return x
```

```python
import functools
import math

import numpy as np
import jax
import jax.numpy as jnp
from jax import lax
from jax.experimental import pallas as pl
from jax.experimental.pallas import tpu as pltpu

F32 = jnp.float32
BF16 = jnp.bfloat16
ACT = jnp.bfloat16

LANES = 128
SUBLANES = 8
HEAD_DIM = 64
PAIR = 2 * HEAD_DIM
GRID_W = 64
NA_WR = 8
NA_WC = 16
W_LORA = 64
A_LORA = 64
G_LORA = 128
DECAY_SCALE = math.exp(-0.5)
GN_EPS = 64e-5
RMS_EPS = 1e-5
N_EXPERTS = 32
TOP_K = 4
MOE_BLOCK = 512
SWIGLU_LIMIT = 7.0
SWIGLU_ALPHA = 1.702
NEG_BIG = -1e30
CHUNK = 128
NA_ROWS_PER_STEP = 16
VMEM_LIMIT = 56 * 1024 * 1024


def _dot(a, b):
    return jnp.dot(a, b, preferred_element_type=F32)


def _dot_nt(a, b):
    return lax.dot_general(a, b, (((1,), (1,)), ((), ())), preferred_element_type=F32)


def _split(a):
    hi = a.astype(BF16)
    lo = (a - hi.astype(F32)).astype(BF16)
    return hi, lo


def _mm_exact_rhs(a, b_bf16):
    hi, lo = _split(a)
    return _dot(hi, b_bf16) + _dot(lo, b_bf16)


def _hilo(b):
    return jnp.stack(_split(b))


def _mm3(a, b_hilo):
    ah, al = _split(a)
    bh = b_hilo[0]
    return _dot(ah, bh) + _dot(al, bh) + _dot(ah, b_hilo[1])


def _store_rows_as_tiles(ref, val):
    n = val.shape[0]
    for s in range(SUBLANES):
        ref[pl.ds(s, n, stride=SUBLANES), :] = val[:, s * LANES:(s + 1) * LANES]


def _load_tiles_as_rows(ref, n):
    return jnp.concatenate([ref[pl.ds(s, n, stride=SUBLANES), :] for s in range(SUBLANES)], axis=1)


def _sigmoid(x):
    return 1.0 / (1.0 + jnp.exp(-x))


def _params(sem):
    return pltpu.CompilerParams(dimension_semantics=sem, vmem_limit_bytes=VMEM_LIMIT)


def _block_ones(n, blk):
    i = np.arange(n) // blk
    return jnp.asarray(i[:, None] == i[None, :], BF16)


def _inproj_kernel(x_ref, g_ref, wa_ref, wb_ref, wg_ref, pa_ref, pb_ref, pg_ref):
    x = x_ref[...]
    ms = jnp.mean(x * x, axis=-1, keepdims=True)
    h = (x * lax.rsqrt(ms + RMS_EPS) * g_ref[...]).astype(BF16)
    pa_ref[...] = _dot(h, wa_ref[...]).astype(pa_ref.dtype)
    pb_ref[...] = _dot(h, wb_ref[...]).astype(pb_ref.dtype)
    pg_ref[...] = _dot(h, wg_ref[...]).astype(pg_ref.dtype)


def _inproj(x2, g_mix, w_a, w_b, w_g, tm=512):
    m, d = x2.shape
    na, nb, ng = w_a.shape[1], w_b.shape[1], w_g.shape[1]
    full = lambda i: (0, 0)
    return pl.pallas_call(
        _inproj_kernel,
        grid=(m // tm,),
        in_specs=[
            pl.BlockSpec((tm, d), lambda i: (i, 0)),
            pl.BlockSpec((1, d), full),
            pl.BlockSpec((d, na), full),
            pl.BlockSpec((d, nb), full),
            pl.BlockSpec((d, ng), full),
        ],
        out_specs=[
            pl.BlockSpec((tm, na), lambda i: (i, 0)),
            pl.BlockSpec((tm, nb), lambda i: (i, 0)),
            pl.BlockSpec((tm, ng), lambda i: (i, 0)),
        ],
        out_shape=[
            jax.ShapeDtypeStruct((m, na), ACT),
            jax.ShapeDtypeStruct((m, nb), ACT),
            jax.ShapeDtypeStruct((m, ng), ACT),
        ],
        compiler_params=_params(("arbitrary",)),
        name="inproj",
    )(x2, g_mix, w_a, w_b, w_g)


def _prep_kernel(p_ref, prev_ref, next_ref, mup_ref, mun_ref, w0_ref, w2_ref, a0_ref, a2_ref,
                 g2_ref, kk_ref, ka_ref, rk_ref, ones_ref,
                 r_o, v_o, kk_o, lwf_o, lwb_o, kf_o, kb_o, bf_o, bb_o, bonus_o, g_o, *, d_a):
    i = pl.program_id(1)
    n_t = pl.num_programs(1)
    p = p_ref[...].astype(F32)
    tt = p.shape[0]
    halo = prev_ref.shape[0]
    prow = jnp.where(i > 0, prev_ref[halo - 1:halo, :].astype(F32), 0.0)
    nrow = jnp.where(i < n_t - 1, next_ref[0:1, :].astype(F32), 0.0)
    rid = lax.broadcasted_iota(jnp.int32, (tt, 1), 0)
    prev = jnp.where(rid == 0, prow, pltpu.roll(p, 1, axis=0))
    nxt = jnp.where(rid == tt - 1, nrow, pltpu.roll(p, tt - 1, axis=0))
    xa = p + mup_ref[...] * (prev - p) + mun_ref[...] * (nxt - p)

    r = xa[:, 0:d_a]
    k = xa[:, d_a:2 * d_a]
    v = xa[:, 2 * d_a:3 * d_a]
    o = 3 * d_a
    lw = xa[:, o:o + 2 * W_LORA]
    la = xa[:, o + 2 * W_LORA:o + 2 * W_LORA + 2 * A_LORA]
    lg = xa[:, o + 2 * W_LORA + 2 * A_LORA:]

    dpre = w0_ref[...] + _mm3(jnp.tanh(lw), w2_ref)
    apre = a0_ref[...] + _mm3(la, a2_ref)
    g = _mm3(_sigmoid(lg), g2_ref)
    logw = -DECAY_SCALE * _sigmoid(dpre)
    a = _sigmoid(apre)

    ones = ones_ref[...]
    kkr = k * kk_ref[...]
    ss = _mm_exact_rhs(kkr * kkr, ones)
    kk = kkr / jnp.maximum(jnp.sqrt(ss), 1e-12)

    ka = ka_ref[...]
    k_f = k * (1.0 + (a[:, :d_a] - 1.0) * ka)
    k_b = k * (1.0 + (a[:, d_a:] - 1.0) * ka)
    b_f = kk * a[:, :d_a]
    b_b = kk * a[:, d_a:]
    rk = rk_ref[...]
    bon = _mm_exact_rhs(r * (k_f + k_b) * rk, ones) * v

    bonus_o[...] = bon.astype(bonus_o.dtype)
    g_o[...] = g.astype(g_o.dtype)
    for pi in range(d_a // PAIR):
        sl = slice(pi * PAIR, (pi + 1) * PAIR)
        r_o[pi] = r[:, sl].astype(r_o.dtype)
        v_o[pi] = v[:, sl].astype(v_o.dtype)
        kk_o[pi] = kk[:, sl].astype(kk_o.dtype)
        lwf_o[pi] = logw[:, sl]
        lwb_o[pi] = logw[:, d_a + pi * PAIR:d_a + (pi + 1) * PAIR]
        kf_o[pi] = k_f[:, sl].astype(kf_o.dtype)
        kb_o[pi] = k_b[:, sl].astype(kb_o.dtype)
        bf_o[pi] = b_f[:, sl].astype(bf_o.dtype)
        bb_o[pi] = b_b[:, sl].astype(bb_o.dtype)


def _prep(pa3, mu_prev, mu_next, w0c, w2blk, a0c, a2blk, g2, k_k, k_a, r_k, d_a, tt=512):
    b, t, ac = pa3.shape
    n_t = t // tt
    n_p = d_a // PAIR
    ones = _block_ones(d_a, HEAD_DIM)
    c2 = lambda bi, i: (0, 0)
    c3 = lambda bi, i: (0, 0, 0)
    pair_spec = pl.BlockSpec((None, n_p, tt, PAIR), lambda bi, i: (bi, 0, i, 0))
    pair_act = jax.ShapeDtypeStruct((b, n_p, t, PAIR), ACT)
    pair_f32 = jax.ShapeDtypeStruct((b, n_p, t, PAIR), F32)
    flat_spec = pl.BlockSpec((None, tt, d_a), lambda bi, i: (bi, i, 0))
    flat_shape = jax.ShapeDtypeStruct((b, t, d_a), ACT)
    halo = 16
    r8 = tt // halo
    return pl.pallas_call(
        functools.partial(_prep_kernel, d_a=d_a),
        grid=(b, n_t),
        in_specs=[
            pl.BlockSpec((None, tt, ac), lambda bi, i: (bi, i, 0)),
            pl.BlockSpec((None, halo, ac), lambda bi, i: (bi, jnp.maximum(i * r8 - 1, 0), 0)),
            pl.BlockSpec((None, halo, ac), lambda bi, i: (bi, jnp.minimum((i + 1) * r8, t // halo - 1), 0)),
            pl.BlockSpec((1, ac), c2),
            pl.BlockSpec((1, ac), c2),
            pl.BlockSpec((1, 2 * d_a), c2),
            pl.BlockSpec((2, 2 * W_LORA, 2 * d_a), c3),
            pl.BlockSpec((1, 2 * d_a), c2),
            pl.BlockSpec((2, 2 * A_LORA, 2 * d_a), c3),
            pl.BlockSpec((2, G_LORA, d_a), c3),
            pl.BlockSpec((1, d_a), c2),
            pl.BlockSpec((1, d_a), c2),
            pl.BlockSpec((1, d_a), c2),
            pl.BlockSpec((d_a, d_a), c2),
        ],
        out_specs=[pair_spec] * 9 + [flat_spec] * 2,
        out_shape=[pair_act] * 3 + [pair_f32] * 2 + [pair_act] * 4 + [flat_shape] * 2,
        compiler_params=_params(("arbitrary", "arbitrary")),
        name="rwkv_prep",
    )(pa3, pa3, pa3, mu_prev, mu_next, w0c, _hilo(w2blk), a0c, _hilo(a2blk), _hilo(g2), k_k, k_a, r_k, ones)


def _mm_exact_rhs_left(tri_bf16, x):
    hi, lo = _split(x)
    return _dot(tri_bf16, hi) + _dot(tri_bf16, lo)


def _scan_stage(items, s_refs):
    c = items[0][0].shape[0]
    ri = lax.broadcasted_iota(jnp.int32, (c, c), 0)
    ci = lax.broadcasted_iota(jnp.int32, (c, c), 1)
    lane = lax.broadcasted_iota(jnp.int32, (1, PAIR), 1)
    m0 = (lane < HEAD_DIM).astype(F32)
    m1 = 1.0 - m0
    eye = (ri == ci).astype(F32)
    hi_ = lax.broadcasted_iota(jnp.int32, (PAIR, PAIR), 0) // HEAD_DIM
    hj_ = lax.broadcasted_iota(jnp.int32, (PAIR, PAIR), 1) // HEAD_DIM
    same_head = hi_ == hj_
    zero = jnp.zeros((), F32)
    n = len(items)

    def masks(reverse):
        if reverse:
            return ci >= ri, ci > ri, 0
        return ci <= ri, ci < ri, c - 1

    cums = [_mm_exact_rhs_left(masks(it[6])[0].astype(BF16), it[3]) for it in items]

    pre = []
    for (r, kk, v, lw, k, b, reverse), cum in zip(items, cums):
        end = masks(reverse)[2]
        cmid = cum[c // 2:c // 2 + 1, :]
        cend = cum[end:end + 1, :]
        r_abs = r * jnp.exp(cum)
        a_abs = -kk * jnp.exp(cum - lw)
        to_mid = jnp.exp(-cmid)
        from_mid = jnp.exp(cmid - cum)
        to_end = jnp.exp(cend - cum)
        pre.append(dict(
            r_abs=r_abs, a_abs=a_abs, r_rel=r_abs * to_mid, a_rel=a_abs * to_mid,
            k_rel=k * from_mid, b_rel=b * from_mid, k_end=k * to_end, b_end=b * to_end,
            d_tot=jnp.exp(cend), v=v))

    grams = []
    for p in pre:
        lhs = jnp.concatenate([p["r_rel"] * m0, p["r_rel"] * m1, p["a_rel"] * m0, p["a_rel"] * m1],
                              axis=0).astype(BF16)
        rhs = jnp.concatenate([p["k_rel"], p["b_rel"]], axis=0).astype(BF16)
        grams.append(_dot_nt(lhs, rhs))

    pws, ts = [], []
    for it, gram in zip(items, grams):
        strict = masks(it[6])[1]
        for e in range(2):
            a_ab = jnp.where(strict, gram[(2 + e) * c:(3 + e) * c, c:2 * c], zero)
            pws.append(a_ab)
            ts.append(eye + a_ab)
    for _ in range(int(round(math.log2(c))) - 1):
        pws = [_dot(pw.astype(BF16), pw.astype(BF16)) for pw in pws]
        ts = [t + _dot(t.astype(BF16), pw.astype(BF16)) for t, pw in zip(ts, pws)]

    v_blks = [jnp.concatenate([p["v"] * m0, p["v"] * m1], axis=0).astype(BF16) for p in pre]
    akvs, o_rks, rb_cats = [], [], []
    for it, gram, v_blk in zip(items, grams, v_blks):
        incl, strict, _ = masks(it[6])
        ak_cat = jnp.concatenate(
            [jnp.where(strict, gram[(2 + e) * c:(3 + e) * c, 0:c], zero) for e in range(2)], axis=1)
        rk_cat = jnp.concatenate(
            [jnp.where(incl, gram[e * c:(e + 1) * c, 0:c], zero) for e in range(2)], axis=1)
        rb_cats.append(jnp.concatenate(
            [jnp.where(incl, gram[e * c:(e + 1) * c, c:2 * c], zero) for e in range(2)], axis=1).astype(BF16))
        akvs.append(_dot(ak_cat.astype(BF16), v_blk))
        o_rks.append(_dot(rk_cat.astype(BF16), v_blk))

    xs = []
    for i, (p, akv) in enumerate(zip(pre, akvs)):
        t_cat = jnp.concatenate([ts[2 * i], ts[2 * i + 1]], axis=1).astype(BF16)
        y_blk = jnp.concatenate([
            jnp.concatenate([p["a_abs"] * m0, akv * m0], axis=1),
            jnp.concatenate([p["a_abs"] * m1, akv * m1], axis=1)], axis=0).astype(BF16)
        xs.append(_dot(t_cat, y_blk))

    s0s = [s_ref[...] for s_ref in s_refs]
    s0bs = [s0.astype(BF16) for s0 in s0s]
    us = [_dot(x[:, 0:PAIR].astype(BF16), s0b) + x[:, PAIR:2 * PAIR] for x, s0b in zip(xs, s0bs)]
    outs = []
    for i in range(n):
        p, u = pre[i], us[i]
        u_blk = jnp.concatenate([u * m0, u * m1], axis=0).astype(BF16)
        outs.append(_dot(p["r_abs"].astype(BF16), s0bs[i]) + _dot(rb_cats[i], u_blk) + o_rks[i])
        kb_t = jnp.concatenate([p["b_end"].T, p["k_end"].T], axis=1).astype(BF16)
        uv = jnp.concatenate([u, p["v"]], axis=0).astype(BF16)
        d_col = jnp.broadcast_to(p["d_tot"], (PAIR, PAIR)).T
        s_refs[i][...] = jnp.where(same_head, d_col * s0s[i] + _dot(kb_t, uv), zero)
    return outs


def _scan_kernel(rf, vf, kkf, lwf, kf, bf, rb, vb, kkb, lwb, kb, bb, of_ref, ob_ref, s_ref):
    @pl.when(pl.program_id(1) == 0)
    def _():
        s_ref[...] = jnp.zeros_like(s_ref)

    n_p = rf.shape[0]
    items, s_refs = [], []
    for p in range(n_p):
        ld = lambda ref: ref[p].astype(F32)
        items.append((ld(rf), ld(kkf), ld(vf), lwf[p], ld(kf), ld(bf), False))
        s_refs.append(s_ref.at[2 * p])
        items.append((ld(rb), ld(kkb), ld(vb), lwb[p], ld(kb), ld(bb), True))
        s_refs.append(s_ref.at[2 * p + 1])
    outs = _scan_stage(items, s_refs)
    for p in range(n_p):
        of_ref[p] = outs[2 * p].astype(of_ref.dtype)
        ob_ref[p] = outs[2 * p + 1].astype(ob_ref.dtype)


def _scan(r, v, kk, lw_f, lw_b, k_f, k_b, b_f, b_b):
    bsz, n_p, t, _ = r.shape
    nc = t // CHUNK
    fwd = pl.BlockSpec((None, n_p, CHUNK, PAIR), lambda bi, c: (bi, 0, c, 0))
    bwd = pl.BlockSpec((None, n_p, CHUNK, PAIR), lambda bi, c: (bi, 0, nc - 1 - c, 0))
    shape = jax.ShapeDtypeStruct((bsz, n_p, t, PAIR), ACT)
    return pl.pallas_call(
        _scan_kernel,
        grid=(bsz, nc),
        in_specs=[fwd] * 6 + [bwd] * 6,
        out_specs=[fwd, bwd],
        out_shape=[shape, shape],
        scratch_shapes=[pltpu.VMEM((2 * n_p, PAIR, PAIR), F32)],
        compiler_params=_params(("arbitrary", "arbitrary")),
        name="rwkv_scan",
    )(r, v, kk, lw_f, k_f, b_f, r, v, kk, lw_b, k_b, b_b)


def _na_kernel(q_ref, k_ref, v_ref, gq_ref, gk_ref, tab_ref, ones_ref, o_ref, qn_s, kn_s, vb_s, *, rows):
    ones = ones_ref[...]
    scale = HEAD_DIM ** -0.5
    q = q_ref[...].astype(F32)
    k = k_ref[...].astype(F32)
    inv_d = 1.0 / HEAD_DIM
    qn = q * lax.rsqrt(_mm_exact_rhs(q * q, ones) * inv_d + RMS_EPS) * (gq_ref[...] * scale)
    kn = k * lax.rsqrt(_mm_exact_rhs(k * k, ones) * inv_d + RMS_EPS) * gk_ref[...]
    qn_s[...] = qn.astype(BF16)
    kn_s[...] = kn.astype(BF16)
    vb_s[...] = v_ref[...].astype(BF16)
    lane = lax.broadcasted_iota(jnp.int32, (1, PAIR), 1)
    head0 = lane < HEAD_DIM
    win = NA_WR * GRID_W

    def row_group(gi, carry):
        rws = [gi * NA_ROWS_PER_STEP + j for j in range(NA_ROWS_PER_STEP)]
        rss = [jnp.clip(r - NA_WR // 2, 0, rows - NA_WR) for r in rws]
        q_rows = [qn_s[pl.ds(pl.multiple_of(r * GRID_W, GRID_W), GRID_W), :] for r in rws]
        k_wins = [kn_s[pl.ds(pl.multiple_of(rs * GRID_W, GRID_W), win), :] for rs in rss]
        v_wins = [vb_s[pl.ds(pl.multiple_of(rs * GRID_W, GRID_W), win), :] for rs in rss]
        ss = []
        for j in range(NA_ROWS_PER_STEP):
            d0 = rss[j] - rws[j] + NA_WR - 1
            for e in range(2):
                mask = head0 if e == 0 else jnp.logical_not(head0)
                qm = jnp.where(mask, q_rows[j], jnp.zeros_like(q_rows[j]))
                bias = jnp.concatenate(
                    [tab_ref[e, pl.ds(d0 + 2 * m, 1)][0] for m in range(NA_WR // 2)], axis=1)
                ss.append(_dot_nt(qm, k_wins[j]) + bias)
        mxs = [jnp.max(s, axis=-1, keepdims=True) for s in ss]
        ps = [jnp.exp(s - mx) for s, mx in zip(ss, mxs)]
        ls = [jnp.sum(p, axis=-1, keepdims=True) for p in ps]
        pvs = [_dot(p.astype(BF16), v_wins[i // 2]) for i, p in enumerate(ps)]
        for j in range(NA_ROWS_PER_STEP):
            o0 = pvs[2 * j] / ls[2 * j]
            o1 = pvs[2 * j + 1] / ls[2 * j + 1]
            o_ref[pl.ds(pl.multiple_of(rws[j] * GRID_W, GRID_W), GRID_W), :] = (
                jnp.where(head0, o0, o1).astype(o_ref.dtype))
        return carry

    lax.fori_loop(0, rows // NA_ROWS_PER_STEP, row_group, 0)


def _na_bias_table(rpb):
    qc = np.arange(GRID_W)
    kc = np.arange(GRID_W)
    cs = np.clip(qc - NA_WC // 2, 0, GRID_W - NA_WC)
    valid = (kc[None, :] >= cs[:, None]) & (kc[None, :] < cs[:, None] + NA_WC)
    dc = np.clip(kc[None, :] - qc[:, None] + NA_WC - 1, 0, 2 * NA_WC - 2)
    n_dc = 2 * NA_WC - 1
    pick = jnp.asarray(dc.reshape(1, -1) == np.arange(n_dc).reshape(-1, 1), F32)
    b = jnp.dot(rpb.astype(F32).reshape(-1, n_dc), pick, precision=lax.Precision.HIGHEST)
    b = b.reshape(rpb.shape[0], rpb.shape[1], GRID_W, GRID_W)
    b = jnp.where(jnp.asarray(valid)[None, None], b, NEG_BIG)
    return jnp.concatenate([b[:, :-1], b[:, 1:]], axis=-1)


def _na(pb3, q_gain, k_gain, table, d_b):
    bsz, t, _ = pb3.shape
    rows = t // GRID_W
    n_p = d_b // PAIR
    ones = _block_ones(PAIR, HEAD_DIM)
    gq = jnp.tile(q_gain.reshape(1, HEAD_DIM), (1, 2))
    gk = jnp.tile(k_gain.reshape(1, HEAD_DIM), (1, 2))
    n_d = table.shape[1]
    c2 = lambda bi, p: (0, 0)
    return pl.pallas_call(
        functools.partial(_na_kernel, rows=rows),
        grid=(bsz, n_p),
        in_specs=[
            pl.BlockSpec((None, t, PAIR), lambda bi, p: (bi, 0, p)),
            pl.BlockSpec((None, t, PAIR), lambda bi, p: (bi, 0, n_p + p)),
            pl.BlockSpec((None, t, PAIR), lambda bi, p: (bi, 0, 2 * n_p + p)),
            pl.BlockSpec((1, PAIR), c2),
            pl.BlockSpec((1, PAIR), c2),
            pl.BlockSpec((2, n_d, GRID_W, PAIR), lambda bi, p: (p, 0, 0, 0)),
            pl.BlockSpec((PAIR, PAIR), c2),
        ],
        out_specs=pl.BlockSpec((None, t, PAIR), lambda bi, p: (bi, 0, p)),
        out_shape=jax.ShapeDtypeStruct((bsz, t, d_b), ACT),
        scratch_shapes=[pltpu.VMEM((t, PAIR), BF16)] * 3,
        compiler_params=_params(("arbitrary", "arbitrary")),
        name="natten",
    )(pb3, pb3, pb3, gq, gk, table, ones)


def _merge_kernel(of_ref, ob_ref, bonus_ref, g_ref, yb_ref, gates_ref, x_ref,
                  lng_ref, lnb_ref, wa_ref, wb_ref, wo_ref, gffn_ref, wr_ref, br_ref,
                  ones_ref, tri_ref,
                  x1_ref, h2_ref, idx_ref, rank_ref, gw_ref, cnt_ref, carry_ref, *, d_model):
    first = jnp.logical_and(pl.program_id(0) == 0, pl.program_id(1) == 0)

    @pl.when(first)
    def _():
        carry_ref[...] = jnp.zeros_like(carry_ref)

    n_p = of_ref.shape[0]
    o = jnp.concatenate([of_ref[p].astype(F32) + ob_ref[p].astype(F32) for p in range(n_p)],
                        axis=1)
    ones = ones_ref[...]
    inv_d = 1.0 / HEAD_DIM
    mu = _mm_exact_rhs(o, ones) * inv_d
    dv = o - mu
    var = _mm_exact_rhs(dv * dv, ones) * inv_d
    y = dv * lax.rsqrt(var + GN_EPS) * lng_ref[...] + lnb_ref[...] + bonus_ref[...].astype(F32)
    ya = y * g_ref[...].astype(F32)

    gates = gates_ref[...].astype(F32)
    pa = _dot(ya.astype(BF16), wa_ref[...])
    pb = _dot(yb_ref[...].astype(BF16), wb_ref[...])
    merged = _sigmoid(gates[:, :d_model]) * pa + _sigmoid(gates[:, d_model:]) * pb
    x1 = x_ref[...] + _dot(merged.astype(BF16), wo_ref[...])
    x1_ref[...] = x1
    ms = jnp.mean(x1 * x1, axis=-1, keepdims=True)
    h2 = x1 * lax.rsqrt(ms + RMS_EPS) * gffn_ref[...]
    _store_rows_as_tiles(h2_ref, h2)

    logits = _mm3(h2, wr_ref) + br_ref[...]
    tm = logits.shape[0]
    lane = lax.broadcasted_iota(jnp.int32, (tm, LANES), 1)
    work = logits
    vals, idxs = [], []
    for _ in range(TOP_K):
        m = jnp.max(work, axis=-1, keepdims=True)
        ix = jnp.min(jnp.where(work == m, lane, LANES), axis=-1, keepdims=True)
        vals.append(m)
        idxs.append(ix)
        work = jnp.where(lane == ix, -jnp.inf, work)
    es = [jnp.exp(vk - vals[0]) for vk in vals]
    den = es[0] + es[1] + es[2] + es[3]
    member = jnp.zeros((tm, LANES), F32)
    for ix in idxs:
        member = member + (lane == ix).astype(F32)
    before = _dot(tri_ref[...], member.astype(BF16)) + carry_ref[...]
    idx_out = jnp.zeros((tm, LANES), F32)
    rank_out = jnp.zeros((tm, LANES), F32)
    gw_out = jnp.zeros((tm, LANES), F32)
    for kq in range(TOP_K):
        rk = jnp.sum(jnp.where(lane == idxs[kq], before, 0.0), axis=-1, keepdims=True)
        sel = lane == kq
        idx_out = jnp.where(sel, idxs[kq].astype(F32), idx_out)
        rank_out = jnp.where(sel, rk, rank_out)
        gw_out = jnp.where(sel, es[kq] / den, gw_out)
    idx_ref[...] = idx_out.T[:SUBLANES].astype(jnp.int32)
    rank_ref[...] = rank_out.T[:SUBLANES].astype(jnp.int32)
    gw_ref[...] = gw_out
    carry_ref[...] = carry_ref[...] + jnp.sum(member, axis=0, keepdims=True)
    cnt_ref[...] = carry_ref[...]


def _merge(o_f, o_b, bonus, g, yb, gates3, x3, lnx_g, lnx_b, w_a, w_b, w_o, g_ffn, wr_pad, br_pad, tm=512):
    bsz, n_p, t, _ = o_f.shape
    d_a = n_p * PAIR
    d_b = yb.shape[-1]
    d_model = x3.shape[-1]
    n_t = t // tm
    m = bsz * t
    ones = _block_ones(d_a, HEAD_DIM)
    tri = jnp.asarray(np.tril(np.ones((tm, tm)), -1), BF16)
    c2 = lambda bi, i: (0, 0)
    tok = lambda w: pl.BlockSpec((None, tm, w), lambda bi, i: (bi, i, 0))
    flat = lambda w: pl.BlockSpec((tm, w), lambda bi, i: (bi * n_t + i, 0))
    pair = pl.BlockSpec((None, n_p, tm, PAIR), lambda bi, i: (bi, 0, i, 0))
    return pl.pallas_call(
        functools.partial(_merge_kernel, d_model=d_model),
        grid=(bsz, n_t),
        in_specs=[
            pair, pair, tok(d_a), tok(d_a), tok(d_b), tok(2 * d_model), tok(d_model),
            pl.BlockSpec((1, d_a), c2), pl.BlockSpec((1, d_a), c2),
            pl.BlockSpec((d_a, d_model), c2), pl.BlockSpec((d_b, d_model), c2),
            pl.BlockSpec((d_model, d_model), c2), pl.BlockSpec((1, d_model), c2),
            pl.BlockSpec((2, d_model, LANES), lambda bi, i: (0, 0, 0)), pl.BlockSpec((1, LANES), c2),
            pl.BlockSpec((d_a, d_a), c2), pl.BlockSpec((tm, tm), c2),
        ],
        out_specs=[flat(d_model),
                   pl.BlockSpec((tm * SUBLANES, LANES), lambda bi, i: (bi * n_t + i, 0)),
                   pl.BlockSpec((SUBLANES, tm), lambda bi, i: (0, bi * n_t + i)),
                   pl.BlockSpec((SUBLANES, tm), lambda bi, i: (0, bi * n_t + i)),
                   flat(LANES),
                   pl.BlockSpec((1, LANES), c2)],
        out_shape=[
            jax.ShapeDtypeStruct((m, d_model), F32),
            jax.ShapeDtypeStruct((m * SUBLANES, LANES), F32),
            jax.ShapeDtypeStruct((SUBLANES, m), jnp.int32),
            jax.ShapeDtypeStruct((SUBLANES, m), jnp.int32),
            jax.ShapeDtypeStruct((m, LANES), F32),
            jax.ShapeDtypeStruct((1, LANES), F32),
        ],
        scratch_shapes=[pltpu.VMEM((1, LANES), F32)],
        compiler_params=_params(("arbitrary", "arbitrary")),
        name="merge_router",
    )(o_f, o_b, bonus, g, yb, gates3, x3, lnx_g, lnx_b, w_a, w_b, w_o, g_ffn, _hilo(wr_pad), br_pad, ones, tri)


def _tile_major(dest, tm):
    k, m = dest.shape
    return dest.reshape(k, m // tm, tm).transpose(1, 0, 2).reshape(m // tm, 1, k * tm)


def _dispatch_kernel(ps_ref, pn_ref, nu_ref, dest_ref, h_ref, xs_ref, zbuf, sem, zsem, csem, bsem):
    i = pl.program_id(0)
    tm = h_ref.shape[0]
    n_e = ps_ref.shape[0]
    nb = xs_ref.shape[0] // MOE_BLOCK

    def row_copy(t, kq):
        return pltpu.make_async_copy(
            h_ref.at[t], xs_ref.at[dest_ref[0, kq * tm + t]], sem)

    def pad_copy(r):
        return pltpu.make_async_copy(zbuf.at[0], xs_ref.at[r], zsem)

    def oct_copy(o):
        return pltpu.make_async_copy(
            zbuf.at[pl.ds(0, SUBLANES)], xs_ref.at[pl.ds(o * SUBLANES, SUBLANES)], csem)

    def blk_copy(b):
        return pltpu.make_async_copy(zbuf, xs_ref.at[pl.ds(b * MOE_BLOCK, MOE_BLOCK)], bsem)

    def pads(fn):
        def per_expert(e, c):
            start = ps_ref[e]
            end = start + pn_ref[e]
            first_oct = (start + SUBLANES - 1) // SUBLANES

            def single(r, c2):
                fn(pad_copy(r))
                return c2

            def octet(o, c2):
                fn(oct_copy(o))
                return c2
            c = lax.fori_loop(start, jnp.minimum(first_oct * SUBLANES, end), single, c)
            return lax.fori_loop(first_oct, end // SUBLANES, octet, c)
        lax.fori_loop(0, n_e, per_expert, 0)

        def per_block(b, c):
            fn(blk_copy(b))
            return c
        lax.fori_loop(nu_ref[0], nb, per_block, 0)

    @pl.when(i == 0)
    def _():
        zbuf[...] = jnp.zeros_like(zbuf)
        pads(lambda cp: cp.start())

    def issue(t, c):
        for kq in range(TOP_K):
            row_copy(t, kq).start(priority=kq % 2)
        return c

    def drain(t, c):
        for kq in range(TOP_K):
            row_copy(t, kq).wait()
        return c

    lax.fori_loop(0, tm, issue, 0, unroll=8)
    lax.fori_loop(0, tm, drain, 0, unroll=8)

    @pl.when(i == 0)
    def _():
        pads(lambda cp: cp.wait())


def _dispatch(dest, pad_start, pad_n, n_used, h2, n_pad, tm=256):
    m, ds, dl = h2.shape
    nt = m // tm
    grid_spec = pltpu.PrefetchScalarGridSpec(
        num_scalar_prefetch=3,
        grid=(nt,),
        in_specs=[
            pl.BlockSpec((None, 1, TOP_K * tm), lambda i, ps, pn, nu: (i, 0, 0), memory_space=pltpu.SMEM),
            pl.BlockSpec((tm, ds, dl), lambda i, ps, pn, nu: (i, 0, 0)),
        ],
        out_specs=pl.BlockSpec(memory_space=pl.ANY),
        scratch_shapes=[pltpu.VMEM((MOE_BLOCK, ds, dl), F32)] + [pltpu.SemaphoreType.DMA(())] * 4,
    )
    return pl.pallas_call(
        _dispatch_kernel,
        grid_spec=grid_spec,
        out_shape=jax.ShapeDtypeStruct((n_pad, ds, dl), F32),
        compiler_params=_params(("arbitrary",)),
        name="moe_dispatch",
    )(pad_start, pad_n, n_used, _tile_major(dest, tm), h2)


def _expert_kernel(blk_ref, be_ref, nu_ref, xs_ref, w1_ref, b1_ref, w2_ref, b2_ref, ys_ref,
                   w1b_ref, w2b_ref, *, d_e):
    del blk_ref
    i = pl.program_id(0)
    new_expert = jnp.logical_or(i == 0, be_ref[i] != be_ref[jnp.maximum(i - 1, 0)])

    @pl.when(new_expert)
    def _():
        w1b_ref[...] = w1_ref[...].astype(BF16)
        w2b_ref[...] = w2_ref[...].astype(BF16)

    @pl.when(i < nu_ref[0])
    def _():
        x = _load_tiles_as_rows(xs_ref, MOE_BLOCK).astype(BF16)
        u = _dot(x, w1b_ref[...]) + b1_ref[...]
        glu = jnp.minimum(u[:, :d_e], SWIGLU_LIMIT)
        lin = jnp.clip(u[:, d_e:], -SWIGLU_LIMIT, SWIGLU_LIMIT)
        act = glu * _sigmoid(SWIGLU_ALPHA * glu) * (lin + 1.0)
        _store_rows_as_tiles(ys_ref, _dot(act.astype(BF16), w2b_ref[...]) + b2_ref[...])

    @pl.when(i >= nu_ref[0])
    def _():
        ys_ref[...] = jnp.zeros_like(ys_ref)


def _experts(blk_idx, blk_e, n_used, xs, w1, b1, w2, b2):
    n_pad = xs.shape[0] // SUBLANES
    d = SUBLANES * LANES
    nb = n_pad // MOE_BLOCK
    n_e, _, d2 = w1.shape
    d_e = d2 // 2
    grid_spec = pltpu.PrefetchScalarGridSpec(
        num_scalar_prefetch=3,
        grid=(nb,),
        in_specs=[
            pl.BlockSpec((MOE_BLOCK * SUBLANES, LANES), lambda i, bi, be, nu: (bi[i], 0)),
            pl.BlockSpec((None, d, d2), lambda i, bi, be, nu: (be[i], 0, 0)),
            pl.BlockSpec((None, 1, d2), lambda i, bi, be, nu: (be[i], 0, 0)),
            pl.BlockSpec((None, d_e, d), lambda i, bi, be, nu: (be[i], 0, 0)),
            pl.BlockSpec((None, 1, d), lambda i, bi, be, nu: (be[i], 0, 0)),
        ],
        out_specs=pl.BlockSpec((MOE_BLOCK * SUBLANES, LANES), lambda i, bi, be, nu: (i, 0)),
        scratch_shapes=[pltpu.VMEM((d, d2), BF16), pltpu.VMEM((d_e, d), BF16)],
    )
    return pl.pallas_call(
        functools.partial(_expert_kernel, d_e=d_e),
        grid_spec=grid_spec,
        out_shape=jax.ShapeDtypeStruct((n_pad * SUBLANES, LANES), F32),
        compiler_params=_params(("arbitrary",)),
        name="moe_experts",
    )(blk_idx, blk_e, n_used, xs, w1, b1.reshape(n_e, 1, d2), w2, b2.reshape(n_e, 1, d))


def _combine_kernel(dest_ref, dnext_ref, ys_ref, x1_ref, gw_ref, o_ref, buf, sems):
    i = pl.program_id(0)
    nt = pl.num_programs(0)
    tm = x1_ref.shape[0]
    slot = i % 2

    def row_copy(d_ref, sl, t, kq):
        return pltpu.make_async_copy(
            ys_ref.at[d_ref[0, kq * tm + t]],
            buf.at[sl, kq, pl.ds(pl.multiple_of(t * SUBLANES, SUBLANES), SUBLANES)], sems.at[sl])

    def issue_all(d_ref, sl):
        def body(t, c):
            for kq in range(TOP_K):
                row_copy(d_ref, sl, t, kq).start(priority=kq % 2)
            return c
        lax.fori_loop(0, tm, body, 0, unroll=8)

    @pl.when(i == 0)
    def _():
        issue_all(dest_ref, 0)

    @pl.when(i + 1 < nt)
    def _():
        issue_all(dnext_ref, 1 - slot)

    def drain(t, c):
        for kq in range(TOP_K):
            row_copy(dest_ref, slot, t, kq).wait()
        return c

    lax.fori_loop(0, tm, drain, 0, unroll=8)
    gw = gw_ref[...]
    for s in range(SUBLANES):
        acc = x1_ref[:, s * LANES:(s + 1) * LANES]
        for kq in range(TOP_K):
            acc = acc + gw[:, kq:kq + 1] * buf[slot, kq, pl.ds(s, tm, stride=SUBLANES), :]
        o_ref[:, s * LANES:(s + 1) * LANES] = acc


def _combine(dest, ys, x1, gw, tm=128):
    m, d = x1.shape
    nt = m // tm
    return pl.pallas_call(
        _combine_kernel,
        grid=(nt,),
        in_specs=[
            pl.BlockSpec((None, 1, TOP_K * tm), lambda i: (i, 0, 0), memory_space=pltpu.SMEM),
            pl.BlockSpec((None, 1, TOP_K * tm), lambda i: (jnp.minimum(i + 1, nt - 1), 0, 0),
                         memory_space=pltpu.SMEM),
            pl.BlockSpec(memory_space=pl.ANY),
            pl.BlockSpec((tm, d), lambda i: (i, 0)),
            pl.BlockSpec((tm, LANES), lambda i: (i, 0)),
        ],
        out_specs=pl.BlockSpec((tm, d), lambda i: (i, 0)),
        out_shape=jax.ShapeDtypeStruct((m, d), F32),
        scratch_shapes=[pltpu.VMEM((2, TOP_K, tm * SUBLANES, LANES), F32), pltpu.SemaphoreType.DMA((2,))],
        compiler_params=_params(("arbitrary",)),
        name="moe_combine",
    )(_tile_major(dest, tm), _tile_major(dest, tm), ys, x1, gw)


def _blockdiag2(a, b):
    za = jnp.zeros((a.shape[0], b.shape[1]), a.dtype)
    zb = jnp.zeros((b.shape[0], a.shape[1]), a.dtype)
    return jnp.concatenate([jnp.concatenate([a, za], axis=1), jnp.concatenate([zb, b], axis=1)], axis=0)


def _layer(x, g_mix, w_in, mu_prev, mu_next, w0_f, w2_f, w0_b, w2_b, a0_f, a2_f, a0_b, a2_b,
           g2, k_k, k_a, r_k, lnx_g, lnx_b, q_norm_g, k_norm_g, rpb, w_a, w_b, w_o,
           g_ffn, w_router, b_router, w1, b1, w2, b2):
    bsz, t, d_model = x.shape
    m = bsz * t
    d_a = w_a.shape[0]
    d_b = w_b.shape[0]
    a_cols = mu_prev.shape[0]
    b_cols = 3 * d_b
    row = lambda a: a.reshape(1, -1).astype(F32)

    w_in_b = w_in.astype(BF16)
    pa, pb, gates = _inproj(x.reshape(m, d_model), row(g_mix), w_in_b[:, :a_cols],
                            w_in_b[:, a_cols:a_cols + b_cols], w_in_b[:, a_cols + b_cols:])

    prep = _prep(pa.reshape(bsz, t, a_cols), row(mu_prev), row(mu_next),
                 jnp.concatenate([row(w0_f), row(w0_b)], axis=1), _blockdiag2(w2_f, w2_b),
                 jnp.concatenate([row(a0_f), row(a0_b)], axis=1), _blockdiag2(a2_f, a2_b),
                 g2, row(k_k), row(k_a), row(r_k), d_a)
    r, v, kk, lw_f, lw_b, k_f, k_b, b_f, b_b, bonus, g = prep
    o_f, o_b = _scan(r, v, kk, lw_f, lw_b, k_f, k_b, b_f, b_b)

    yb = _na(pb.reshape(bsz, t, b_cols), q_norm_g, k_norm_g, _na_bias_table(rpb), d_b)

    n_e = w_router.shape[1]
    wr_pad = jnp.zeros((d_model, LANES), F32).at[:, :n_e].set(w_router)
    br_pad = jnp.full((1, LANES), NEG_BIG, F32).at[0, :n_e].set(b_router)
    x1, h2, idx, rank, gw, cnt = _merge(
        o_f, o_b, bonus, g, yb, gates.reshape(bsz, t, 2 * d_model), x,
        row(lnx_g), row(lnx_b), w_a.astype(BF16), w_b.astype(BF16), w_o.astype(BF16),
        row(g_ffn), wr_pad, br_pad)

    counts = cnt[0, :n_e].astype(jnp.int32)
    padded = ((counts + MOE_BLOCK - 1) // MOE_BLOCK) * MOE_BLOCK
    pend = jnp.cumsum(padded)
    pstart = pend - padded
    n_assign = m * TOP_K
    n_blocks = -(-n_assign // MOE_BLOCK) + n_e
    n_pad = n_blocks * MOE_BLOCK
    dest = rank[:TOP_K]
    for e in range(n_e):
        dest = dest + jnp.where(idx[:TOP_K] == e, pstart[e], 0)
    dest = dest.astype(jnp.int32)
    n_used = (pend[-1] // MOE_BLOCK).astype(jnp.int32)
    blk_idx = jnp.minimum(jnp.arange(n_blocks, dtype=jnp.int32), n_used - 1)
    blk_e = jnp.sum((blk_idx[:, None] * MOE_BLOCK >= pend[None, :]).astype(jnp.int32), axis=1)
    blk_e = jnp.minimum(blk_e, n_e - 1)

    xs = _dispatch(dest, (pstart + counts).astype(jnp.int32), (padded - counts).astype(jnp.int32),
                   n_used.reshape(1), h2.reshape(m, SUBLANES, LANES), n_pad)
    ys = _experts(blk_idx, blk_e, n_used.reshape(1), xs.reshape(n_pad * SUBLANES, LANES), w1, b1, w2, b2)
    out = _combine(dest, ys.reshape(n_pad, SUBLANES, LANES), x1, gw)
    return out.reshape(bsz, t, d_model)


def kernel(x, g_mix, w_in, mu_prev, mu_next, w0_f, w2_f, w0_b, w2_b, a0_f, a2_f, a0_b, a2_b, g2, k_k, k_a, r_k, lnx_g, lnx_b, q_norm_g, k_norm_g, rpb, w_a, w_b, w_o, g_ffn, w_router, b_router, w1, b1, w2, b2):
    for l in range(g_mix.shape[0]):
        x = _layer(x, g_mix[l], w_in[l], mu_prev[l], mu_next[l], w0_f[l], w2_f[l], w0_b[l], w2_b[l],
                   a0_f[l], a2_f[l], a0_b[l], a2_b[l], g2[l], k_k[l], k_a[l], r_k[l], lnx_g[l], lnx_b[l],
                   q_norm_g[l], k_norm_g[l], rpb[l], w_a[l], w_b[l], w_o[l], g_ffn[l], w_router[l],
                   b_router[l], w1[l], b1[l], w2[l], b2[l])
    return x
```

```python
import functools
import math

import numpy as np
import jax
import jax.numpy as jnp
from jax import lax
from jax.experimental import pallas as pl
from jax.experimental.pallas import tpu as pltpu

F32 = jnp.float32
BF16 = jnp.bfloat16
ACT = jnp.bfloat16

LANES = 128
SUBLANES = 8
HEAD_DIM = 64
PAIR = 2 * HEAD_DIM
GRID_W = 64
NA_WR = 8
NA_WC = 16
W_LORA = 64
A_LORA = 64
G_LORA = 128
DECAY_SCALE = math.exp(-0.5)
GN_EPS = 64e-5
RMS_EPS = 1e-5
N_EXPERTS = 32
TOP_K = 4
MOE_BLOCK = 512
SWIGLU_LIMIT = 7.0
SWIGLU_ALPHA = 1.702
NEG_BIG = -1e30
CHUNK = 128
NA_ROWS_PER_STEP = 32
VMEM_LIMIT = 56 * 1024 * 1024


def _dot(a, b):
    return jnp.dot(a, b, preferred_element_type=F32)


def _dot_nt(a, b):
    return lax.dot_general(a, b, (((1,), (1,)), ((), ())), preferred_element_type=F32)


def _split(a):
    hi = a.astype(BF16)
    lo = (a - hi.astype(F32)).astype(BF16)
    return hi, lo


def _mm_exact_rhs(a, b_bf16):
    hi, lo = _split(a)
    return _dot(hi, b_bf16) + _dot(lo, b_bf16)


def _hilo(b):
    return jnp.stack(_split(b))


def _mm3(a, b_hilo):
    ah, al = _split(a)
    bh = b_hilo[0]
    return _dot(ah, bh) + _dot(al, bh) + _dot(ah, b_hilo[1])


def _store_rows_as_tiles(ref, val):
    n = val.shape[0]
    for s in range(SUBLANES):
        ref[pl.ds(s, n, stride=SUBLANES), :] = val[:, s * LANES:(s + 1) * LANES]


def _load_tiles_as_rows(ref, n):
    return jnp.concatenate([ref[pl.ds(s, n, stride=SUBLANES), :] for s in range(SUBLANES)], axis=1)


def _sigmoid(x):
    return 1.0 / (1.0 + jnp.exp(-x))


def _params(sem):
    return pltpu.CompilerParams(dimension_semantics=sem, vmem_limit_bytes=VMEM_LIMIT)


def _block_ones(n, blk):
    i = np.arange(n) // blk
    return jnp.asarray(i[:, None] == i[None, :], BF16)


def _inproj_kernel(x_ref, g_ref, wa_ref, wb_ref, wg_ref, pa_ref, pb_ref, pg_ref):
    x = x_ref[...]
    ms = jnp.mean(x * x, axis=-1, keepdims=True)
    h = (x * lax.rsqrt(ms + RMS_EPS) * g_ref[...]).astype(BF16)
    pa_ref[...] = _dot(h, wa_ref[...]).astype(pa_ref.dtype)
    pb_ref[...] = _dot(h, wb_ref[...]).astype(pb_ref.dtype)
    pg_ref[...] = _dot(h, wg_ref[...]).astype(pg_ref.dtype)


def _inproj(x2, g_mix, w_a, w_b, w_g, tm=512):
    m, d = x2.shape
    na, nb, ng = w_a.shape[1], w_b.shape[1], w_g.shape[1]
    full = lambda i: (0, 0)
    return pl.pallas_call(
        _inproj_kernel,
        grid=(m // tm,),
        in_specs=[
            pl.BlockSpec((tm, d), lambda i: (i, 0)),
            pl.BlockSpec((1, d), full),
            pl.BlockSpec((d, na), full),
            pl.BlockSpec((d, nb), full),
            pl.BlockSpec((d, ng), full),
        ],
        out_specs=[
            pl.BlockSpec((tm, na), lambda i: (i, 0)),
            pl.BlockSpec((tm, nb), lambda i: (i, 0)),
            pl.BlockSpec((tm, ng), lambda i: (i, 0)),
        ],
        out_shape=[
            jax.ShapeDtypeStruct((m, na), ACT),
            jax.ShapeDtypeStruct((m, nb), ACT),
            jax.ShapeDtypeStruct((m, ng), ACT),
        ],
        compiler_params=_params(("arbitrary",)),
        name="inproj",
    )(x2, g_mix, w_a, w_b, w_g)


def _prep_kernel(p_ref, prev_ref, next_ref, mup_ref, mun_ref, w0_ref, w2_ref, a0_ref, a2_ref,
                 g2_ref, kk_ref, ka_ref, rk_ref, ones_ref,
                 r_o, v_o, kk_o, lwf_o, lwb_o, kf_o, kb_o, bf_o, bb_o, bonus_o, g_o, *, d_a):
    i = pl.program_id(1)
    n_t = pl.num_programs(1)
    p = p_ref[...].astype(F32)
    tt = p.shape[0]
    halo = prev_ref.shape[0]
    prow = jnp.where(i > 0, prev_ref[halo - 1:halo, :].astype(F32), 0.0)
    nrow = jnp.where(i < n_t - 1, next_ref[0:1, :].astype(F32), 0.0)
    rid = lax.broadcasted_iota(jnp.int32, (tt, 1), 0)
    prev = jnp.where(rid == 0, prow, pltpu.roll(p, 1, axis=0))
    nxt = jnp.where(rid == tt - 1, nrow, pltpu.roll(p, tt - 1, axis=0))
    xa = p + mup_ref[...] * (prev - p) + mun_ref[...] * (nxt - p)

    r = xa[:, 0:d_a]
    k = xa[:, d_a:2 * d_a]
    v = xa[:, 2 * d_a:3 * d_a]
    o = 3 * d_a
    lw = xa[:, o:o + 2 * W_LORA]
    la = xa[:, o + 2 * W_LORA:o + 2 * W_LORA + 2 * A_LORA]
    lg = xa[:, o + 2 * W_LORA + 2 * A_LORA:]

    dpre = w0_ref[...] + _mm3(jnp.tanh(lw), w2_ref)
    apre = a0_ref[...] + _mm3(la, a2_ref)
    g = _mm3(_sigmoid(lg), g2_ref)
    logw = -DECAY_SCALE * _sigmoid(dpre)
    a = _sigmoid(apre)

    ones = ones_ref[...]
    kkr = k * kk_ref[...]
    ss = _mm_exact_rhs(kkr * kkr, ones)
    kk = kkr / jnp.maximum(jnp.sqrt(ss), 1e-12)

    ka = ka_ref[...]
    k_f = k * (1.0 + (a[:, :d_a] - 1.0) * ka)
    k_b = k * (1.0 + (a[:, d_a:] - 1.0) * ka)
    b_f = kk * a[:, :d_a]
    b_b = kk * a[:, d_a:]
    rk = rk_ref[...]
    bon = _mm_exact_rhs(r * (k_f + k_b) * rk, ones) * v

    bonus_o[...] = bon.astype(bonus_o.dtype)
    g_o[...] = g.astype(g_o.dtype)
    for pi in range(d_a // PAIR):
        sl = slice(pi * PAIR, (pi + 1) * PAIR)
        r_o[pi] = r[:, sl].astype(r_o.dtype)
        v_o[pi] = v[:, sl].astype(v_o.dtype)
        kk_o[pi] = kk[:, sl].astype(kk_o.dtype)
        lwf_o[pi] = logw[:, sl]
        lwb_o[pi] = logw[:, d_a + pi * PAIR:d_a + (pi + 1) * PAIR]
        kf_o[pi] = k_f[:, sl].astype(kf_o.dtype)
        kb_o[pi] = k_b[:, sl].astype(kb_o.dtype)
        bf_o[pi] = b_f[:, sl].astype(bf_o.dtype)
        bb_o[pi] = b_b[:, sl].astype(bb_o.dtype)


def _prep(pa3, mu_prev, mu_next, w0c, w2blk, a0c, a2blk, g2, k_k, k_a, r_k, d_a, tt=512):
    b, t, ac = pa3.shape
    n_t = t // tt
    n_p = d_a // PAIR
    ones = _block_ones(d_a, HEAD_DIM)
    c2 = lambda bi, i: (0, 0)
    c3 = lambda bi, i: (0, 0, 0)
    pair_spec = pl.BlockSpec((None, n_p, tt, PAIR), lambda bi, i: (bi, 0, i, 0))
    pair_act = jax.ShapeDtypeStruct((b, n_p, t, PAIR), ACT)
    pair_f32 = jax.ShapeDtypeStruct((b, n_p, t, PAIR), F32)
    flat_spec = pl.BlockSpec((None, tt, d_a), lambda bi, i: (bi, i, 0))
    flat_shape = jax.ShapeDtypeStruct((b, t, d_a), ACT)
    halo = 16
    r8 = tt // halo
    return pl.pallas_call(
        functools.partial(_prep_kernel, d_a=d_a),
        grid=(b, n_t),
        in_specs=[
            pl.BlockSpec((None, tt, ac), lambda bi, i: (bi, i, 0)),
            pl.BlockSpec((None, halo, ac), lambda bi, i: (bi, jnp.maximum(i * r8 - 1, 0), 0)),
            pl.BlockSpec((None, halo, ac), lambda bi, i: (bi, jnp.minimum((i + 1) * r8, t // halo - 1), 0)),
            pl.BlockSpec((1, ac), c2),
            pl.BlockSpec((1, ac), c2),
            pl.BlockSpec((1, 2 * d_a), c2),
            pl.BlockSpec((2, 2 * W_LORA, 2 * d_a), c3),
            pl.BlockSpec((1, 2 * d_a), c2),
            pl.BlockSpec((2, 2 * A_LORA, 2 * d_a), c3),
            pl.BlockSpec((2, G_LORA, d_a), c3),
            pl.BlockSpec((1, d_a), c2),
            pl.BlockSpec((1, d_a), c2),
            pl.BlockSpec((1, d_a), c2),
            pl.BlockSpec((d_a, d_a), c2),
        ],
        out_specs=[pair_spec] * 9 + [flat_spec] * 2,
        out_shape=[pair_act] * 3 + [pair_f32] * 2 + [pair_act] * 4 + [flat_shape] * 2,
        compiler_params=_params(("arbitrary", "arbitrary")),
        name="rwkv_prep",
    )(pa3, pa3, pa3, mu_prev, mu_next, w0c, _hilo(w2blk), a0c, _hilo(a2blk), _hilo(g2), k_k, k_a, r_k, ones)


def _mm_exact_rhs_left(tri_bf16, x):
    hi, lo = _split(x)
    return _dot(tri_bf16, hi) + _dot(tri_bf16, lo)


def _scan_stage(items, s_refs):
    c = items[0][0].shape[0]
    ri = lax.broadcasted_iota(jnp.int32, (c, c), 0)
    ci = lax.broadcasted_iota(jnp.int32, (c, c), 1)
    lane = lax.broadcasted_iota(jnp.int32, (1, PAIR), 1)
    m0 = (lane < HEAD_DIM).astype(F32)
    m1 = 1.0 - m0
    eye = (ri == ci).astype(F32)
    hi_ = lax.broadcasted_iota(jnp.int32, (PAIR, PAIR), 0) // HEAD_DIM
    hj_ = lax.broadcasted_iota(jnp.int32, (PAIR, PAIR), 1) // HEAD_DIM
    same_head = hi_ == hj_
    zero = jnp.zeros((), F32)
    n = len(items)

    def masks(reverse):
        if reverse:
            return ci >= ri, ci > ri, 0
        return ci <= ri, ci < ri, c - 1

    cums = [_mm_exact_rhs_left(masks(it[6])[0].astype(BF16), it[3]) for it in items]

    pre = []
    for (r, kk, v, lw, k, b, reverse), cum in zip(items, cums):
        end = masks(reverse)[2]
        cmid = cum[c // 2:c // 2 + 1, :]
        cend = cum[end:end + 1, :]
        r_abs = r * jnp.exp(cum)
        a_abs = -kk * jnp.exp(cum - lw)
        to_mid = jnp.exp(-cmid)
        from_mid = jnp.exp(cmid - cum)
        to_end = jnp.exp(cend - cum)
        pre.append(dict(
            r_abs=r_abs, a_abs=a_abs, r_rel=r_abs * to_mid, a_rel=a_abs * to_mid,
            k_rel=k * from_mid, b_rel=b * from_mid, k_end=k * to_end, b_end=b * to_end,
            d_tot=jnp.exp(cend), v=v))

    grams = []
    for p in pre:
        lhs = jnp.concatenate([p["r_rel"] * m0, p["r_rel"] * m1, p["a_rel"] * m0, p["a_rel"] * m1],
                              axis=0).astype(BF16)
        rhs = jnp.concatenate([p["k_rel"], p["b_rel"]], axis=0).astype(BF16)
        grams.append(_dot_nt(lhs, rhs))

    pws, ts = [], []
    for it, gram in zip(items, grams):
        strict = masks(it[6])[1]
        for e in range(2):
            a_ab = jnp.where(strict, gram[(2 + e) * c:(3 + e) * c, c:2 * c], zero)
            pws.append(a_ab)
            ts.append(eye + a_ab)
    for _ in range(int(round(math.log2(c))) - 1):
        pws = [_dot(pw.astype(BF16), pw.astype(BF16)) for pw in pws]
        ts = [t + _dot(t.astype(BF16), pw.astype(BF16)) for t, pw in zip(ts, pws)]

    v_blks = [jnp.concatenate([p["v"] * m0, p["v"] * m1], axis=0).astype(BF16) for p in pre]
    akvs, o_rks, rb_cats = [], [], []
    for it, gram, v_blk in zip(items, grams, v_blks):
        incl, strict, _ = masks(it[6])
        ak_cat = jnp.concatenate(
            [jnp.where(strict, gram[(2 + e) * c:(3 + e) * c, 0:c], zero) for e in range(2)], axis=1)
        rk_cat = jnp.concatenate(
            [jnp.where(incl, gram[e * c:(e + 1) * c, 0:c], zero) for e in range(2)], axis=1)
        rb_cats.append(jnp.concatenate(
            [jnp.where(incl, gram[e * c:(e + 1) * c, c:2 * c], zero) for e in range(2)], axis=1).astype(BF16))
        akvs.append(_dot(ak_cat.astype(BF16), v_blk))
        o_rks.append(_dot(rk_cat.astype(BF16), v_blk))

    xs = []
    for i, (p, akv) in enumerate(zip(pre, akvs)):
        t_cat = jnp.concatenate([ts[2 * i], ts[2 * i + 1]], axis=1).astype(BF16)
        y_blk = jnp.concatenate([
            jnp.concatenate([p["a_abs"] * m0, akv * m0], axis=1),
            jnp.concatenate([p["a_abs"] * m1, akv * m1], axis=1)], axis=0).astype(BF16)
        xs.append(_dot(t_cat, y_blk))

    s0s = [s_ref[...] for s_ref in s_refs]
    s0bs = [s0.astype(BF16) for s0 in s0s]
    us = [_dot(x[:, 0:PAIR].astype(BF16), s0b) + x[:, PAIR:2 * PAIR] for x, s0b in zip(xs, s0bs)]
    outs = []
    for i in range(n):
        p, u = pre[i], us[i]
        u_blk = jnp.concatenate([u * m0, u * m1], axis=0).astype(BF16)
        outs.append(_dot(p["r_abs"].astype(BF16), s0bs[i]) + _dot(rb_cats[i], u_blk) + o_rks[i])
        kb_t = jnp.concatenate([p["b_end"].T, p["k_end"].T], axis=1).astype(BF16)
        uv = jnp.concatenate([u, p["v"]], axis=0).astype(BF16)
        d_col = jnp.broadcast_to(p["d_tot"], (PAIR, PAIR)).T
        s_refs[i][...] = jnp.where(same_head, d_col * s0s[i] + _dot(kb_t, uv), zero)
    return outs


def _scan_kernel(rf, vf, kkf, lwf, kf, bf, rb, vb, kkb, lwb, kb, bb, of_ref, ob_ref, s_ref):
    @pl.when(pl.program_id(1) == 0)
    def _():
        s_ref[...] = jnp.zeros_like(s_ref)

    n_p = rf.shape[0]
    items, s_refs = [], []
    for p in range(n_p):
        ld = lambda ref: ref[p].astype(F32)
        items.append((ld(rf), ld(kkf), ld(vf), lwf[p], ld(kf), ld(bf), False))
        s_refs.append(s_ref.at[2 * p])
        items.append((ld(rb), ld(kkb), ld(vb), lwb[p], ld(kb), ld(bb), True))
        s_refs.append(s_ref.at[2 * p + 1])
    outs = _scan_stage(items, s_refs)
    for p in range(n_p):
        of_ref[p] = outs[2 * p].astype(of_ref.dtype)
        ob_ref[p] = outs[2 * p + 1].astype(ob_ref.dtype)


def _scan(r, v, kk, lw_f, lw_b, k_f, k_b, b_f, b_b):
    bsz, n_p, t, _ = r.shape
    nc = t // CHUNK
    fwd = pl.BlockSpec((None, n_p, CHUNK, PAIR), lambda bi, c: (bi, 0, c, 0))
    bwd = pl.BlockSpec((None, n_p, CHUNK, PAIR), lambda bi, c: (bi, 0, nc - 1 - c, 0))
    shape = jax.ShapeDtypeStruct((bsz, n_p, t, PAIR), ACT)
    return pl.pallas_call(
        _scan_kernel,
        grid=(bsz, nc),
        in_specs=[fwd] * 6 + [bwd] * 6,
        out_specs=[fwd, bwd],
        out_shape=[shape, shape],
        scratch_shapes=[pltpu.VMEM((2 * n_p, PAIR, PAIR), F32)],
        compiler_params=_params(("arbitrary", "arbitrary")),
        name="rwkv_scan",
    )(r, v, kk, lw_f, k_f, b_f, r, v, kk, lw_b, k_b, b_b)


def _na_kernel(q_ref, k_ref, v_ref, gq_ref, gk_ref, tab_ref, ones_ref, o_ref, qn_s, kn_s, vb_s, *, rows):
    ones = ones_ref[...]
    scale = HEAD_DIM ** -0.5
    q = q_ref[...].astype(F32)
    k = k_ref[...].astype(F32)
    inv_d = 1.0 / HEAD_DIM
    qn = q * lax.rsqrt(_mm_exact_rhs(q * q, ones) * inv_d + RMS_EPS) * (gq_ref[...] * scale)
    kn = k * lax.rsqrt(_mm_exact_rhs(k * k, ones) * inv_d + RMS_EPS) * gk_ref[...]
    qn_s[...] = qn.astype(BF16)
    kn_s[...] = kn.astype(BF16)
    vb_s[...] = v_ref[...].astype(BF16)
    lane = lax.broadcasted_iota(jnp.int32, (1, PAIR), 1)
    head0 = lane < HEAD_DIM
    win = NA_WR * GRID_W

    def row_group(gi, carry):
        rws = [gi * NA_ROWS_PER_STEP + j for j in range(NA_ROWS_PER_STEP)]
        rss = [jnp.clip(r - NA_WR // 2, 0, rows - NA_WR) for r in rws]
        q_rows = [qn_s[pl.ds(pl.multiple_of(r * GRID_W, GRID_W), GRID_W), :] for r in rws]
        k_wins = [kn_s[pl.ds(pl.multiple_of(rs * GRID_W, GRID_W), win), :] for rs in rss]
        v_wins = [vb_s[pl.ds(pl.multiple_of(rs * GRID_W, GRID_W), win), :] for rs in rss]
        ss = []
        for j in range(NA_ROWS_PER_STEP):
            d0 = rss[j] - rws[j] + NA_WR - 1
            for e in range(2):
                mask = head0 if e == 0 else jnp.logical_not(head0)
                qm = jnp.where(mask, q_rows[j], jnp.zeros_like(q_rows[j]))
                bias = jnp.concatenate(
                    [tab_ref[e, pl.ds(d0 + 2 * m, 1)][0] for m in range(NA_WR // 2)], axis=1)
                ss.append(_dot_nt(qm, k_wins[j]) + bias)
        mxs = [jnp.max(s, axis=-1, keepdims=True) for s in ss]
        ps = [jnp.exp(s - mx) for s, mx in zip(ss, mxs)]
        ls = [jnp.sum(p, axis=-1, keepdims=True) for p in ps]
        pvs = [_dot(p.astype(BF16), v_wins[i // 2]) for i, p in enumerate(ps)]
        for j in range(NA_ROWS_PER_STEP):
            o0 = pvs[2 * j] / ls[2 * j]
            o1 = pvs[2 * j + 1] / ls[2 * j + 1]
            o_ref[pl.ds(pl.multiple_of(rws[j] * GRID_W, GRID_W), GRID_W), :] = (
                jnp.where(head0, o0, o1).astype(o_ref.dtype))
        return carry

    lax.fori_loop(0, rows // NA_ROWS_PER_STEP, row_group, 0)


def _na_bias_table(rpb):
    qc = np.arange(GRID_W)
    kc = np.arange(GRID_W)
    cs = np.clip(qc - NA_WC // 2, 0, GRID_W - NA_WC)
    valid = (kc[None, :] >= cs[:, None]) & (kc[None, :] < cs[:, None] + NA_WC)
    dc = np.clip(kc[None, :] - qc[:, None] + NA_WC - 1, 0, 2 * NA_WC - 2)
    n_dc = 2 * NA_WC - 1
    pick = jnp.asarray(dc.reshape(1, -1) == np.arange(n_dc).reshape(-1, 1), F32)
    b = jnp.dot(rpb.astype(F32).reshape(-1, n_dc), pick, precision=lax.Precision.HIGHEST)
    b = b.reshape(rpb.shape[0], rpb.shape[1], GRID_W, GRID_W)
    b = jnp.where(jnp.asarray(valid)[None, None], b, NEG_BIG)
    return jnp.concatenate([b[:, :-1], b[:, 1:]], axis=-1)


def _na(pb3, q_gain, k_gain, table, d_b):
    bsz, t, _ = pb3.shape
    rows = t // GRID_W
    n_p = d_b // PAIR
    ones = _block_ones(PAIR, HEAD_DIM)
    gq = jnp.tile(q_gain.reshape(1, HEAD_DIM), (1, 2))
    gk = jnp.tile(k_gain.reshape(1, HEAD_DIM), (1, 2))
    n_d = table.shape[1]
    c2 = lambda bi, p: (0, 0)
    return pl.pallas_call(
        functools.partial(_na_kernel, rows=rows),
        grid=(bsz, n_p),
        in_specs=[
            pl.BlockSpec((None, t, PAIR), lambda bi, p: (bi, 0, p)),
            pl.BlockSpec((None, t, PAIR), lambda bi, p: (bi, 0, n_p + p)),
            pl.BlockSpec((None, t, PAIR), lambda bi, p: (bi, 0, 2 * n_p + p)),
            pl.BlockSpec((1, PAIR), c2),
            pl.BlockSpec((1, PAIR), c2),
            pl.BlockSpec((2, n_d, GRID_W, PAIR), lambda bi, p: (p, 0, 0, 0)),
            pl.BlockSpec((PAIR, PAIR), c2),
        ],
        out_specs=pl.BlockSpec((None, t, PAIR), lambda bi, p: (bi, 0, p)),
        out_shape=jax.ShapeDtypeStruct((bsz, t, d_b), ACT),
        scratch_shapes=[pltpu.VMEM((t, PAIR), BF16)] * 3,
        compiler_params=_params(("arbitrary", "arbitrary")),
        name="natten",
    )(pb3, pb3, pb3, gq, gk, table, ones)


def _merge_kernel(of_ref, ob_ref, bonus_ref, g_ref, yb_ref, gates_ref, x_ref,
                  lng_ref, lnb_ref, wa_ref, wb_ref, wo_ref, gffn_ref, wr_ref, br_ref,
                  ones_ref, tri_ref,
                  x1_ref, h2_ref, idx_ref, rank_ref, gw_ref, cnt_ref, carry_ref, *, d_model):
    first = jnp.logical_and(pl.program_id(0) == 0, pl.program_id(1) == 0)

    @pl.when(first)
    def _():
        carry_ref[...] = jnp.zeros_like(carry_ref)

    n_p = of_ref.shape[0]
    o = jnp.concatenate([of_ref[p].astype(F32) + ob_ref[p].astype(F32) for p in range(n_p)],
                        axis=1)
    ones = ones_ref[...]
    inv_d = 1.0 / HEAD_DIM
    mu = _mm_exact_rhs(o, ones) * inv_d
    dv = o - mu
    var = _mm_exact_rhs(dv * dv, ones) * inv_d
    y = dv * lax.rsqrt(var + GN_EPS) * lng_ref[...] + lnb_ref[...] + bonus_ref[...].astype(F32)
    ya = y * g_ref[...].astype(F32)

    gates = gates_ref[...].astype(F32)
    pa = _dot(ya.astype(BF16), wa_ref[...])
    pb = _dot(yb_ref[...].astype(BF16), wb_ref[...])
    merged = _sigmoid(gates[:, :d_model]) * pa + _sigmoid(gates[:, d_model:]) * pb
    x1 = x_ref[...] + _dot(merged.astype(BF16), wo_ref[...])
    x1_ref[...] = x1
    ms = jnp.mean(x1 * x1, axis=-1, keepdims=True)
    h2 = x1 * lax.rsqrt(ms + RMS_EPS) * gffn_ref[...]
    _store_rows_as_tiles(h2_ref, h2)

    logits = _mm3(h2, wr_ref) + br_ref[...]
    tm = logits.shape[0]
    lane = lax.broadcasted_iota(jnp.int32, (tm, LANES), 1)
    work = logits
    vals, idxs = [], []
    for _ in range(TOP_K):
        m = jnp.max(work, axis=-1, keepdims=True)
        ix = jnp.min(jnp.where(work == m, lane, LANES), axis=-1, keepdims=True)
        vals.append(m)
        idxs.append(ix)
        work = jnp.where(lane == ix, -jnp.inf, work)
    es = [jnp.exp(vk - vals[0]) for vk in vals]
    den = es[0] + es[1] + es[2] + es[3]
    member = jnp.zeros((tm, LANES), F32)
    for ix in idxs:
        member = member + (lane == ix).astype(F32)
    before = _dot(tri_ref[...], member.astype(BF16)) + carry_ref[...]
    idx_out = jnp.zeros((tm, LANES), F32)
    rank_out = jnp.zeros((tm, LANES), F32)
    gw_out = jnp.zeros((tm, LANES), F32)
    for kq in range(TOP_K):
        rk = jnp.sum(jnp.where(lane == idxs[kq], before, 0.0), axis=-1, keepdims=True)
        sel = lane == kq
        idx_out = jnp.where(sel, idxs[kq].astype(F32), idx_out)
        rank_out = jnp.where(sel, rk, rank_out)
        gw_out = jnp.where(sel, es[kq] / den, gw_out)
    idx_ref[...] = idx_out.T[:SUBLANES].astype(jnp.int32)
    rank_ref[...] = rank_out.T[:SUBLANES].astype(jnp.int32)
    gw_ref[...] = gw_out
    carry_ref[...] = carry_ref[...] + jnp.sum(member, axis=0, keepdims=True)
    cnt_ref[...] = carry_ref[...]


def _merge(o_f, o_b, bonus, g, yb, gates3, x3, lnx_g, lnx_b, w_a, w_b, w_o, g_ffn, wr_pad, br_pad, tm=512):
    bsz, n_p, t, _ = o_f.shape
    d_a = n_p * PAIR
    d_b = yb.shape[-1]
    d_model = x3.shape[-1]
    n_t = t // tm
    m = bsz * t
    ones = _block_ones(d_a, HEAD_DIM)
    tri = jnp.asarray(np.tril(np.ones((tm, tm)), -1), BF16)
    c2 = lambda bi, i: (0, 0)
    tok = lambda w: pl.BlockSpec((None, tm, w), lambda bi, i: (bi, i, 0))
    flat = lambda w: pl.BlockSpec((tm, w), lambda bi, i: (bi * n_t + i, 0))
    pair = pl.BlockSpec((None, n_p, tm, PAIR), lambda bi, i: (bi, 0, i, 0))
    return pl.pallas_call(
        functools.partial(_merge_kernel, d_model=d_model),
        grid=(bsz, n_t),
        in_specs=[
            pair, pair, tok(d_a), tok(d_a), tok(d_b), tok(2 * d_model), tok(d_model),
            pl.BlockSpec((1, d_a), c2), pl.BlockSpec((1, d_a), c2),
            pl.BlockSpec((d_a, d_model), c2), pl.BlockSpec((d_b, d_model), c2),
            pl.BlockSpec((d_model, d_model), c2), pl.BlockSpec((1, d_model), c2),
            pl.BlockSpec((2, d_model, LANES), lambda bi, i: (0, 0, 0)), pl.BlockSpec((1, LANES), c2),
            pl.BlockSpec((d_a, d_a), c2), pl.BlockSpec((tm, tm), c2),
        ],
        out_specs=[flat(d_model),
                   pl.BlockSpec((tm * SUBLANES, LANES), lambda bi, i: (bi * n_t + i, 0)),
                   pl.BlockSpec((SUBLANES, tm), lambda bi, i: (0, bi * n_t + i)),
                   pl.BlockSpec((SUBLANES, tm), lambda bi, i: (0, bi * n_t + i)),
                   flat(LANES),
                   pl.BlockSpec((1, LANES), c2)],
        out_shape=[
            jax.ShapeDtypeStruct((m, d_model), F32),
            jax.ShapeDtypeStruct((m * SUBLANES, LANES), F32),
            jax.ShapeDtypeStruct((SUBLANES, m), jnp.int32),
            jax.ShapeDtypeStruct((SUBLANES, m), jnp.int32),
            jax.ShapeDtypeStruct((m, LANES), F32),
            jax.ShapeDtypeStruct((1, LANES), F32),
        ],
        scratch_shapes=[pltpu.VMEM((1, LANES), F32)],
        compiler_params=_params(("arbitrary", "arbitrary")),
        name="merge_router",
    )(o_f, o_b, bonus, g, yb, gates3, x3, lnx_g, lnx_b, w_a, w_b, w_o, g_ffn, _hilo(wr_pad), br_pad, ones, tri)


def _tile_major(dest, tm):
    k, m = dest.shape
    return dest.reshape(k, m // tm, tm).transpose(1, 0, 2).reshape(m // tm, 1, k * tm)


def _dispatch_kernel(ps_ref, pn_ref, nu_ref, dest_ref, h_ref, xs_ref, zbuf, sem, zsem, csem, bsem):
    i = pl.program_id(0)
    tm = h_ref.shape[0]
    n_e = ps_ref.shape[0]
    nb = xs_ref.shape[0] // MOE_BLOCK

    def row_copy(t, kq):
        return pltpu.make_async_copy(
            h_ref.at[t], xs_ref.at[dest_ref[0, kq * tm + t]], sem)

    def pad_copy(r):
        return pltpu.make_async_copy(zbuf.at[0], xs_ref.at[r], zsem)

    def oct_copy(o):
        return pltpu.make_async_copy(
            zbuf.at[pl.ds(0, SUBLANES)], xs_ref.at[pl.ds(o * SUBLANES, SUBLANES)], csem)

    def blk_copy(b):
        return pltpu.make_async_copy(zbuf, xs_ref.at[pl.ds(b * MOE_BLOCK, MOE_BLOCK)], bsem)

    def pads(fn):
        def per_expert(e, c):
            start = ps_ref[e]
            end = start + pn_ref[e]
            first_oct = (start + SUBLANES - 1) // SUBLANES

            def single(r, c2):
                fn(pad_copy(r))
                return c2

            def octet(o, c2):
                fn(oct_copy(o))
                return c2
            c = lax.fori_loop(start, jnp.minimum(first_oct * SUBLANES, end), single, c)
            return lax.fori_loop(first_oct, end // SUBLANES, octet, c)
        lax.fori_loop(0, n_e, per_expert, 0)

        def per_block(b, c):
            fn(blk_copy(b))
            return c
        lax.fori_loop(nu_ref[0], nb, per_block, 0)

    @pl.when(i == 0)
    def _():
        zbuf[...] = jnp.zeros_like(zbuf)
        pads(lambda cp: cp.start())

    def issue(t, c):
        for kq in range(TOP_K):
            row_copy(t, kq).start(priority=kq % 2)
        return c

    def drain(t, c):
        for kq in range(TOP_K):
            row_copy(t, kq).wait()
        return c

    lax.fori_loop(0, tm, issue, 0, unroll=8)
    lax.fori_loop(0, tm, drain, 0, unroll=8)

    @pl.when(i == 0)
    def _():
        pads(lambda cp: cp.wait())


def _dispatch(dest, pad_start, pad_n, n_used, h2, n_pad, tm=256):
    m, ds, dl = h2.shape
    nt = m // tm
    grid_spec = pltpu.PrefetchScalarGridSpec(
        num_scalar_prefetch=3,
        grid=(nt,),
        in_specs=[
            pl.BlockSpec((None, 1, TOP_K * tm), lambda i, ps, pn, nu: (i, 0, 0), memory_space=pltpu.SMEM),
            pl.BlockSpec((tm, ds, dl), lambda i, ps, pn, nu: (i, 0, 0)),
        ],
        out_specs=pl.BlockSpec(memory_space=pl.ANY),
        scratch_shapes=[pltpu.VMEM((MOE_BLOCK, ds, dl), F32)] + [pltpu.SemaphoreType.DMA(())] * 4,
    )
    return pl.pallas_call(
        _dispatch_kernel,
        grid_spec=grid_spec,
        out_shape=jax.ShapeDtypeStruct((n_pad, ds, dl), F32),
        compiler_params=_params(("arbitrary",)),
        name="moe_dispatch",
    )(pad_start, pad_n, n_used, _tile_major(dest, tm), h2)


def _expert_kernel(blk_ref, be_ref, nu_ref, xs_ref, w1_ref, b1_ref, w2_ref, b2_ref, ys_ref,
                   w1b_ref, w2b_ref, *, d_e):
    del blk_ref
    i = pl.program_id(0)
    new_expert = jnp.logical_or(i == 0, be_ref[i] != be_ref[jnp.maximum(i - 1, 0)])

    @pl.when(new_expert)
    def _():
        w1b_ref[...] = w1_ref[...].astype(BF16)
        w2b_ref[...] = w2_ref[...].astype(BF16)

    @pl.when(i < nu_ref[0])
    def _():
        x = _load_tiles_as_rows(xs_ref, MOE_BLOCK).astype(BF16)
        u = _dot(x, w1b_ref[...]) + b1_ref[...]
        glu = jnp.minimum(u[:, :d_e], SWIGLU_LIMIT)
        lin = jnp.clip(u[:, d_e:], -SWIGLU_LIMIT, SWIGLU_LIMIT)
        act = glu * _sigmoid(SWIGLU_ALPHA * glu) * (lin + 1.0)
        _store_rows_as_tiles(ys_ref, _dot(act.astype(BF16), w2b_ref[...]) + b2_ref[...])

    @pl.when(i >= nu_ref[0])
    def _():
        ys_ref[...] = jnp.zeros_like(ys_ref)


def _experts(blk_idx, blk_e, n_used, xs, w1, b1, w2, b2):
    n_pad = xs.shape[0] // SUBLANES
    d = SUBLANES * LANES
    nb = n_pad // MOE_BLOCK
    n_e, _, d2 = w1.shape
    d_e = d2 // 2
    grid_spec = pltpu.PrefetchScalarGridSpec(
        num_scalar_prefetch=3,
        grid=(nb,),
        in_specs=[
            pl.BlockSpec((MOE_BLOCK * SUBLANES, LANES), lambda i, bi, be, nu: (bi[i], 0)),
            pl.BlockSpec((None, d, d2), lambda i, bi, be, nu: (be[i], 0, 0)),
            pl.BlockSpec((None, 1, d2), lambda i, bi, be, nu: (be[i], 0, 0)),
            pl.BlockSpec((None, d_e, d), lambda i, bi, be, nu: (be[i], 0, 0)),
            pl.BlockSpec((None, 1, d), lambda i, bi, be, nu: (be[i], 0, 0)),
        ],
        out_specs=pl.BlockSpec((MOE_BLOCK * SUBLANES, LANES), lambda i, bi, be, nu: (i, 0)),
        scratch_shapes=[pltpu.VMEM((d, d2), BF16), pltpu.VMEM((d_e, d), BF16)],
    )
    return pl.pallas_call(
        functools.partial(_expert_kernel, d_e=d_e),
        grid_spec=grid_spec,
        out_shape=jax.ShapeDtypeStruct((n_pad * SUBLANES, LANES), F32),
        compiler_params=_params(("arbitrary",)),
        name="moe_experts",
    )(blk_idx, blk_e, n_used, xs, w1, b1.reshape(n_e, 1, d2), w2, b2.reshape(n_e, 1, d))


def _combine_kernel(dest_ref, dnext_ref, ys_ref, x1_ref, gw_ref, o_ref, buf, sems):
    i = pl.program_id(0)
    nt = pl.num_programs(0)
    tm = x1_ref.shape[0]
    slot = i % 2

    def row_copy(d_ref, sl, t, kq):
        return pltpu.make_async_copy(
            ys_ref.at[d_ref[0, kq * tm + t]],
            buf.at[sl, kq, pl.ds(pl.multiple_of(t * SUBLANES, SUBLANES), SUBLANES)], sems.at[sl])

    def issue_all(d_ref, sl):
        def body(t, c):
            for kq in range(TOP_K):
                row_copy(d_ref, sl, t, kq).start(priority=kq % 2)
            return c
        lax.fori_loop(0, tm, body, 0, unroll=8)

    @pl.when(i == 0)
    def _():
        issue_all(dest_ref, 0)

    @pl.when(i + 1 < nt)
    def _():
        issue_all(dnext_ref, 1 - slot)

    def drain(t, c):
        for kq in range(TOP_K):
            row_copy(dest_ref, slot, t, kq).wait()
        return c

    lax.fori_loop(0, tm, drain, 0, unroll=8)
    gw = gw_ref[...]
    for s in range(SUBLANES):
        acc = x1_ref[:, s * LANES:(s + 1) * LANES]
        for kq in range(TOP_K):
            acc = acc + gw[:, kq:kq + 1] * buf[slot, kq, pl.ds(s, tm, stride=SUBLANES), :]
        o_ref[:, s * LANES:(s + 1) * LANES] = acc


def _combine(dest, ys, x1, gw, tm=128):
    m, d = x1.shape
    nt = m // tm
    return pl.pallas_call(
        _combine_kernel,
        grid=(nt,),
        in_specs=[
            pl.BlockSpec((None, 1, TOP_K * tm), lambda i: (i, 0, 0), memory_space=pltpu.SMEM),
            pl.BlockSpec((None, 1, TOP_K * tm), lambda i: (jnp.minimum(i + 1, nt - 1), 0, 0),
                         memory_space=pltpu.SMEM),
            pl.BlockSpec(memory_space=pl.ANY),
            pl.BlockSpec((tm, d), lambda i: (i, 0)),
            pl.BlockSpec((tm, LANES), lambda i: (i, 0)),
        ],
        out_specs=pl.BlockSpec((tm, d), lambda i: (i, 0)),
        out_shape=jax.ShapeDtypeStruct((m, d), F32),
        scratch_shapes=[pltpu.VMEM((2, TOP_K, tm * SUBLANES, LANES), F32), pltpu.SemaphoreType.DMA((2,))],
        compiler_params=_params(("arbitrary",)),
        name="moe_combine",
    )(_tile_major(dest, tm), _tile_major(dest, tm), ys, x1, gw)


def _blockdiag2(a, b):
    za = jnp.zeros((a.shape[0], b.shape[1]), a.dtype)
    zb = jnp.zeros((b.shape[0], a.shape[1]), a.dtype)
    return jnp.concatenate([jnp.concatenate([a, za], axis=1), jnp.concatenate([zb, b], axis=1)], axis=0)


def _layer(x, g_mix, w_in, mu_prev, mu_next, w0_f, w2_f, w0_b, w2_b, a0_f, a2_f, a0_b, a2_b,
           g2, k_k, k_a, r_k, lnx_g, lnx_b, q_norm_g, k_norm_g, rpb, w_a, w_b, w_o,
           g_ffn, w_router, b_router, w1, b1, w2, b2):
    bsz, t, d_model = x.shape
    m = bsz * t
    d_a = w_a.shape[0]
    d_b = w_b.shape[0]
    a_cols = mu_prev.shape[0]
    b_cols = 3 * d_b
    row = lambda a: a.reshape(1, -1).astype(F32)

    w_in_b = w_in.astype(BF16)
    pa, pb, gates = _inproj(x.reshape(m, d_model), row(g_mix), w_in_b[:, :a_cols],
                            w_in_b[:, a_cols:a_cols + b_cols], w_in_b[:, a_cols + b_cols:])

    prep = _prep(pa.reshape(bsz, t, a_cols), row(mu_prev), row(mu_next),
                 jnp.concatenate([row(w0_f), row(w0_b)], axis=1), _blockdiag2(w2_f, w2_b),
                 jnp.concatenate([row(a0_f), row(a0_b)], axis=1), _blockdiag2(a2_f, a2_b),
                 g2, row(k_k), row(k_a), row(r_k), d_a)
    r, v, kk, lw_f, lw_b, k_f, k_b, b_f, b_b, bonus, g = prep
    o_f, o_b = _scan(r, v, kk, lw_f, lw_b, k_f, k_b, b_f, b_b)

    yb = _na(pb.reshape(bsz, t, b_cols), q_norm_g, k_norm_g, _na_bias_table(rpb), d_b)

    n_e = w_router.shape[1]
    wr_pad = jnp.zeros((d_model, LANES), F32).at[:, :n_e].set(w_router)
    br_pad = jnp.full((1, LANES), NEG_BIG, F32).at[0, :n_e].set(b_router)
    x1, h2, idx, rank, gw, cnt = _merge(
        o_f, o_b, bonus, g, yb, gates.reshape(bsz, t, 2 * d_model), x,
        row(lnx_g), row(lnx_b), w_a.astype(BF16), w_b.astype(BF16), w_o.astype(BF16),
        row(g_ffn), wr_pad, br_pad)

    counts = cnt[0, :n_e].astype(jnp.int32)
    padded = ((counts + MOE_BLOCK - 1) // MOE_BLOCK) * MOE_BLOCK
    pend = jnp.cumsum(padded)
    pstart = pend - padded
    n_assign = m * TOP_K
    n_blocks = -(-n_assign // MOE_BLOCK) + n_e
    n_pad = n_blocks * MOE_BLOCK
    dest = rank[:TOP_K]
    for e in range(n_e):
        dest = dest + jnp.where(idx[:TOP_K] == e, pstart[e], 0)
    dest = dest.astype(jnp.int32)
    n_used = (pend[-1] // MOE_BLOCK).astype(jnp.int32)
    blk_idx = jnp.minimum(jnp.arange(n_blocks, dtype=jnp.int32), n_used - 1)
    blk_e = jnp.sum((blk_idx[:, None] * MOE_BLOCK >= pend[None, :]).astype(jnp.int32), axis=1)
    blk_e = jnp.minimum(blk_e, n_e - 1)

    xs = _dispatch(dest, (pstart + counts).astype(jnp.int32), (padded - counts).astype(jnp.int32),
                   n_used.reshape(1), h2.reshape(m, SUBLANES, LANES), n_pad)
    ys = _experts(blk_idx, blk_e, n_used.reshape(1), xs.reshape(n_pad * SUBLANES, LANES), w1, b1, w2, b2)
    out = _combine(dest, ys.reshape(n_pad, SUBLANES, LANES), x1, gw)
    return out.reshape(bsz, t, d_model)


def kernel(x, g_mix, w_in, mu_prev, mu_next, w0_f, w2_f, w0_b, w2_b, a0_f, a2_f, a0_b, a2_b, g2, k_k, k_a, r_k, lnx_g, lnx_b, q_norm_g, k_norm_g, rpb, w_a, w_b, w_o, g_ffn, w_router, b_router, w1, b1, w2, b2):
    for l in range(g_mix.shape[0]):
        x = _layer(x, g_mix[l], w_in[l], mu_prev[l], mu_next[l], w0_f[l], w2_f[l], w0_b[l], w2_b[l],
                   a0_f[l], a2_f[l], a0_b[l], a2_b[l], g2[l], k_k[l], k_a[l], r_k[l], lnx_g[l], lnx_b[l],
                   q_norm_g[l], k_norm_g[l], rpb[l], w_a[l], w_b[l], w_o[l], g_ffn[l], w_router[l],
                   b_router[l], w1[l], b1[l], w2[l], b2[l])
    return x
```
